```python
import math
import jax
import jax.numpy as jnp
from jax import lax
import numpy as np

D_MODEL = 2048
BATCH = 4
SEQ = 2048
DEPTH = 2

N_EVEN = (DEPTH + 1) // 2
N_ODD = DEPTH // 2
MIX_W = D_MODEL
NORM_EPS = 1e-6

RW_HEAD = 64
RW_W = MIX_W // 2
RW_H = RW_W // RW_HEAD
RW_DECAY_LORA = 96
RW_ICLR_LORA = 96
RW_GATE_LORA = 256
RW_IN = 3 * RW_W + RW_DECAY_LORA + RW_ICLR_LORA + RW_GATE_LORA
RW_LN_EPS = 64e-5
RW_DECAY_SCALE = math.exp(-0.5)

ML_W = MIX_W - RW_W
ML_H = 4
ML_DV = ML_W // ML_H
ML_DK = ML_DV // 2
ML_QK = ML_H * ML_DK
ML_CONV = 4
ML_CHUNK = 64
ML_IN = 2 * ML_QK + 2 * ML_W + 2 * ML_H
EV_IN = RW_IN + ML_IN

HG_W = MIX_W // 2
HG_HEAD = 128
HG_H = HG_W // HG_HEAD
HG_CHUNK = 64
HG_IN = 4 * HG_W

MB_W = MIX_W - HG_W
MB_HEAD = 64
MB_H = MB_W // MB_HEAD
MB_G = 4
MB_E = MB_H // MB_G
MB_N = 128
MB_CONV = 4
MB_CHUNK = 64
MB_CONV_W = MB_W + 2 * MB_G * MB_N
MB_IN = MB_W + MB_CONV_W + MB_H
OD_IN = HG_IN + MB_IN

FFN_DENSE = 5632
N_EXPERTS = 8
TOP_K = 2
FFN_EXPERT = 2816

kernel_name = 'hybrid_rwkv7_mlstm_hgrn2_mamba2_moe'


def rmsnorm(x, w, eps=NORM_EPS):
    xf = x.astype(jnp.float32)
    y = xf * lax.rsqrt(jnp.mean(xf * xf, axis=-1, keepdims=True) + eps)
    return (y * w.astype(jnp.float32)).astype(x.dtype)


def head_rmsnorm(x, w):
    H, d = x.shape[-2:]
    return rmsnorm(x, w.reshape(H, d))


def split_cols(u, sizes):
    out, off = [], 0
    for s in sizes:
        out.append(u[..., off:off + s])
        off += s
    return out


def token_shift(x):
    return jnp.pad(x, ((0, 0), (1, 0), (0, 0)))[:, :-1]


def causal_dwconv(x, w, b):
    K, C = w.shape
    y = lax.conv_general_dilated(x, w.astype(x.dtype)[:, None, :], window_strides=(1,),
                                 padding=[(K - 1, 0)], dimension_numbers=('NWC', 'WIO', 'NWC'),
                                 feature_group_count=C)
    return y + b.astype(x.dtype)


def segsum(a):
    L = a.shape[-1]
    xr = jnp.broadcast_to(a[..., :, None], a.shape + (L,))
    xr = jnp.where(jnp.tril(jnp.ones((L, L), bool), -1), xr, 0.0)
    xr = jnp.cumsum(xr, axis=-2)
    return jnp.where(jnp.tril(jnp.ones((L, L), bool)), xr, -jnp.inf)


def rwkv7_group(u, mu, w0, w2, a0, a2, g2, k_k, k_a, r_k, ln_w, ln_b):
    Bsz, T, _ = u.shape
    uf = u.astype(jnp.float32)
    uf = uf + (token_shift(uf) - uf) * mu
    r, k, v, dw, da, dg = split_cols(uf, [RW_W, RW_W, RW_W, RW_DECAY_LORA, RW_ICLR_LORA, RW_GATE_LORA])
    log_w = -RW_DECAY_SCALE * jax.nn.sigmoid(w0 + jnp.tanh(dw) @ w2)
    a = jax.nn.sigmoid(a0 + da @ a2)
    g = jax.nn.sigmoid(dg) @ g2
    hs = lambda t: t.reshape(Bsz, T, RW_H, RW_HEAD)
    kk = hs(k * k_k)
    kk = kk * lax.rsqrt(jnp.maximum(jnp.sum(kk * kk, -1, keepdims=True), 1e-12))
    k = k * (1.0 + (a - 1.0) * k_a)
    r_h, k_h, v_h, a_h, w_h = hs(r), hs(k), hs(v), hs(a), jnp.exp(hs(log_w))

    def step(S, inp):
        r_t, w_t, k_t, v_t, kk_t, a_t = inp
        sa = jnp.einsum('bhij,bhj->bhi', S, -kk_t)
        S = (S * w_t[:, :, None, :] + sa[..., None] * (kk_t * a_t)[:, :, None, :]
             + v_t[..., None] * k_t[:, :, None, :])
        return S, jnp.einsum('bhij,bhj->bhi', S, r_t)

    tm = lambda t: jnp.moveaxis(t, 1, 0)
    S0 = jnp.zeros((Bsz, RW_H, RW_HEAD, RW_HEAD), jnp.float32)
    _, y = lax.scan(step, S0, (tm(r_h), tm(w_h), tm(k_h), tm(v_h), tm(kk), tm(a_h)))
    y = jnp.moveaxis(y, 0, 1)
    mean = jnp.mean(y, -1, keepdims=True)
    var = jnp.mean(jnp.square(y - mean), -1, keepdims=True)
    y = (y - mean) * lax.rsqrt(var + RW_LN_EPS) * ln_w.reshape(RW_H, RW_HEAD) + ln_b.reshape(RW_H, RW_HEAD)
    y = y + jnp.sum(r_h * k_h * r_k, -1, keepdims=True) * v_h
    return (y.reshape(Bsz, T, RW_W) * g).astype(u.dtype)


def mlstm_chunkwise(q, k, v, i_log, f_log):
    Bsz, H, T, DK = q.shape
    DV = v.shape[-1]
    L = ML_CHUNK
    NC = T // L
    q = q.reshape(Bsz, H, NC, L, DK)
    k = k.reshape(Bsz, H, NC, L, DK)
    v = v.reshape(Bsz, H, NC, L, DV)
    i_log = i_log.reshape(Bsz, H, NC, L)
    b = jnp.cumsum(f_log.reshape(Bsz, H, NC, L), -1)
    g = b[..., -1]
    w_state = g[..., None] - b + i_log
    a_loc = jnp.max(w_state, -1)
    e = jnp.exp(w_state - a_loc[..., None])
    kv_loc = jnp.einsum('bhcl,bhclk,bhclv->bhckv', e, k, v)
    n_loc = jnp.einsum('bhcl,bhclk->bhck', e, k)

    def step(carry, inp):
        C, n, m = carry
        g_c, a_c, kv_c, n_c = inp
        m_new = jnp.maximum(g_c + m, a_c)
        s_old = jnp.exp(g_c + m - m_new)
        s_loc = jnp.exp(a_c - m_new)
        C_new = s_old[..., None, None] * C + s_loc[..., None, None] * kv_c
        n_new = s_old[..., None] * n + s_loc[..., None] * n_c
        return (C_new, n_new, m_new), (C, n, m)

    init = (jnp.zeros((Bsz, H, DK, DV), jnp.float32), jnp.zeros((Bsz, H, DK), jnp.float32),
            jnp.zeros((Bsz, H), jnp.float32))
    cm = lambda t: jnp.moveaxis(t, 2, 0)
    _, (C_prev, n_prev, m_prev) = lax.scan(step, init, (cm(g), cm(a_loc), cm(kv_loc), cm(n_loc)))
    C_prev = jnp.moveaxis(C_prev, 0, 2)
    n_prev = jnp.moveaxis(n_prev, 0, 2)
    m_prev = jnp.moveaxis(m_prev, 0, 2)
    causal = jnp.tril(jnp.ones((L, L), bool))
    D = jnp.where(causal, b[..., :, None] - b[..., None, :] + i_log[..., None, :], -jnp.inf)
    inter_log = b + m_prev[..., None]
    m_t = jnp.maximum(inter_log, jnp.max(D, -1))
    inter_s = jnp.exp(inter_log - m_t)
    qk = jnp.einsum('bhctk,bhcsk->bhcts', q, k) * jnp.exp(D - m_t[..., None])
    num = inter_s[..., None] * jnp.einsum('bhctk,bhckv->bhctv', q, C_prev) + jnp.einsum('bhcts,bhcsv->bhctv', qk, v)
    den = inter_s * jnp.einsum('bhctk,bhck->bhct', q, n_prev) + jnp.sum(qk, -1)
    h = num / jnp.maximum(jnp.abs(den), jnp.exp(-m_t))[..., None]
    return h.reshape(Bsz, H, T, DV)


def mlstm_group(u, conv_w, conv_b, i_b, f_b, norm_w):
    Bsz, T, _ = u.shape
    qk, v, o, i_pre, f_pre = split_cols(u, [2 * ML_QK, ML_W, ML_W, ML_H, ML_H])
    qk = jax.nn.silu(causal_dwconv(qk, conv_w, conv_b)).astype(jnp.float32)
    heads = lambda t, d: jnp.moveaxis(t.reshape(Bsz, T, ML_H, d), 2, 1)
    q = heads(qk[..., :ML_QK], ML_DK) * (ML_DK ** -0.5)
    k = heads(qk[..., ML_QK:], ML_DK)
    v = heads(v.astype(jnp.float32), ML_DV)
    i_log = jnp.moveaxis(i_pre.astype(jnp.float32) + i_b, 2, 1)
    f_log = jax.nn.log_sigmoid(jnp.moveaxis(f_pre.astype(jnp.float32) + f_b, 2, 1))
    h = mlstm_chunkwise(q, k, v, i_log, f_log)
    h = head_rmsnorm(jnp.moveaxis(h, 1, 2), norm_w).reshape(Bsz, T, ML_W)
    return (h * jax.nn.sigmoid(o.astype(jnp.float32))).astype(u.dtype)


def hgrn2_chunkwise(q, k, i, log_f):
    Bsz, H, T, DK = q.shape
    DV = i.shape[-1]
    L = HG_CHUNK
    NC = T // L
    to_chunks = lambda t: jnp.moveaxis(t.reshape(Bsz, H, NC, L, t.shape[-1]), 2, 0)
    causal = jnp.tril(jnp.ones((L, L), bool))[:, :, None]

    def step(S, inp):
        q_c, k_c, i_c, lf_c = inp
        bcum = jnp.cumsum(lf_c, axis=2)
        o_inter = jnp.einsum('bhtk,bhkv->bhtv', q_c * jnp.exp(bcum), S)
        diff = bcum[:, :, :, None, :] - bcum[:, :, None, :, :]
        decay = jnp.exp(jnp.where(causal, diff, -jnp.inf))
        attn = jnp.einsum('bhtk,bhsk,bhtsk->bhts', q_c, k_c, decay)
        o_intra = jnp.einsum('bhts,bhsv->bhtv', attn, i_c)
        b_last = bcum[:, :, -1, :]
        S_new = (jnp.exp(b_last)[..., None] * S
                 + jnp.einsum('bhsk,bhsv->bhkv', k_c * jnp.exp(b_last[:, :, None, :] - bcum), i_c))
        return S_new, o_inter + o_intra

    S0 = jnp.zeros((Bsz, H, DK, DV), jnp.float32)
    _, o = lax.scan(step, S0, (to_chunks(q), to_chunks(k), to_chunks(i), to_chunks(log_f)))
    return jnp.moveaxis(o, 0, 2).reshape(Bsz, H, T, DV)


def hgrn2_group(u, lb, norm_w):
    Bsz, T, _ = u.shape
    q, f_pre, i, g = split_cols(u.astype(jnp.float32), [HG_W, HG_W, HG_W, HG_W])
    q = jax.nn.silu(q)
    lb = lb.astype(jnp.float32)
    log_f = jnp.logaddexp(jnp.log(lb), jnp.log1p(-lb) + jax.nn.log_sigmoid(f_pre))
    k = (1.0 - lb) * jax.nn.sigmoid(-f_pre)
    heads = lambda t: jnp.moveaxis(t.reshape(Bsz, T, HG_H, HG_HEAD), 2, 1)
    o = hgrn2_chunkwise(heads(q), heads(k), heads(i), heads(log_f))
    o = head_rmsnorm(jnp.moveaxis(o, 1, 2), norm_w).reshape(Bsz, T, HG_W)
    return (o * jax.nn.silu(g)).astype(u.dtype)


def ssd_chunked(X, Adt, Bm, Cm):
    Bsz, T, G, E, P = X.shape
    N = Bm.shape[-1]
    L = MB_CHUNK
    NC = T // L
    X = X.reshape(Bsz, NC, L, G, E, P)
    Bm = Bm.reshape(Bsz, NC, L, G, N)
    Cm = Cm.reshape(Bsz, NC, L, G, N)
    Adt = jnp.moveaxis(Adt.reshape(Bsz, NC, L, G, E), (1, 2), (3, 4))
    A_cum = jnp.cumsum(Adt, -1)
    Lmat = jnp.exp(segsum(Adt))
    CB = jnp.einsum('bclgn,bcsgn->bgcls', Cm, Bm)
    Y_diag = jnp.einsum('bgecls,bcsgep->bclgep', CB[:, :, None] * Lmat, X)
    decay_states = jnp.exp(A_cum[..., -1:] - A_cum)
    states = jnp.einsum('bclgn,bgecl,bclgep->bcgepn', Bm, decay_states, X)
    states = jnp.concatenate([jnp.zeros_like(states[:, :1]), states], axis=1)
    decay_chunk = jnp.exp(segsum(jnp.pad(A_cum[..., -1], ((0, 0), (0, 0), (0, 0), (1, 0)))))
    states = jnp.einsum('bgezc,bcgepn->bzgepn', decay_chunk, states)[:, :-1]
    Y_off = jnp.einsum('bclgn,bcgepn,bgecl->bclgep', Cm, states, jnp.exp(A_cum))
    return (Y_diag + Y_off).reshape(Bsz, T, G, E, P)


def mamba2_group(u, conv_w, conv_b, dt_bias, A_log, D_skip, norm_w):
    Bsz, T, _ = u.shape
    z, xBC, dt = split_cols(u, [MB_W, MB_CONV_W, MB_H])
    xBC = jax.nn.silu(causal_dwconv(xBC, conv_w, conv_b)).astype(jnp.float32)
    xs, Bm, Cm = split_cols(xBC, [MB_W, MB_G * MB_N, MB_G * MB_N])
    dt = jax.nn.softplus(dt.astype(jnp.float32) + dt_bias)
    A = -jnp.exp(A_log.astype(jnp.float32))
    xh = xs.reshape(Bsz, T, MB_G, MB_E, MB_HEAD)
    dth = dt.reshape(Bsz, T, MB_G, MB_E)
    y = ssd_chunked(xh * dth[..., None], (dt * A).reshape(Bsz, T, MB_G, MB_E),
                    Bm.reshape(Bsz, T, MB_G, MB_N), Cm.reshape(Bsz, T, MB_G, MB_N))
    y = y + xh * D_skip.reshape(MB_G, MB_E)[..., None]
    y = y.reshape(Bsz, T, MB_W) * jax.nn.silu(z.astype(jnp.float32))
    y = rmsnorm(y.reshape(Bsz, T, MB_G, MB_W // MB_G), norm_w.reshape(MB_G, MB_W // MB_G))
    return y.reshape(Bsz, T, MB_W).astype(u.dtype)


def swiglu(h, w_gate, w_up, w_down):
    return (jax.nn.silu(h @ w_gate) * (h @ w_up)) @ w_down


def moe_swiglu(h, router, w_gate, w_up, w_down):
    logits = (h @ router).astype(jnp.float32)
    top_val, top_idx = lax.top_k(logits, TOP_K)
    top_p = jax.nn.softmax(top_val, axis=-1)
    gates = jnp.einsum('btk,btke->bte', top_p, jax.nn.one_hot(top_idx, N_EXPERTS, dtype=jnp.float32))
    out = jnp.zeros_like(h)
    for e in range(N_EXPERTS):
        out = out + gates[..., e:e + 1].astype(h.dtype) * swiglu(h, w_gate[e], w_up[e], w_down[e])
    return out


def setup_inputs(seed: int = 0) -> dict:
    key = jax.random.key(seed)
    ks = iter(jax.random.split(key, 64))
    nrm = lambda shape, scale: scale * jax.random.normal(next(ks), shape, jnp.float32)
    unif = lambda shape, lo, hi: jax.random.uniform(next(ks), shape, jnp.float32, lo, hi)
    gain = lambda shape: 1.0 + nrm(shape, 0.02)
    E, O = N_EVEN, N_ODD
    x = nrm((BATCH, SEQ, D_MODEL), 1.0)
    final_norm_w = gain((D_MODEL,))
    hg_lb_logits = nrm((DEPTH, HG_W), 0.3)
    ev_norm1_w = gain((E, D_MODEL))
    ev_w_in = nrm((E, D_MODEL, EV_IN), D_MODEL ** -0.5)
    ev_w_out = nrm((E, MIX_W, D_MODEL), MIX_W ** -0.5)
    rw_mu = unif((E, RW_IN), 0.0, 1.0)
    rw_w0 = jnp.linspace(-6.0, -0.5, RW_W)[None] + nrm((E, RW_W), 0.1)
    rw_w2 = nrm((E, RW_DECAY_LORA, RW_W), 0.5 * RW_DECAY_LORA ** -0.5)
    rw_a0 = nrm((E, RW_W), 0.1)
    rw_a2 = nrm((E, RW_ICLR_LORA, RW_W), RW_ICLR_LORA ** -0.5)
    rw_g2 = nrm((E, RW_GATE_LORA, RW_W), RW_GATE_LORA ** -0.5)
    rw_k_k = 0.85 + nrm((E, RW_W), 0.02)
    rw_k_a = 1.0 + nrm((E, RW_W), 0.02)
    rw_r_k = nrm((E, RW_H, RW_HEAD), 0.1)
    rw_ln_w = gain((E, RW_W))
    rw_ln_b = nrm((E, RW_W), 0.02)
    ml_conv_w = nrm((E, ML_CONV, 2 * ML_QK), 0.5)
    ml_conv_b = nrm((E, 2 * ML_QK), 0.02)
    ml_i_b = nrm((E, ML_H), 0.1)
    ml_f_b = jnp.linspace(3.0, 6.0, ML_H)[None] + nrm((E, ML_H), 0.1)
    ml_norm_w = gain((E, ML_W))
    ev_norm2_w = gain((E, D_MODEL))
    ffn_w_gate = nrm((E, D_MODEL, FFN_DENSE), D_MODEL ** -0.5)
    ffn_w_up = nrm((E, D_MODEL, FFN_DENSE), D_MODEL ** -0.5)
    ffn_w_down = nrm((E, FFN_DENSE, D_MODEL), FFN_DENSE ** -0.5)
    od_norm1_w = gain((O, D_MODEL))
    od_w_in = nrm((O, D_MODEL, OD_IN), D_MODEL ** -0.5)
    od_w_out = nrm((O, MIX_W, D_MODEL), MIX_W ** -0.5)
    hg_norm_w = gain((O, HG_W))
    mb_conv_w = nrm((O, MB_CONV, MB_CONV_W), 0.5)
    mb_conv_b = nrm((O, MB_CONV_W), 0.02)
    dt0 = jnp.exp(unif((O, MB_H), math.log(1e-3), math.log(1e-1)))
    mb_dt_bias = dt0 + jnp.log(-jnp.expm1(-dt0))
    mb_A_log = jnp.log(unif((O, MB_H), 1.0, 16.0))
    mb_D = gain((O, MB_H))
    mb_norm_w = gain((O, MB_W))
    od_norm2_w = gain((O, D_MODEL))
    moe_router = nrm((O, D_MODEL, N_EXPERTS), D_MODEL ** -0.5)
    moe_w_gate = nrm((O, N_EXPERTS, D_MODEL, FFN_EXPERT), D_MODEL ** -0.5)
    moe_w_up = nrm((O, N_EXPERTS, D_MODEL, FFN_EXPERT), D_MODEL ** -0.5)
    moe_w_down = nrm((O, N_EXPERTS, FFN_EXPERT, D_MODEL), FFN_EXPERT ** -0.5)
    return {'x': x, 'final_norm_w': final_norm_w, 'hg_lb_logits': hg_lb_logits,
            'ev_norm1_w': ev_norm1_w, 'ev_w_in': ev_w_in, 'ev_w_out': ev_w_out,
            'rw_mu': rw_mu, 'rw_w0': rw_w0, 'rw_w2': rw_w2, 'rw_a0': rw_a0, 'rw_a2': rw_a2,
            'rw_g2': rw_g2, 'rw_k_k': rw_k_k, 'rw_k_a': rw_k_a, 'rw_r_k': rw_r_k,
            'rw_ln_w': rw_ln_w, 'rw_ln_b': rw_ln_b,
            'ml_conv_w': ml_conv_w, 'ml_conv_b': ml_conv_b, 'ml_i_b': ml_i_b, 'ml_f_b': ml_f_b,
            'ml_norm_w': ml_norm_w,
            'ev_norm2_w': ev_norm2_w, 'ffn_w_gate': ffn_w_gate, 'ffn_w_up': ffn_w_up, 'ffn_w_down': ffn_w_down,
            'od_norm1_w': od_norm1_w, 'od_w_in': od_w_in, 'od_w_out': od_w_out,
            'hg_norm_w': hg_norm_w,
            'mb_conv_w': mb_conv_w, 'mb_conv_b': mb_conv_b, 'mb_dt_bias': mb_dt_bias,
            'mb_A_log': mb_A_log, 'mb_D': mb_D, 'mb_norm_w': mb_norm_w,
            'od_norm2_w': od_norm2_w, 'moe_router': moe_router, 'moe_w_gate': moe_w_gate,
            'moe_w_up': moe_w_up, 'moe_w_down': moe_w_down}


def reference(x, final_norm_w, hg_lb_logits,
              ev_norm1_w, ev_w_in, ev_w_out,
              rw_mu, rw_w0, rw_w2, rw_a0, rw_a2, rw_g2, rw_k_k, rw_k_a, rw_r_k, rw_ln_w, rw_ln_b,
              ml_conv_w, ml_conv_b, ml_i_b, ml_f_b, ml_norm_w,
              ev_norm2_w, ffn_w_gate, ffn_w_up, ffn_w_down,
              od_norm1_w, od_w_in, od_w_out,
              hg_norm_w,
              mb_conv_w, mb_conv_b, mb_dt_bias, mb_A_log, mb_D, mb_norm_w,
              od_norm2_w, moe_router, moe_w_gate, moe_w_up, moe_w_down):
    p = jax.nn.softmax(hg_lb_logits.astype(jnp.float32), axis=0)
    lower_bounds = jnp.cumsum(p, axis=0) - p[0]
    for layer in range(DEPTH):
        j = layer // 2
        if layer % 2 == 0:
            u = rmsnorm(x, ev_norm1_w[j]) @ ev_w_in[j]
            y_a = rwkv7_group(u[..., :RW_IN], rw_mu[j], rw_w0[j], rw_w2[j], rw_a0[j], rw_a2[j], rw_g2[j],
                              rw_k_k[j], rw_k_a[j], rw_r_k[j], rw_ln_w[j], rw_ln_b[j])
            y_b = mlstm_group(u[..., RW_IN:], ml_conv_w[j], ml_conv_b[j], ml_i_b[j], ml_f_b[j], ml_norm_w[j])
            x = x + jnp.concatenate([y_a, y_b], axis=-1) @ ev_w_out[j]
            x = x + swiglu(rmsnorm(x, ev_norm2_w[j]), ffn_w_gate[j], ffn_w_up[j], ffn_w_down[j])
        else:
            u = rmsnorm(x, od_norm1_w[j]) @ od_w_in[j]
            y_c = hgrn2_group(u[..., :HG_IN], lower_bounds[layer], hg_norm_w[j])
            y_d = mamba2_group(u[..., HG_IN:], mb_conv_w[j], mb_conv_b[j], mb_dt_bias[j], mb_A_log[j],
                               mb_D[j], mb_norm_w[j])
            x = x + jnp.concatenate([y_c, y_d], axis=-1) @ od_w_out[j]
            x = x + moe_swiglu(rmsnorm(x, od_norm2_w[j]), moe_router[j], moe_w_gate[j], moe_w_up[j], moe_w_down[j])
    return rmsnorm(x, final_norm_w)
```

```python
import functools
import math

import jax
import jax.numpy as jnp
from jax import lax
from jax.experimental import pallas as pl
from jax.experimental.pallas import tpu as pltpu

F32 = jnp.float32
BF16 = jnp.bfloat16

D_MODEL = 2048
NORM_EPS = 1e-6
RW_HEAD = 64
RW_W = 1024
RW_H = RW_W // RW_HEAD
RW_DECAY_LORA = 96
RW_ICLR_LORA = 96
RW_GATE_LORA = 256
RW_IN = 3 * RW_W + RW_DECAY_LORA + RW_ICLR_LORA + RW_GATE_LORA
RW_LN_EPS = 64e-5
RW_DECAY_SCALE = math.exp(-0.5)
ML_W = 1024
ML_H = 4
ML_DV = ML_W // ML_H
ML_DK = ML_DV // 2
ML_QK = ML_H * ML_DK
ML_IN = 2 * ML_QK + 2 * ML_W + 2 * ML_H
HG_W = 1024
HG_HEAD = 128
HG_H = HG_W // HG_HEAD
HG_IN = 4 * HG_W
MB_W = 1024
MB_HEAD = 64
MB_H = MB_W // MB_HEAD
MB_G = 4
MB_N = 128
MB_CONV_W = MB_W + 2 * MB_G * MB_N
N_EXPERTS = 8

LANES = 128
SUBLANES = 8
VMEM_LIMIT = 56 * 1024 * 1024

TM_PROJ = 1024
TN_PROJ = 512
TM_FFN = 512
TF_FFN = 512
TF_MOE = 256
RW_L = 64
RW_BLK = 16
ML_L = 128
HG_L = 64
MB_L = 128
CONV_K = 4


def _cparams(sem):
    return pltpu.CompilerParams(dimension_semantics=sem, vmem_limit_bytes=VMEM_LIMIT)


def _dot(a, b):
    return jnp.dot(a, b, preferred_element_type=F32)


def _dot_nt(a, b):
    return lax.dot_general(a, b, (((1,), (1,)), ((), ())), preferred_element_type=F32)


def _dot_tn(a, b):
    return lax.dot_general(a, b, (((0,), (0,)), ((), ())), preferred_element_type=F32)


def _bf(x):
    return x.astype(BF16)


def _split2(x):
    h = x.astype(BF16)
    l = (x - h.astype(F32)).astype(BF16)
    return h, l


def _split3(x):
    h = x.astype(BF16)
    r = x - h.astype(F32)
    m = r.astype(BF16)
    l = (r - m.astype(F32)).astype(BF16)
    return h, m, l


def _dot_sel(sel_bf16, x):
    h, m, l = _split3(x)
    return _dot(sel_bf16, h) + _dot(sel_bf16, m) + _dot(sel_bf16, l)


def _dot_sel_r(x, sel_bf16):
    h, m, l = _split3(x)
    return _dot(h, sel_bf16) + _dot(m, sel_bf16) + _dot(l, sel_bf16)


def _dot3(a, b):
    ah, al = _split2(a)
    bh, bl = _split2(b)
    return _dot(ah, bh) + _dot(ah, bl) + _dot(al, bh)


def _sigmoid(x):
    return 1.0 / (1.0 + jnp.exp(-x))


def _silu(x):
    return x * _sigmoid(x)


def _log_sigmoid(x):
    return -(jnp.maximum(-x, 0.0) + jnp.log1p(jnp.exp(-jnp.abs(x))))


def _softplus(x):
    return jnp.maximum(x, 0.0) + jnp.log1p(jnp.exp(-jnp.abs(x)))


def _iota(shape, dim):
    return lax.broadcasted_iota(jnp.int32, shape, dim)


def _idiv(x, d):
    sh = d.bit_length() - 1
    assert d == 1 << sh
    return lax.shift_right_logical(x, jnp.int32(sh))


def _shift_rows(tail, x, j):
    xc = jnp.concatenate([tail, x], axis=0)
    return pltpu.roll(xc, j, 0)[SUBLANES:]


def _norm_matmul_kernel(x_ref, nw_ref, w_ref, o_ref, h_ref):
    @pl.when(pl.program_id(1) == 0)
    def _():
        x = x_ref[...]
        ms = jnp.mean(x * x, axis=-1, keepdims=True)
        h_ref[...] = _bf(x * lax.rsqrt(ms + NORM_EPS) * nw_ref[...])

    o_ref[...] = _dot(h_ref[...], w_ref[...])


def _norm_matmul(x, nw, w):
    m, d = x.shape
    n = w.shape[1]
    return pl.pallas_call(
        _norm_matmul_kernel,
        grid=(m // TM_PROJ, n // TN_PROJ),
        in_specs=[pl.BlockSpec((TM_PROJ, d), lambda i, j: (i, 0)),
                  pl.BlockSpec((1, d), lambda i, j: (0, 0)),
                  pl.BlockSpec((d, TN_PROJ), lambda i, j: (0, j))],
        out_specs=pl.BlockSpec((TM_PROJ, TN_PROJ), lambda i, j: (i, j)),
        out_shape=jax.ShapeDtypeStruct((m, n), F32),
        scratch_shapes=[pltpu.VMEM((TM_PROJ, d), BF16)],
        compiler_params=_cparams(("parallel", "arbitrary")),
        name="norm_matmul",
    )(x, nw, w)


def _matmul_res_kernel(y_ref, w_ref, r_ref, o_ref):
    o_ref[...] = r_ref[...] + _dot(y_ref[...], w_ref[...])


def _matmul_res(y, w, res):
    m, k = y.shape
    n = w.shape[1]
    return pl.pallas_call(
        _matmul_res_kernel,
        grid=(m // TM_PROJ, n // TN_PROJ),
        in_specs=[pl.BlockSpec((TM_PROJ, k), lambda i, j: (i, 0)),
                  pl.BlockSpec((k, TN_PROJ), lambda i, j: (0, j)),
                  pl.BlockSpec((TM_PROJ, TN_PROJ), lambda i, j: (i, j))],
        out_specs=pl.BlockSpec((TM_PROJ, TN_PROJ), lambda i, j: (i, j)),
        out_shape=jax.ShapeDtypeStruct((m, n), F32),
        compiler_params=_cparams(("parallel", "arbitrary")),
        name="matmul_res",
    )(y, w, res)


def _ffn_kernel(x_ref, nw_ref, wg_ref, wu_ref, wd_ref, o_ref, h_ref):
    f = pl.program_id(1)

    @pl.when(f == 0)
    def _():
        x = x_ref[...]
        ms = jnp.mean(x * x, axis=-1, keepdims=True)
        h_ref[...] = _bf(x * lax.rsqrt(ms + NORM_EPS) * nw_ref[...])
        o_ref[...] = x

    h = h_ref[...]
    act = _silu(_dot(h, wg_ref[...])) * _dot(h, wu_ref[...])
    o_ref[...] += _dot(_bf(act), wd_ref[...])


def _ffn(x, nw, wg, wu, wd):
    m, d = x.shape
    f = wg.shape[1]
    return pl.pallas_call(
        _ffn_kernel,
        grid=(m // TM_FFN, f // TF_FFN),
        in_specs=[pl.BlockSpec((TM_FFN, d), lambda i, j: (i, 0)),
                  pl.BlockSpec((1, d), lambda i, j: (0, 0)),
                  pl.BlockSpec((d, TF_FFN), lambda i, j: (0, j)),
                  pl.BlockSpec((d, TF_FFN), lambda i, j: (0, j)),
                  pl.BlockSpec((TF_FFN, d), lambda i, j: (j, 0))],
        out_specs=pl.BlockSpec((TM_FFN, d), lambda i, j: (i, 0)),
        out_shape=jax.ShapeDtypeStruct((m, d), F32),
        scratch_shapes=[pltpu.VMEM((TM_FFN, d), BF16)],
        compiler_params=_cparams(("parallel", "arbitrary")),
        name="ffn_swiglu",
    )(x, nw, wg, wu, wd)


def _router_kernel(x_ref, nw_ref, wr_ref, g_ref, h_ref):
    x = x_ref[...]
    ms = jnp.mean(x * x, axis=-1, keepdims=True)
    h = x * lax.rsqrt(ms + NORM_EPS) * nw_ref[...]
    h_ref[...] = _bf(h)
    wr = wr_ref[...]
    hh, hl = _split2(h)
    wh, wl = _split2(wr)
    logits = _dot(hh, wh) + _dot(hh, wl) + _dot(hl, wh)
    lane = _iota(logits.shape, 1)
    neg = jnp.float32(-jnp.inf)
    logits = jnp.where(lane < N_EXPERTS, logits, neg)
    v1 = jnp.max(logits, axis=-1, keepdims=True)
    i1 = jnp.min(jnp.where(logits == v1, lane, LANES), axis=-1, keepdims=True)
    rest = jnp.where(lane == i1, neg, logits)
    v2 = jnp.max(rest, axis=-1, keepdims=True)
    i2 = jnp.min(jnp.where(rest == v2, lane, LANES), axis=-1, keepdims=True)
    e2 = jnp.exp(v2 - v1)
    p1 = 1.0 / (1.0 + e2)
    p2 = e2 / (1.0 + e2)
    g_ref[...] = jnp.where(lane == i1, p1, 0.0) + jnp.where(lane == i2, p2, 0.0)


def _router(x, nw, wr):
    m, d = x.shape
    return pl.pallas_call(
        _router_kernel,
        grid=(m // TM_FFN,),
        in_specs=[pl.BlockSpec((TM_FFN, d), lambda i: (i, 0)),
                  pl.BlockSpec((1, d), lambda i: (0, 0)),
                  pl.BlockSpec((d, LANES), lambda i: (0, 0))],
        out_specs=[pl.BlockSpec((TM_FFN, LANES), lambda i: (i, 0)),
                   pl.BlockSpec((TM_FFN, d), lambda i: (i, 0))],
        out_shape=[jax.ShapeDtypeStruct((m, LANES), F32),
                   jax.ShapeDtypeStruct((m, d), BF16)],
        compiler_params=_cparams(("parallel",)),
        name="moe_router",
    )(x, nw, wr)


def _moe_kernel(x_ref, h_ref, g_ref, wg_ref, wu_ref, wd_ref, fw_ref, o_ref):
    e = pl.program_id(1)
    f = pl.program_id(2)

    @pl.when((e == 0) & (f == 0))
    def _():
        o_ref[...] = x_ref[...]

    gates = g_ref[...]
    lane = _iota(gates.shape, 1)
    gate = jnp.sum(jnp.where(lane == e, gates, 0.0), axis=-1, keepdims=True)
    h = h_ref[...]
    act = _silu(_dot(h, wg_ref[0])) * _dot(h, wu_ref[0]) * gate
    o_ref[...] += _dot(_bf(act), wd_ref[0])

    @pl.when((e == pl.num_programs(1) - 1) & (f == pl.num_programs(2) - 1))
    def _():
        y = o_ref[...]
        ms = jnp.mean(y * y, axis=-1, keepdims=True)
        o_ref[...] = y * lax.rsqrt(ms + NORM_EPS) * fw_ref[...]


def _moe(x, h, gates, wg, wu, wd, final_w):
    m, d = x.shape
    ne, _, fe = wg.shape
    return pl.pallas_call(
        _moe_kernel,
        grid=(m // TM_FFN, ne, fe // TF_MOE),
        in_specs=[pl.BlockSpec((TM_FFN, d), lambda i, e, j: (i, 0)),
                  pl.BlockSpec((TM_FFN, d), lambda i, e, j: (i, 0)),
                  pl.BlockSpec((TM_FFN, LANES), lambda i, e, j: (i, 0)),
                  pl.BlockSpec((1, d, TF_MOE), lambda i, e, j: (e, 0, j)),
                  pl.BlockSpec((1, d, TF_MOE), lambda i, e, j: (e, 0, j)),
                  pl.BlockSpec((1, TF_MOE, d), lambda i, e, j: (e, j, 0)),
                  pl.BlockSpec((1, d), lambda i, e, j: (0, 0))],
        out_specs=pl.BlockSpec((TM_FFN, d), lambda i, e, j: (i, 0)),
        out_shape=jax.ShapeDtypeStruct((m, d), F32),
        compiler_params=_cparams(("parallel", "arbitrary", "arbitrary")),
        name="moe_swiglu",
    )(x, h, gates, wg, wu, wd, final_w)


def _rwkv_kernel(r_ref, k_ref, v_ref, dw_ref, da_ref, dg_ref,
                 mur_ref, muk_ref, muv_ref, mudw_ref, muda_ref, mudg_ref,
                 w0_ref, w2_ref, a0_ref, a2_ref, g2_ref, kk_ref, ka_ref, rk_ref, lnw_ref, lnb_ref,
                 o_ref,
                 H_ref, tr_ref, tk_ref, tv_ref, tdw_ref, tda_ref, tdg_ref):
    L = RW_L
    L2 = 2 * L

    @pl.when(pl.program_id(1) == 0)
    def _():
        H_ref[...] = jnp.zeros_like(H_ref)
        for t in (tr_ref, tk_ref, tv_ref, tdw_ref, tda_ref, tdg_ref):
            t[...] = jnp.zeros_like(t)

    def shift_mix(ref, tail, mu):
        x = ref[...]
        prev = _shift_rows(tail[...], x, 1)
        tail[...] = x[L - SUBLANES:]
        return x + (prev - x) * mu[...]

    r = shift_mix(r_ref, tr_ref, mur_ref)
    k = shift_mix(k_ref, tk_ref, muk_ref)
    v = shift_mix(v_ref, tv_ref, muv_ref)
    dw = shift_mix(dw_ref, tdw_ref, mudw_ref)
    da = shift_mix(da_ref, tda_ref, muda_ref)
    dg = shift_mix(dg_ref, tdg_ref, mudg_ref)

    log_w = -RW_DECAY_SCALE * _sigmoid(w0_ref[...] + _dot(_bf(jnp.tanh(dw)), w2_ref[...]))
    a = _sigmoid(a0_ref[...] + _dot(_bf(da), a2_ref[...]))
    g = _dot(_bf(_sigmoid(dg)), g2_ref[...])

    ri = _iota((LANES, LANES), 0)
    ci = _iota((LANES, LANES), 1)
    hsum = jnp.where(_idiv(ri, RW_HEAD) == _idiv(ci, RW_HEAD), 1.0, 0.0).astype(BF16)
    same_head = _idiv(ri, L) == _idiv(ci, L)
    tril_s = jnp.where(same_head & (ri > ci), 1.0, 0.0)
    tril_i = jnp.where(same_head & (ri >= ci), 1.0, 0.0)
    blk_d = jnp.where(_idiv(ri, RW_BLK) == _idiv(ci, RW_BLK), 1.0, 0.0)
    eye = jnp.where(ri == ci, 1.0, 0.0)
    cum = jnp.where(_iota((L, L), 0) >= _iota((L, L), 1), 1.0, 0.0).astype(BF16)
    lane = _iota((1, LANES), 1)
    m0 = lane < RW_HEAD

    def head_sum(x):
        h, l = _split2(x)
        return _dot(h, hsum) + _dot(l, hsum)

    kk = k * kk_ref[...]
    k2 = k * (1.0 + (a - 1.0) * ka_ref[...])
    cw = _dot_sel(cum, log_w)
    e_prev = jnp.exp(cw - log_w)
    e_inv = jnp.exp(-cw)
    e_cw = jnp.exp(cw)
    wl = cw[L - 1:L]
    e_end = jnp.exp(wl - cw)
    e_wl = jnp.exp(wl)
    rk2 = r * k2 * rk_ref[...]

    def stack(x):
        x0 = jnp.where(m0, x, 0.0)
        return jnp.concatenate([x0, x - x0], axis=0)

    def mm3(x, y):
        return _dot3(x, y)

    ys = []
    for p in range(RW_H // 2):
        sl = slice(p * LANES, (p + 1) * LANES)
        kk_p = kk[:, sl]
        kk_p = kk_p * lax.rsqrt(jnp.maximum(head_sum(kk_p * kk_p), 1e-12))
        a_p = a[:, sl]
        k2_p = k2[:, sl]
        kka = kk_p * a_p
        At = _bf(stack(-kk_p * e_prev[:, sl]))
        Bt = _bf(stack(kka * e_inv[:, sl]))
        Kt = _bf(stack(k2_p * e_inv[:, sl]))
        Rt_f = stack(r[:, sl] * e_cw[:, sl])
        Rt = _bf(Rt_f)
        Bh = _bf(stack(kka * e_end[:, sl]))
        Kh = _bf(stack(k2_p * e_end[:, sl]))
        Vs = _bf(stack(v[:, sl]))

        Mab = _dot_nt(At, Bt) * tril_s
        Mak = _dot_nt(At, Kt) * tril_s
        Arb = _dot_nt(Rt, Bt) * tril_i
        Ark = _dot_nt(Rt, Kt) * tril_i

        MD = Mab * blk_d
        Nn = Mab - MD
        P = eye + MD
        S = mm3(MD, MD)
        P = P + mm3(P, S)
        S = mm3(S, S)
        P = P + mm3(P, S)
        S = mm3(S, S)
        P = P + mm3(P, S)
        X = mm3(P, Nn)
        X2 = mm3(X, X)
        Y = eye + X
        Y = Y + mm3(Y, X2)
        T = mm3(Y, P)

        P1 = mm3(T, At.astype(F32))
        P2 = mm3(T, _dot(_bf(Mak), Vs))
        P1b = _bf(P1)
        P2b = _bf(P2)
        G = eye * e_wl[:, sl] + _dot_tn(Bh, P1b)
        J = _dot_tn(Bh, P2b) + _dot_tn(Kh, Vs)
        Q = Rt_f + _dot(_bf(Arb), P1b)
        Z = _dot(_bf(Arb), P2b) + _dot(_bf(Ark), Vs)

        H = H_ref[p]
        Yst = mm3(Q, H) + Z
        H_ref[p] = mm3(G, H) + J
        y_p = Yst[:L] + Yst[L:]

        mean = head_sum(y_p) * (1.0 / RW_HEAD)
        yc = y_p - mean
        var = head_sum(yc * yc) * (1.0 / RW_HEAD)
        yn = yc * lax.rsqrt(var + RW_LN_EPS) * lnw_ref[:, sl] + lnb_ref[:, sl]
        bonus = head_sum(rk2[:, sl]) * v[:, sl]
        ys.append((yn + bonus) * g[:, sl])

    o_ref[...] = _bf(jnp.concatenate(ys, axis=1))


def _rwkv(u, offs, B, T, prm):
    L = RW_L
    nc = T // L

    def col(off, w):
        return pl.BlockSpec((L, w), lambda b, c: (b * nc + c, off // w))

    def par(w, rows=1):
        return pl.BlockSpec((rows, w), lambda b, c: (0, 0))

    in_specs = [col(offs["rw_r"], RW_W), col(offs["rw_k"], RW_W), col(offs["rw_v"], RW_W),
                col(offs["rw_dw"], LANES), col(offs["rw_da"], LANES), col(offs["rw_dg"], RW_GATE_LORA),
                par(RW_W), par(RW_W), par(RW_W), par(LANES), par(LANES), par(RW_GATE_LORA),
                par(RW_W), par(RW_W, LANES), par(RW_W), par(RW_W, LANES), par(RW_W, RW_GATE_LORA),
                par(RW_W), par(RW_W), par(RW_W), par(RW_W), par(RW_W)]
    return pl.pallas_call(
        _rwkv_kernel,
        grid=(B, nc),
        in_specs=in_specs,
        out_specs=pl.BlockSpec((L, RW_W), lambda b, c: (b * nc + c, 0)),
        out_shape=jax.ShapeDtypeStruct((B * T, RW_W), BF16),
        scratch_shapes=[pltpu.VMEM((RW_H // 2, LANES, LANES), F32),
                        pltpu.VMEM((SUBLANES, RW_W), F32), pltpu.VMEM((SUBLANES, RW_W), F32),
                        pltpu.VMEM((SUBLANES, RW_W), F32), pltpu.VMEM((SUBLANES, LANES), F32),
                        pltpu.VMEM((SUBLANES, LANES), F32), pltpu.VMEM((SUBLANES, RW_GATE_LORA), F32)],
        compiler_params=_cparams(("parallel", "arbitrary")),
        name="rwkv7",
    )(u, u, u, u, u, u, *prm)


def _mlstm_kernel(qk_ref, v_ref, o_ref, gc_ref, gr_ref, cw_ref, cb_ref, gbr_ref, gbc_ref, nw_ref,
                  y_ref, C_ref, m_ref, tail_ref):
    L = ML_L
    DK, DV = ML_DK, ML_DV

    @pl.when(pl.program_id(1) == 0)
    def _():
        C_ref[...] = jnp.zeros_like(C_ref)
        m_ref[...] = jnp.zeros_like(m_ref)
        tail_ref[...] = jnp.zeros_like(tail_ref)

    x = qk_ref[...]
    tail = tail_ref[...]
    acc = x * cw_ref[CONV_K - 1:CONV_K] + cb_ref[...]
    for j in range(1, CONV_K):
        acc = acc + _shift_rows(tail, x, j) * cw_ref[CONV_K - 1 - j:CONV_K - j]
    tail_ref[...] = x[L - SUBLANES:]
    qk = _silu(acc)

    gc = gc_ref[...] + gbr_ref[...]
    gr = gr_ref[0] + gbc_ref[...]
    fl_c = _log_sigmoid(gc)
    fl_r = _log_sigmoid(gr)
    ri = _iota((L, L), 0)
    ci = _iota((L, L), 1)
    causal = ri >= ci
    tril = jnp.where(causal, 1.0, 0.0).astype(BF16)
    triu = jnp.where(ri <= ci, 1.0, 0.0).astype(BF16)
    b_c = _dot_sel(tril, fl_c)
    b_r = _dot_sel_r(fl_r, triu)
    one_col = jnp.where(_iota((L, LANES), 1) == 0, 1.0, 0.0)
    neg = jnp.float32(-jnp.inf)
    vv = v_ref[...]
    oo = o_ref[...]

    for h in range(ML_H):
        q_h = _bf(qk[:, h * DK:(h + 1) * DK] * (DK ** -0.5))
        k_f = qk[:, ML_QK + h * DK:ML_QK + (h + 1) * DK]
        k_h = _bf(k_f)
        v_ext = _bf(jnp.concatenate([vv[:, h * DV:(h + 1) * DV], one_col], axis=1))
        bc = b_c[:, ML_H + h:ML_H + h + 1]
        br = b_r[ML_H + h:ML_H + h + 1, :]
        il_c = gc[:, h:h + 1]
        il_r = gr[h:h + 1, :]
        m_prev = m_ref[h:h + 1, 0:1]
        C_prev = C_ref[h]

        D = jnp.where(causal, bc - br + il_r, neg)
        m_t = jnp.maximum(bc + m_prev, jnp.max(D, axis=-1, keepdims=True))
        S = _dot_nt(q_h, k_h) * jnp.exp(D - m_t)
        inter = jnp.exp(bc + m_prev - m_t)
        num = inter * _dot(q_h, _bf(C_prev)) + _dot(_bf(S), v_ext)
        den = num[:, DV:DV + 1]
        hh = num[:, :DV] / jnp.maximum(jnp.abs(den), jnp.exp(-m_t))
        ms = jnp.mean(hh * hh, axis=-1, keepdims=True)
        hn = hh * lax.rsqrt(ms + NORM_EPS) * nw_ref[:, h * DV:(h + 1) * DV]
        y_ref[:, h * DV:(h + 1) * DV] = _bf(hn * _sigmoid(oo[:, h * DV:(h + 1) * DV]))

        g_end = bc[L - 1:L]
        wst = g_end - bc + il_c
        m_new = jnp.maximum(g_end + m_prev, jnp.max(wst, axis=0, keepdims=True))
        kt = _bf(k_f * jnp.exp(wst - m_new))
        C_ref[h] = jnp.exp(g_end + m_prev - m_new) * C_prev + _dot_tn(kt, v_ext)
        m_ref[h:h + 1, :] = jnp.broadcast_to(m_new, (1, LANES))


def _mlstm(u, g_row, offs, B, T, prm):
    L = ML_L
    nc = T // L

    def col(off, w):
        return pl.BlockSpec((L, w), lambda b, c: (b * nc + c, off // w))

    def par(r, w):
        return pl.BlockSpec((r, w), lambda b, c: (0, 0))

    in_specs = [col(offs["ml_qk"], 2 * ML_QK), col(offs["ml_v"], ML_W), col(offs["ml_o"], ML_W),
                col(offs["ml_if"], LANES),
                pl.BlockSpec((1, SUBLANES, L), lambda b, c: (b, 0, c)),
                par(CONV_K, 2 * ML_QK), par(1, 2 * ML_QK), par(1, LANES), par(SUBLANES, 1), par(1, ML_W)]
    return pl.pallas_call(
        _mlstm_kernel,
        grid=(B, nc),
        in_specs=in_specs,
        out_specs=pl.BlockSpec((L, ML_W), lambda b, c: (b * nc + c, 0)),
        out_shape=jax.ShapeDtypeStruct((B * T, ML_W), BF16),
        scratch_shapes=[pltpu.VMEM((ML_H, ML_DK, ML_DV + LANES), F32),
                        pltpu.VMEM((SUBLANES, LANES), F32),
                        pltpu.VMEM((SUBLANES, 2 * ML_QK), F32)],
        compiler_params=_cparams(("parallel", "arbitrary")),
        name="mlstm",
    )(u, u, u, u, g_row, *prm)


def _hgrn_level_matrices():
    L = HG_L
    t = jnp.arange(L)[:, None]
    r = jnp.arange(L)[None, :]
    mats = [(r <= t)]
    for l in range(int(math.log2(L))):
        half = 1 << l
        base = (t // (2 * half)) * (2 * half)
        bnd = base + half - 1
        upper = (t - base) >= half
        m_up = upper & (r > bnd) & (r <= t)
        m_lo = (~upper) & (r > t) & (r <= bnd)
        mats.append(m_up | m_lo)
    return jnp.concatenate(mats, axis=0).astype(BF16)


def _hgrn_masks():
    L = HG_L
    t = jnp.arange(L)[:, None]
    s = jnp.arange(L)[None, :]
    ms = [(t == s)]
    for l in range(int(math.log2(L))):
        half = 1 << l
        same = (t // (2 * half)) == (s // (2 * half))
        ms.append(same & ((t % (2 * half)) >= half) & ((s % (2 * half)) < half))
    return jnp.stack(ms).astype(F32)


def _hgrn_kernel(q_ref, f_ref, i_ref, g_ref, lb_ref, lvl_ref, msk_ref, nw_ref, y_ref, S_ref):
    L = HG_L
    nl = int(math.log2(L))

    @pl.when(pl.program_id(1) == 0)
    def _():
        S_ref[...] = jnp.zeros_like(S_ref)

    lg = lb_ref[...]
    mx = jnp.max(lg, axis=0, keepdims=True)
    ex = jnp.exp(lg - mx)
    pr = ex / jnp.sum(ex, axis=0, keepdims=True)
    lb = (pr[0:1] + pr[1:2]) - pr[0:1]

    q = _silu(q_ref[...])
    fp = f_ref[...]
    iv = i_ref[...]
    a1 = jnp.log(lb)
    a2 = jnp.log1p(-lb) + _log_sigmoid(fp)
    log_f = jnp.maximum(a1, a2) + jnp.log1p(jnp.exp(-jnp.abs(a1 - a2)))
    k = (1.0 - lb) * _sigmoid(-fp)

    E = _dot_sel(lvl_ref[...], log_f)
    bcum = E[0:L]
    b_last = bcum[L - 1:L]
    qb = q * jnp.exp(bcum)
    kd = k * jnp.exp(b_last - bcum)
    e_last = jnp.exp(b_last)
    qs = [q]
    ks = [k]
    for l in range(nl):
        A = jnp.exp(E[(l + 1) * L:(l + 2) * L])
        qs.append(q * A)
        ks.append(k * A)
    gg = g_ref[...]

    for h in range(HG_H):
        sl = slice(h * HG_HEAD, (h + 1) * HG_HEAD)
        attn = jnp.zeros((L, L), F32)
        for l in range(nl + 1):
            attn = attn + msk_ref[l] * _dot_nt(_bf(qs[l][:, sl]), _bf(ks[l][:, sl]))
        i_h = _bf(iv[:, sl])
        St = S_ref[h]
        o = _dot(_bf(attn), i_h) + _dot_nt(_bf(qb[:, sl]), _bf(St))
        S_ref[h] = St * e_last[:, sl] + _dot_tn(i_h, _bf(kd[:, sl]))
        ms = jnp.mean(o * o, axis=-1, keepdims=True)
        on = o * lax.rsqrt(ms + NORM_EPS) * nw_ref[:, sl]
        y_ref[:, sl] = _bf(on * _silu(gg[:, sl]))


def _hgrn(u, offs, B, T, prm):
    L = HG_L
    nc = T // L
    nl = int(math.log2(L))

    def col(off, w):
        return pl.BlockSpec((L, w), lambda b, c: (b * nc + c, off // w))

    in_specs = [col(offs["hg_q"], HG_W), col(offs["hg_f"], HG_W), col(offs["hg_i"], HG_W), col(offs["hg_g"], HG_W),
                pl.BlockSpec((2, HG_W), lambda b, c: (0, 0)),
                pl.BlockSpec(((nl + 1) * L, L), lambda b, c: (0, 0)),
                pl.BlockSpec((nl + 1, L, L), lambda b, c: (0, 0, 0)),
                pl.BlockSpec((1, HG_W), lambda b, c: (0, 0))]
    return pl.pallas_call(
        _hgrn_kernel,
        grid=(B, nc),
        in_specs=in_specs,
        out_specs=pl.BlockSpec((L, HG_W), lambda b, c: (b * nc + c, 0)),
        out_shape=jax.ShapeDtypeStruct((B * T, HG_W), BF16),
        scratch_shapes=[pltpu.VMEM((HG_H, HG_HEAD, HG_HEAD), F32)],
        compiler_params=_cparams(("parallel", "arbitrary")),
        name="hgrn2",
    )(u, u, u, u, *prm)


def _mamba_kernel(z_ref, x_ref, b_ref, c_ref, dtc_ref, dtr_ref,
                  cwx_ref, cbx_ref, cwb_ref, cbb_ref, cwc_ref, cbc_ref,
                  dbr_ref, dbc_ref, alr_ref, alc_ref, dsk_ref, nw_ref, exp_ref,
                  y_ref, S_ref, tx_ref, tb_ref, tc_ref):
    L = MB_L
    GW = MB_W // MB_G
    E = MB_H // MB_G

    @pl.when(pl.program_id(1) == 0)
    def _():
        S_ref[...] = jnp.zeros_like(S_ref)
        tx_ref[...] = jnp.zeros_like(tx_ref)
        tb_ref[...] = jnp.zeros_like(tb_ref)
        tc_ref[...] = jnp.zeros_like(tc_ref)

    def conv(ref, tail_ref, w_ref, bias_ref):
        x = ref[...]
        tail = tail_ref[...]
        acc = x * w_ref[CONV_K - 1:CONV_K] + bias_ref[...]
        for j in range(1, CONV_K):
            acc = acc + _shift_rows(tail, x, j) * w_ref[CONV_K - 1 - j:CONV_K - j]
        tail_ref[...] = x[L - SUBLANES:]
        return _silu(acc)

    xs = conv(x_ref, tx_ref, cwx_ref, cbx_ref)
    Bm = conv(b_ref, tb_ref, cwb_ref, cbb_ref)
    Cm = conv(c_ref, tc_ref, cwc_ref, cbc_ref)

    dt_c = _softplus(dtc_ref[...] + dbr_ref[...])
    dt_r = _softplus(dtr_ref[0] + dbc_ref[...])
    adt_c = dt_c * (-jnp.exp(alr_ref[...]))
    adt_r = dt_r * (-jnp.exp(alc_ref[...]))
    ri = _iota((L, L), 0)
    ci = _iota((L, L), 1)
    causal = ri >= ci
    tril = jnp.where(causal, 1.0, 0.0).astype(BF16)
    triu = jnp.where(ri <= ci, 1.0, 0.0).astype(BF16)
    ac_c = _dot_sel(tril, adt_c)
    ac_r = _dot_sel_r(adt_r, triu)
    ex = exp_ref[...]
    dt_full = _dot_sel_r(dt_c, ex)
    ac_full = _dot_sel_r(ac_c, ex)
    X = xs * dt_full
    a_last = ac_full[L - 1:L]
    dec_out = jnp.exp(ac_full)
    Xd = X * jnp.exp(a_last - ac_full)
    e_last = jnp.exp(a_last)
    neg = jnp.float32(-jnp.inf)
    lane_g = _idiv(_iota((1, GW), 1), MB_HEAD)
    zz = z_ref[...]

    for g in range(MB_G):
        gs = slice(g * GW, (g + 1) * GW)
        Bg = _bf(Bm[:, g * MB_N:(g + 1) * MB_N])
        Cg = _bf(Cm[:, g * MB_N:(g + 1) * MB_N])
        CB = _dot_nt(Cg, Bg)
        Xg = _bf(X[:, gs])
        Sg = S_ref[g]
        y = _dot(Cg, _bf(Sg)) * dec_out[:, gs]
        for e in range(E):
            h = g * E + e
            Lm = jnp.exp(jnp.where(causal, ac_c[:, h:h + 1] - ac_r[h:h + 1, :], neg))
            yd = _dot(_bf(CB * Lm), Xg)
            y = y + jnp.where(lane_g == e, yd, 0.0)
        S_ref[g] = Sg * e_last[:, gs] + _dot_tn(Bg, _bf(Xd[:, gs]))
        y = y + xs[:, gs] * dsk_ref[:, gs]
        y = y * _silu(zz[:, gs])
        ms = jnp.mean(y * y, axis=-1, keepdims=True)
        y_ref[:, gs] = _bf(y * lax.rsqrt(ms + NORM_EPS) * nw_ref[:, gs])


def _mamba(u, dt_row, offs, B, T, prm):
    L = MB_L
    nc = T // L
    GN = MB_G * MB_N

    def col(off, w):
        return pl.BlockSpec((L, w), lambda b, c: (b * nc + c, off // w))

    def par(r, w):
        return pl.BlockSpec((r, w), lambda b, c: (0, 0))

    in_specs = [col(offs["mb_z"], MB_W), col(offs["mb_x"], MB_W), col(offs["mb_b"], GN), col(offs["mb_c"], GN),
                col(offs["mb_dt"], LANES),
                pl.BlockSpec((1, MB_H, L), lambda b, c: (b, 0, c)),
                par(CONV_K, MB_W), par(1, MB_W), par(CONV_K, GN), par(1, GN), par(CONV_K, GN), par(1, GN),
                par(1, LANES), par(MB_H, 1), par(1, LANES), par(MB_H, 1), par(1, MB_W), par(1, MB_W),
                par(LANES, MB_W)]
    return pl.pallas_call(
        _mamba_kernel,
        grid=(B, nc),
        in_specs=in_specs,
        out_specs=pl.BlockSpec((L, MB_W), lambda b, c: (b * nc + c, 0)),
        out_shape=jax.ShapeDtypeStruct((B * T, MB_W), BF16),
        scratch_shapes=[pltpu.VMEM((MB_G, MB_N, MB_W // MB_G), F32),
                        pltpu.VMEM((SUBLANES, MB_W), F32),
                        pltpu.VMEM((SUBLANES, GN), F32),
                        pltpu.VMEM((SUBLANES, GN), F32)],
        compiler_params=_cparams(("parallel", "arbitrary")),
        name="mamba2",
    )(u, u, u, u, u, dt_row, *prm)


def _layout(segs, n_total):
    offs, cur = {}, 0
    for name, _, _, pw in segs:
        offs[name] = cur
        cur += pw
    assert cur <= n_total

    def pack(w, dtype=BF16):
        cols = []
        for _, s, wd, pw in segs:
            cols.append(w[:, s:s + wd])
            if pw > wd:
                cols.append(jnp.zeros((w.shape[0], pw - wd), w.dtype))
        if n_total > cur:
            cols.append(jnp.zeros((w.shape[0], n_total - cur), w.dtype))
        return jnp.concatenate(cols, axis=1).astype(dtype)

    return offs, pack


_EV_SEGS = [("rw_r", 0, RW_W, RW_W), ("rw_k", RW_W, RW_W, RW_W), ("rw_v", 2 * RW_W, RW_W, RW_W),
            ("ml_qk", RW_IN, 2 * ML_QK, 2 * ML_QK), ("ml_v", RW_IN + 2 * ML_QK, ML_W, ML_W),
            ("ml_o", RW_IN + 2 * ML_QK + ML_W, ML_W, ML_W),
            ("rw_dg", 3 * RW_W + RW_DECAY_LORA + RW_ICLR_LORA, RW_GATE_LORA, RW_GATE_LORA),
            ("rw_dw", 3 * RW_W, RW_DECAY_LORA, LANES),
            ("rw_da", 3 * RW_W + RW_DECAY_LORA, RW_ICLR_LORA, LANES),
            ("ml_if", RW_IN + 2 * ML_QK + 2 * ML_W, 2 * ML_H, LANES)]
_EV_N = 7168
_OD_SEGS = [("hg_q", 0, HG_W, HG_W), ("hg_f", HG_W, HG_W, HG_W), ("hg_i", 2 * HG_W, HG_W, HG_W),
            ("hg_g", 3 * HG_W, HG_W, HG_W),
            ("mb_z", HG_IN, MB_W, MB_W), ("mb_x", HG_IN + MB_W, MB_W, MB_W),
            ("mb_b", HG_IN + 2 * MB_W, MB_G * MB_N, MB_G * MB_N),
            ("mb_c", HG_IN + 2 * MB_W + MB_G * MB_N, MB_G * MB_N, MB_G * MB_N),
            ("mb_dt", HG_IN + MB_W + MB_CONV_W, MB_H, LANES)]
_OD_N = 7680


def _row(v, width=None):
    v = v.reshape(1, -1).astype(F32)
    if width is not None and v.shape[1] < width:
        v = jnp.pad(v, ((0, 0), (0, width - v.shape[1])))
    return v


def _pad_rows(w, rows):
    return jnp.pad(w, ((0, rows - w.shape[0]), (0, 0)))


def _even_layer(x, B, T, p):
    offs, pack = _layout(_EV_SEGS, _EV_N)
    u = _norm_matmul(x, _row(p["norm1"]), pack(p["w_in"]))
    mu = p["rw_mu"]
    o_dw, o_da, o_dg = 3 * RW_W, 3 * RW_W + RW_DECAY_LORA, 3 * RW_W + RW_DECAY_LORA + RW_ICLR_LORA
    rw_prm = [_row(mu[0:RW_W]), _row(mu[RW_W:2 * RW_W]), _row(mu[2 * RW_W:3 * RW_W]),
              _row(mu[o_dw:o_da], LANES), _row(mu[o_da:o_dg], LANES), _row(mu[o_dg:]),
              _row(p["rw_w0"]), _bf(_pad_rows(p["rw_w2"], LANES)),
              _row(p["rw_a0"]), _bf(_pad_rows(p["rw_a2"], LANES)), _bf(p["rw_g2"]),
              _row(p["rw_k_k"]), _row(p["rw_k_a"]), _row(p["rw_r_k"]), _row(p["rw_ln_w"]), _row(p["rw_ln_b"])]
    y_a = _rwkv(u, offs, B, T, rw_prm)
    o_if = offs["ml_if"]
    g_row = jnp.swapaxes(u[:, o_if:o_if + 2 * ML_H].reshape(B, T, 2 * ML_H), 1, 2)
    gb = jnp.concatenate([p["ml_i_b"], p["ml_f_b"]]).astype(F32)
    ml_prm = [p["ml_conv_w"].astype(F32), _row(p["ml_conv_b"]), _row(gb, LANES), gb.reshape(-1, 1),
              _row(p["ml_norm_w"])]
    y_b = _mlstm(u, g_row, offs, B, T, ml_prm)
    y = jnp.concatenate([y_a, y_b], axis=1)
    x = _matmul_res(y, _bf(p["w_out"]), x)
    return _ffn(x, _row(p["norm2"]), _bf(p["ffn_w_gate"]), _bf(p["ffn_w_up"]), _bf(p["ffn_w_down"]))


def _odd_layer(x, B, T, p, final_w):
    offs, pack = _layout(_OD_SEGS, _OD_N)
    u = _norm_matmul(x, _row(p["norm1"]), pack(p["w_in"]))
    hg_prm = [p["hg_lb_logits"].astype(F32), _hgrn_level_matrices(), _hgrn_masks(), _row(p["hg_norm_w"])]
    y_c = _hgrn(u, offs, B, T, hg_prm)
    o_dt = offs["mb_dt"]
    dt_row = jnp.swapaxes(u[:, o_dt:o_dt + MB_H].reshape(B, T, MB_H), 1, 2)
    cw, cb = p["mb_conv_w"].astype(F32), p["mb_conv_b"].astype(F32)
    GN = MB_G * MB_N
    expand = (jnp.arange(LANES)[:, None] == (jnp.arange(MB_W)[None, :] // MB_HEAD)).astype(BF16)
    mb_prm = [cw[:, :MB_W], _row(cb[:MB_W]), cw[:, MB_W:MB_W + GN], _row(cb[MB_W:MB_W + GN]),
              cw[:, MB_W + GN:], _row(cb[MB_W + GN:]),
              _row(p["mb_dt_bias"], LANES), p["mb_dt_bias"].astype(F32).reshape(-1, 1),
              _row(p["mb_A_log"], LANES), p["mb_A_log"].astype(F32).reshape(-1, 1),
              _row(jnp.repeat(p["mb_D"], MB_HEAD)), _row(p["mb_norm_w"]), expand]
    y_d = _mamba(u, dt_row, offs, B, T, mb_prm)
    y = jnp.concatenate([y_c, y_d], axis=1)
    x = _matmul_res(y, _bf(p["w_out"]), x)
    wr = jnp.pad(p["moe_router"].astype(F32), ((0, 0), (0, LANES - N_EXPERTS)))
    gates, h = _router(x, _row(p["norm2"]), wr)
    return _moe(x, h, gates, _bf(p["moe_w_gate"]), _bf(p["moe_w_up"]), _bf(p["moe_w_down"]), _row(final_w))


def kernel(x, final_norm_w, hg_lb_logits, ev_norm1_w, ev_w_in, ev_w_out, rw_mu, rw_w0, rw_w2, rw_a0, rw_a2, rw_g2, rw_k_k, rw_k_a, rw_r_k, rw_ln_w, rw_ln_b, ml_conv_w, ml_conv_b, ml_i_b, ml_f_b, ml_norm_w, ev_norm2_w, ffn_w_gate, ffn_w_up, ffn_w_down, od_norm1_w, od_w_in, od_w_out, hg_norm_w, mb_conv_w, mb_conv_b, mb_dt_bias, mb_A_log, mb_D, mb_norm_w, od_norm2_w, moe_router, moe_w_gate, moe_w_up, moe_w_down):
    B, T, D = x.shape
    xf = x.reshape(B * T, D)
    ev = dict(norm1=ev_norm1_w[0], w_in=ev_w_in[0], w_out=ev_w_out[0], rw_mu=rw_mu[0], rw_w0=rw_w0[0],
              rw_w2=rw_w2[0], rw_a0=rw_a0[0], rw_a2=rw_a2[0], rw_g2=rw_g2[0], rw_k_k=rw_k_k[0],
              rw_k_a=rw_k_a[0], rw_r_k=rw_r_k[0], rw_ln_w=rw_ln_w[0], rw_ln_b=rw_ln_b[0],
              ml_conv_w=ml_conv_w[0], ml_conv_b=ml_conv_b[0], ml_i_b=ml_i_b[0], ml_f_b=ml_f_b[0],
              ml_norm_w=ml_norm_w[0], norm2=ev_norm2_w[0], ffn_w_gate=ffn_w_gate[0], ffn_w_up=ffn_w_up[0],
              ffn_w_down=ffn_w_down[0])
    od = dict(norm1=od_norm1_w[0], w_in=od_w_in[0], w_out=od_w_out[0], hg_lb_logits=hg_lb_logits,
              hg_norm_w=hg_norm_w[0], mb_conv_w=mb_conv_w[0], mb_conv_b=mb_conv_b[0], mb_dt_bias=mb_dt_bias[0],
              mb_A_log=mb_A_log[0], mb_D=mb_D[0], mb_norm_w=mb_norm_w[0], norm2=od_norm2_w[0],
              moe_router=moe_router[0], moe_w_gate=moe_w_gate[0], moe_w_up=moe_w_up[0],
              moe_w_down=moe_w_down[0])
    xf = _even_layer(xf, B, T, ev)
    xf = _odd_layer(xf, B, T, od, final_norm_w)
    return xf.reshape(B, T, D)
```

```python
import functools
import math

import jax
import jax.numpy as jnp
from jax import lax
from jax.experimental import pallas as pl
from jax.experimental.pallas import tpu as pltpu

F32 = jnp.float32
BF16 = jnp.bfloat16

D_MODEL = 2048
NORM_EPS = 1e-6
RW_HEAD = 64
RW_W = 1024
RW_H = RW_W // RW_HEAD
RW_DECAY_LORA = 96
RW_ICLR_LORA = 96
RW_GATE_LORA = 256
RW_IN = 3 * RW_W + RW_DECAY_LORA + RW_ICLR_LORA + RW_GATE_LORA
RW_LN_EPS = 64e-5
RW_DECAY_SCALE = math.exp(-0.5)
ML_W = 1024
ML_H = 4
ML_DV = ML_W // ML_H
ML_DK = ML_DV // 2
ML_QK = ML_H * ML_DK
ML_IN = 2 * ML_QK + 2 * ML_W + 2 * ML_H
HG_W = 1024
HG_HEAD = 128
HG_H = HG_W // HG_HEAD
HG_IN = 4 * HG_W
MB_W = 1024
MB_HEAD = 64
MB_H = MB_W // MB_HEAD
MB_G = 4
MB_N = 128
MB_CONV_W = MB_W + 2 * MB_G * MB_N
N_EXPERTS = 8

LANES = 128
SUBLANES = 8
VMEM_LIMIT = 56 * 1024 * 1024

TM_PROJ = 1024
TN_PROJ = 512
TM_FFN = 512
TF_FFN = 512
TM_MOE = 256
TF_MOE = 1408
TN_MOE = 1024
RW_L = 64
RW_BLK = 16
ML_L = 128
HG_L = 64
MB_L = 128
CONV_K = 4


def _cparams(sem):
    return pltpu.CompilerParams(dimension_semantics=sem, vmem_limit_bytes=VMEM_LIMIT)


def _dot(a, b):
    return jnp.dot(a, b, preferred_element_type=F32)


def _dot_nt(a, b):
    return lax.dot_general(a, b, (((1,), (1,)), ((), ())), preferred_element_type=F32)


def _dot_tn(a, b):
    return lax.dot_general(a, b, (((0,), (0,)), ((), ())), preferred_element_type=F32)


def _bf(x):
    return x.astype(BF16)


def _split2(x):
    h = x.astype(BF16)
    l = (x - h.astype(F32)).astype(BF16)
    return h, l


def _split3(x):
    h = x.astype(BF16)
    r = x - h.astype(F32)
    m = r.astype(BF16)
    l = (r - m.astype(F32)).astype(BF16)
    return h, m, l


def _dot_sel(sel_bf16, x):
    h, m, l = _split3(x)
    return _dot(sel_bf16, h) + _dot(sel_bf16, m) + _dot(sel_bf16, l)


def _dot_sel_r(x, sel_bf16):
    h, m, l = _split3(x)
    return _dot(h, sel_bf16) + _dot(m, sel_bf16) + _dot(l, sel_bf16)


def _dot3(a, b):
    ah, al = _split2(a)
    bh, bl = _split2(b)
    return _dot(ah, bh) + _dot(ah, bl) + _dot(al, bh)


def _sigmoid(x):
    return 1.0 / (1.0 + jnp.exp(-x))


def _silu(x):
    return x * _sigmoid(x)


def _log_sigmoid(x):
    return -(jnp.maximum(-x, 0.0) + jnp.log1p(jnp.exp(-jnp.abs(x))))


def _softplus(x):
    return jnp.maximum(x, 0.0) + jnp.log1p(jnp.exp(-jnp.abs(x)))


def _iota(shape, dim):
    return lax.broadcasted_iota(jnp.int32, shape, dim)


def _idiv(x, d):
    sh = d.bit_length() - 1
    assert d == 1 << sh
    return lax.shift_right_logical(x, jnp.int32(sh))


def _shift_rows(tail, x, j):
    xc = jnp.concatenate([tail, x], axis=0)
    return pltpu.roll(xc, j, 0)[SUBLANES:]


def _norm_matmul_kernel(x_ref, nw_ref, w_ref, o_ref, h_ref):
    @pl.when(pl.program_id(1) == 0)
    def _():
        x = x_ref[...]
        ms = jnp.mean(x * x, axis=-1, keepdims=True)
        h_ref[...] = _bf(x * lax.rsqrt(ms + NORM_EPS) * nw_ref[...])

    o_ref[...] = _dot(h_ref[...], w_ref[...])


def _norm_matmul(x, nw, w):
    m, d = x.shape
    n = w.shape[1]
    return pl.pallas_call(
        _norm_matmul_kernel,
        grid=(m // TM_PROJ, n // TN_PROJ),
        in_specs=[pl.BlockSpec((TM_PROJ, d), lambda i, j: (i, 0)),
                  pl.BlockSpec((1, d), lambda i, j: (0, 0)),
                  pl.BlockSpec((d, TN_PROJ), lambda i, j: (0, j))],
        out_specs=pl.BlockSpec((TM_PROJ, TN_PROJ), lambda i, j: (i, j)),
        out_shape=jax.ShapeDtypeStruct((m, n), F32),
        scratch_shapes=[pltpu.VMEM((TM_PROJ, d), BF16)],
        compiler_params=_cparams(("parallel", "arbitrary")),
        name="norm_matmul",
    )(x, nw, w)


def _matmul_res_kernel(y_ref, w_ref, r_ref, o_ref):
    o_ref[...] = r_ref[...] + _dot(y_ref[...], w_ref[...])


def _matmul_res(y, w, res):
    m, k = y.shape
    n = w.shape[1]
    return pl.pallas_call(
        _matmul_res_kernel,
        grid=(m // TM_PROJ, n // TN_PROJ),
        in_specs=[pl.BlockSpec((TM_PROJ, k), lambda i, j: (i, 0)),
                  pl.BlockSpec((k, TN_PROJ), lambda i, j: (0, j)),
                  pl.BlockSpec((TM_PROJ, TN_PROJ), lambda i, j: (i, j))],
        out_specs=pl.BlockSpec((TM_PROJ, TN_PROJ), lambda i, j: (i, j)),
        out_shape=jax.ShapeDtypeStruct((m, n), F32),
        compiler_params=_cparams(("parallel", "arbitrary")),
        name="matmul_res",
    )(y, w, res)


def _ffn_kernel(x_ref, nw_ref, wg_ref, wu_ref, wd_ref, o_ref, h_ref):
    f = pl.program_id(1)

    @pl.when(f == 0)
    def _():
        x = x_ref[...]
        ms = jnp.mean(x * x, axis=-1, keepdims=True)
        h_ref[...] = _bf(x * lax.rsqrt(ms + NORM_EPS) * nw_ref[...])
        o_ref[...] = x

    h = h_ref[...]
    act = _silu(_dot(h, wg_ref[...])) * _dot(h, wu_ref[...])
    o_ref[...] += _dot(_bf(act), wd_ref[...])


def _ffn(x, nw, wg, wu, wd):
    m, d = x.shape
    f = wg.shape[1]
    return pl.pallas_call(
        _ffn_kernel,
        grid=(m // TM_FFN, f // TF_FFN),
        in_specs=[pl.BlockSpec((TM_FFN, d), lambda i, j: (i, 0)),
                  pl.BlockSpec((1, d), lambda i, j: (0, 0)),
                  pl.BlockSpec((d, TF_FFN), lambda i, j: (0, j)),
                  pl.BlockSpec((d, TF_FFN), lambda i, j: (0, j)),
                  pl.BlockSpec((TF_FFN, d), lambda i, j: (j, 0))],
        out_specs=pl.BlockSpec((TM_FFN, d), lambda i, j: (i, 0)),
        out_shape=jax.ShapeDtypeStruct((m, d), F32),
        scratch_shapes=[pltpu.VMEM((TM_FFN, d), BF16)],
        compiler_params=_cparams(("parallel", "arbitrary")),
        name="ffn_swiglu",
    )(x, nw, wg, wu, wd)


def _router_kernel(x_ref, nw_ref, wr_ref, i_ref, p_ref, h_ref):
    x = x_ref[...]
    ms = jnp.mean(x * x, axis=-1, keepdims=True)
    h = x * lax.rsqrt(ms + NORM_EPS) * nw_ref[...]
    h_ref[...] = _bf(h)
    wr = wr_ref[...]
    hh, hl = _split2(h)
    wh, wl = _split2(wr)
    logits = _dot(hh, wh) + _dot(hh, wl) + _dot(hl, wh)
    lane = _iota(logits.shape, 1)
    neg = jnp.float32(-jnp.inf)
    logits = jnp.where(lane < N_EXPERTS, logits, neg)
    v1 = jnp.max(logits, axis=-1, keepdims=True)
    i1 = jnp.min(jnp.where(logits == v1, lane, LANES), axis=-1, keepdims=True)
    rest = jnp.where(lane == i1, neg, logits)
    v2 = jnp.max(rest, axis=-1, keepdims=True)
    i2 = jnp.min(jnp.where(rest == v2, lane, LANES), axis=-1, keepdims=True)
    e2 = jnp.exp(v2 - v1)
    p1 = 1.0 / (1.0 + e2)
    p2 = e2 / (1.0 + e2)
    i_ref[...] = jnp.where(lane == 0, i1, jnp.where(lane == 1, i2, 0))
    p_ref[...] = jnp.where(lane == 0, p1, jnp.where(lane == 1, p2, 0.0))


def _router(x, nw, wr):
    m, d = x.shape
    return pl.pallas_call(
        _router_kernel,
        grid=(m // TM_FFN,),
        in_specs=[pl.BlockSpec((TM_FFN, d), lambda i: (i, 0)),
                  pl.BlockSpec((1, d), lambda i: (0, 0)),
                  pl.BlockSpec((d, LANES), lambda i: (0, 0))],
        out_specs=[pl.BlockSpec((TM_FFN, LANES), lambda i: (i, 0)),
                   pl.BlockSpec((TM_FFN, LANES), lambda i: (i, 0)),
                   pl.BlockSpec((TM_FFN, d), lambda i: (i, 0))],
        out_shape=[jax.ShapeDtypeStruct((m, LANES), jnp.int32),
                   jax.ShapeDtypeStruct((m, LANES), F32),
                   jax.ShapeDtypeStruct((m, d), BF16)],
        compiler_params=_cparams(("parallel",)),
        name="moe_router",
    )(x, nw, wr)


def _route_plan(top_idx, top_p, tm):
    m = top_idx.shape[0]
    n_rows = 2 * m + N_EXPERTS * tm
    e_flat = top_idx.reshape(-1)
    onehot = (e_flat[:, None] == jnp.arange(N_EXPERTS, dtype=jnp.int32)[None, :]).astype(jnp.int32)
    rank = jnp.cumsum(onehot, axis=0) - onehot
    counts = jnp.sum(onehot, axis=0)
    padded = ((counts + tm - 1) // tm) * tm
    ends = jnp.cumsum(padded)
    off = ends - padded
    pos = jnp.sum(onehot * (off[None, :] + rank), axis=1)
    src_tok = jnp.zeros((n_rows,), jnp.int32).at[pos].set(jnp.arange(2 * m, dtype=jnp.int32) // 2)
    row_gate = jnp.zeros((n_rows,), F32).at[pos].set(top_p.reshape(-1))
    tile_start = jnp.arange(n_rows // tm, dtype=jnp.int32) * tm
    tile_expert = jnp.minimum(jnp.sum((tile_start[:, None] >= ends[None, :]).astype(jnp.int32), axis=1),
                              N_EXPERTS - 1)
    tile_active = (tile_start < ends[-1]).astype(jnp.int32)
    return pos.reshape(m, 2), src_tok, row_gate.reshape(n_rows, 1), tile_expert, tile_active


def _moe_up_kernel(te_ref, ta_ref, x_ref, g_ref, wg_ref, wu_ref, a_ref):
    i = pl.program_id(1)

    @pl.when(ta_ref[i] == 1)
    def _():
        x = x_ref[...]
        act = _silu(_dot(x, wg_ref[0])) * _dot(x, wu_ref[0]) * g_ref[...]
        a_ref[...] = _bf(act)

    @pl.when(ta_ref[i] == 0)
    def _():
        a_ref[...] = jnp.zeros_like(a_ref)


def _moe_up(xg, row_gate, tile_expert, tile_active, wg, wu):
    n_rows, d = xg.shape
    fe = wg.shape[2]
    grid_spec = pltpu.PrefetchScalarGridSpec(
        num_scalar_prefetch=2,
        grid=(fe // TF_MOE, n_rows // TM_MOE),
        in_specs=[pl.BlockSpec((TM_MOE, d), lambda f, i, te, ta: (i, 0)),
                  pl.BlockSpec((TM_MOE, 1), lambda f, i, te, ta: (i, 0)),
                  pl.BlockSpec((1, d, TF_MOE), lambda f, i, te, ta: (te[i], 0, f)),
                  pl.BlockSpec((1, d, TF_MOE), lambda f, i, te, ta: (te[i], 0, f))],
        out_specs=pl.BlockSpec((TM_MOE, TF_MOE), lambda f, i, te, ta: (i, f)),
    )
    return pl.pallas_call(
        _moe_up_kernel,
        grid_spec=grid_spec,
        out_shape=jax.ShapeDtypeStruct((n_rows, fe), BF16),
        compiler_params=_cparams(("arbitrary", "arbitrary")),
        name="moe_up",
    )(tile_expert, tile_active, xg, row_gate, wg, wu)


def _moe_down_kernel(te_ref, ta_ref, a_ref, wd_ref, y_ref):
    i = pl.program_id(1)

    @pl.when(ta_ref[i] == 1)
    def _():
        y_ref[...] = _bf(_dot(a_ref[...], wd_ref[0]))

    @pl.when(ta_ref[i] == 0)
    def _():
        y_ref[...] = jnp.zeros_like(y_ref)


def _moe_down(act, tile_expert, tile_active, wd):
    n_rows, fe = act.shape
    d = wd.shape[2]
    grid_spec = pltpu.PrefetchScalarGridSpec(
        num_scalar_prefetch=2,
        grid=(d // TN_MOE, n_rows // TM_MOE),
        in_specs=[pl.BlockSpec((TM_MOE, fe), lambda n, i, te, ta: (i, 0)),
                  pl.BlockSpec((1, fe, TN_MOE), lambda n, i, te, ta: (te[i], 0, n))],
        out_specs=pl.BlockSpec((TM_MOE, TN_MOE), lambda n, i, te, ta: (i, n)),
    )
    return pl.pallas_call(
        _moe_down_kernel,
        grid_spec=grid_spec,
        out_shape=jax.ShapeDtypeStruct((n_rows, d), BF16),
        compiler_params=_cparams(("arbitrary", "arbitrary")),
        name="moe_down",
    )(tile_expert, tile_active, act, wd)


def _combine_kernel(x_ref, y0_ref, y1_ref, fw_ref, o_ref):
    y = x_ref[...] + y0_ref[...].astype(F32) + y1_ref[...].astype(F32)
    ms = jnp.mean(y * y, axis=-1, keepdims=True)
    o_ref[...] = y * lax.rsqrt(ms + NORM_EPS) * fw_ref[...]


def _combine(x, y0, y1, final_w):
    m, d = x.shape
    row = pl.BlockSpec((TM_FFN, d), lambda i: (i, 0))
    return pl.pallas_call(
        _combine_kernel,
        grid=(m // TM_FFN,),
        in_specs=[row, row, row, pl.BlockSpec((1, d), lambda i: (0, 0))],
        out_specs=row,
        out_shape=jax.ShapeDtypeStruct((m, d), F32),
        compiler_params=_cparams(("parallel",)),
        name="moe_combine",
    )(x, y0, y1, final_w)


def _rwkv_kernel(r_ref, k_ref, v_ref, dw_ref, da_ref, dg_ref,
                 mur_ref, muk_ref, muv_ref, mudw_ref, muda_ref, mudg_ref,
                 w0_ref, w2_ref, a0_ref, a2_ref, g2_ref, kk_ref, ka_ref, rk_ref, lnw_ref, lnb_ref,
                 o_ref,
                 H_ref, tr_ref, tk_ref, tv_ref, tdw_ref, tda_ref, tdg_ref):
    L = RW_L
    L2 = 2 * L

    @pl.when(pl.program_id(1) == 0)
    def _():
        H_ref[...] = jnp.zeros_like(H_ref)
        for t in (tr_ref, tk_ref, tv_ref, tdw_ref, tda_ref, tdg_ref):
            t[...] = jnp.zeros_like(t)

    def shift_mix(ref, tail, mu):
        x = ref[...]
        prev = _shift_rows(tail[...], x, 1)
        tail[...] = x[L - SUBLANES:]
        return x + (prev - x) * mu[...]

    r = shift_mix(r_ref, tr_ref, mur_ref)
    k = shift_mix(k_ref, tk_ref, muk_ref)
    v = shift_mix(v_ref, tv_ref, muv_ref)
    dw = shift_mix(dw_ref, tdw_ref, mudw_ref)
    da = shift_mix(da_ref, tda_ref, muda_ref)
    dg = shift_mix(dg_ref, tdg_ref, mudg_ref)

    log_w = -RW_DECAY_SCALE * _sigmoid(w0_ref[...] + _dot(_bf(jnp.tanh(dw)), w2_ref[...]))
    a = _sigmoid(a0_ref[...] + _dot(_bf(da), a2_ref[...]))
    g = _dot(_bf(_sigmoid(dg)), g2_ref[...])

    ri = _iota((LANES, LANES), 0)
    ci = _iota((LANES, LANES), 1)
    hsum = jnp.where(_idiv(ri, RW_HEAD) == _idiv(ci, RW_HEAD), 1.0, 0.0).astype(BF16)
    same_head = _idiv(ri, L) == _idiv(ci, L)
    tril_s = jnp.where(same_head & (ri > ci), 1.0, 0.0)
    tril_i = jnp.where(same_head & (ri >= ci), 1.0, 0.0)
    blk_d = jnp.where(_idiv(ri, RW_BLK) == _idiv(ci, RW_BLK), 1.0, 0.0)
    eye = jnp.where(ri == ci, 1.0, 0.0)
    cum = jnp.where(_iota((L, L), 0) >= _iota((L, L), 1), 1.0, 0.0).astype(BF16)
    lane = _iota((1, LANES), 1)
    m0 = lane < RW_HEAD

    def head_sum(x):
        h, l = _split2(x)
        return _dot(h, hsum) + _dot(l, hsum)

    kk = k * kk_ref[...]
    k2 = k * (1.0 + (a - 1.0) * ka_ref[...])
    cw = _dot_sel(cum, log_w)
    e_prev = jnp.exp(cw - log_w)
    e_inv = jnp.exp(-cw)
    e_cw = jnp.exp(cw)
    wl = cw[L - 1:L]
    e_end = jnp.exp(wl - cw)
    e_wl = jnp.exp(wl)
    rk2 = r * k2 * rk_ref[...]

    def stack(x):
        x0 = jnp.where(m0, x, 0.0)
        return jnp.concatenate([x0, x - x0], axis=0)

    def mm3(xs, ys):
        return _dot(xs[0], ys[0]) + _dot(xs[0], ys[1]) + _dot(xs[1], ys[0])

    NP = RW_H // 2
    sls = [slice(p * LANES, (p + 1) * LANES) for p in range(NP)]

    def each(fn, *lists):
        return [fn(*args) for args in zip(*lists)]

    kk_l = [kk[:, sl] for sl in sls]
    kk_l = each(lambda x: x * lax.rsqrt(jnp.maximum(head_sum(x * x), 1e-12)), kk_l)
    kka_l = [x * a[:, sl] for x, sl in zip(kk_l, sls)]
    At = [_bf(stack(-x * e_prev[:, sl])) for x, sl in zip(kk_l, sls)]
    Bt = [_bf(stack(x * e_inv[:, sl])) for x, sl in zip(kka_l, sls)]
    Kt = [_bf(stack(k2[:, sl] * e_inv[:, sl])) for sl in sls]
    Rt_f = [stack(r[:, sl] * e_cw[:, sl]) for sl in sls]
    Rt = each(_bf, Rt_f)
    Bh = [_bf(stack(x * e_end[:, sl])) for x, sl in zip(kka_l, sls)]
    Kh = [_bf(stack(k2[:, sl] * e_end[:, sl])) for sl in sls]
    Vs = [_bf(stack(v[:, sl])) for sl in sls]

    Mab = each(lambda x, y: _dot_nt(x, y) * tril_s, At, Bt)
    Mak = each(lambda x, y: _bf(_dot_nt(x, y) * tril_s), At, Kt)
    Arb = each(lambda x, y: _bf(_dot_nt(x, y) * tril_i), Rt, Bt)
    Ark = each(lambda x, y: _bf(_dot_nt(x, y) * tril_i), Rt, Kt)

    MD = each(lambda m: m * blk_d, Mab)
    Nn = each(lambda m, d: _split2(m - d), Mab, MD)
    P = each(lambda d: eye + d, MD)
    MDs = each(_split2, MD)
    S = each(mm3, MDs, MDs)
    for it in range(3):
        Ss = each(_split2, S)
        P = each(lambda p_, s_: p_ + mm3(_split2(p_), s_), P, Ss)
        if it < 2:
            S = each(mm3, Ss, Ss)
    Ps = each(_split2, P)
    X = each(mm3, Ps, Nn)
    Xs = each(_split2, X)
    X2 = each(lambda x: _split2(mm3(x, x)), Xs)
    Y = each(lambda x: eye + x, X)
    Y = each(lambda y_, x2: y_ + mm3(_split2(y_), x2), Y, X2)
    T = each(lambda y_, p_: _split2(mm3(_split2(y_), p_)), Y, Ps)

    P1 = each(lambda t, x: _bf(_dot(t[0], x) + _dot(t[1], x)), T, At)
    MV = each(lambda m, x: _split2(_dot(m, x)), Mak, Vs)
    P2 = each(lambda t, x: _bf(mm3(t, x)), T, MV)
    G = [eye * e_wl[:, sl] + _dot_tn(b, p1) for sl, b, p1 in zip(sls, Bh, P1)]
    J = each(lambda b, p2, kh, vs: _dot_tn(b, p2) + _dot_tn(kh, vs), Bh, P2, Kh, Vs)
    Q = each(lambda rf, ar, p1: rf + _dot(ar, p1), Rt_f, Arb, P1)
    Z = each(lambda ar, p2, ak, vs: _dot(ar, p2) + _dot(ak, vs), Arb, P2, Ark, Vs)

    Hs = [_split2(H_ref[p]) for p in range(NP)]
    Yst = each(lambda q, h, z: mm3(_split2(q), h) + z, Q, Hs, Z)
    Hn = each(lambda g_, h, j: mm3(_split2(g_), h) + j, G, Hs, J)
    for p in range(NP):
        H_ref[p] = Hn[p]
    y_l = each(lambda y_: y_[:L] + y_[L:], Yst)

    mean = each(lambda y_: head_sum(y_) * (1.0 / RW_HEAD), y_l)
    yc = each(lambda y_, m: y_ - m, y_l, mean)
    var = each(lambda c: head_sum(c * c) * (1.0 / RW_HEAD), yc)
    yn = [c * lax.rsqrt(vr + RW_LN_EPS) * lnw_ref[:, sl] + lnb_ref[:, sl] for c, vr, sl in zip(yc, var, sls)]
    bonus = [head_sum(rk2[:, sl]) * v[:, sl] for sl in sls]
    for p in range(NP):
        o_ref[:, sls[p]] = _bf((yn[p] + bonus[p]) * g[:, sls[p]])


def _rwkv(u, offs, B, T, prm):
    L = RW_L
    nc = T // L

    def col(off, w):
        return pl.BlockSpec((L, w), lambda b, c: (b * nc + c, off // w))

    def par(w, rows=1):
        return pl.BlockSpec((rows, w), lambda b, c: (0, 0))

    in_specs = [col(offs["rw_r"], RW_W), col(offs["rw_k"], RW_W), col(offs["rw_v"], RW_W),
                col(offs["rw_dw"], LANES), col(offs["rw_da"], LANES), col(offs["rw_dg"], RW_GATE_LORA),
                par(RW_W), par(RW_W), par(RW_W), par(LANES), par(LANES), par(RW_GATE_LORA),
                par(RW_W), par(RW_W, LANES), par(RW_W), par(RW_W, LANES), par(RW_W, RW_GATE_LORA),
                par(RW_W), par(RW_W), par(RW_W), par(RW_W), par(RW_W)]
    return pl.pallas_call(
        _rwkv_kernel,
        grid=(B, nc),
        in_specs=in_specs,
        out_specs=pl.BlockSpec((L, RW_W), lambda b, c: (b * nc + c, 0)),
        out_shape=jax.ShapeDtypeStruct((B * T, RW_W), BF16),
        scratch_shapes=[pltpu.VMEM((RW_H // 2, LANES, LANES), F32),
                        pltpu.VMEM((SUBLANES, RW_W), F32), pltpu.VMEM((SUBLANES, RW_W), F32),
                        pltpu.VMEM((SUBLANES, RW_W), F32), pltpu.VMEM((SUBLANES, LANES), F32),
                        pltpu.VMEM((SUBLANES, LANES), F32), pltpu.VMEM((SUBLANES, RW_GATE_LORA), F32)],
        compiler_params=_cparams(("parallel", "arbitrary")),
        name="rwkv7",
    )(u, u, u, u, u, u, *prm)


def _mlstm_kernel(qk_ref, v_ref, o_ref, gc_ref, gr_ref, cw_ref, cb_ref, gbr_ref, gbc_ref, nw_ref,
                  y_ref, C_ref, m_ref, tail_ref):
    L = ML_L
    DK, DV = ML_DK, ML_DV

    @pl.when(pl.program_id(1) == 0)
    def _():
        C_ref[...] = jnp.zeros_like(C_ref)
        m_ref[...] = jnp.zeros_like(m_ref)
        tail_ref[...] = jnp.zeros_like(tail_ref)

    x = qk_ref[...]
    tail = tail_ref[...]
    acc = x * cw_ref[CONV_K - 1:CONV_K] + cb_ref[...]
    for j in range(1, CONV_K):
        acc = acc + _shift_rows(tail, x, j) * cw_ref[CONV_K - 1 - j:CONV_K - j]
    tail_ref[...] = x[L - SUBLANES:]
    qk = _silu(acc)

    gc = gc_ref[...] + gbr_ref[...]
    gr = gr_ref[0] + gbc_ref[...]
    fl_c = _log_sigmoid(gc)
    fl_r = _log_sigmoid(gr)
    ri = _iota((L, L), 0)
    ci = _iota((L, L), 1)
    causal = ri >= ci
    tril = jnp.where(causal, 1.0, 0.0).astype(BF16)
    triu = jnp.where(ri <= ci, 1.0, 0.0).astype(BF16)
    b_c = _dot_sel(tril, fl_c)
    b_r = _dot_sel_r(fl_r, triu)
    one_col = jnp.where(_iota((L, LANES), 1) == 0, 1.0, 0.0)
    neg = jnp.float32(-jnp.inf)
    vv = v_ref[...]
    oo = o_ref[...]

    for h in range(ML_H):
        q_h = _bf(qk[:, h * DK:(h + 1) * DK] * (DK ** -0.5))
        k_f = qk[:, ML_QK + h * DK:ML_QK + (h + 1) * DK]
        k_h = _bf(k_f)
        v_ext = _bf(jnp.concatenate([vv[:, h * DV:(h + 1) * DV], one_col], axis=1))
        bc = b_c[:, ML_H + h:ML_H + h + 1]
        br = b_r[ML_H + h:ML_H + h + 1, :]
        il_c = gc[:, h:h + 1]
        il_r = gr[h:h + 1, :]
        m_prev = m_ref[h:h + 1, 0:1]
        C_prev = C_ref[h]

        D = jnp.where(causal, bc - br + il_r, neg)
        m_t = jnp.maximum(bc + m_prev, jnp.max(D, axis=-1, keepdims=True))
        S = _dot_nt(q_h, k_h) * jnp.exp(D - m_t)
        inter = jnp.exp(bc + m_prev - m_t)
        num = inter * _dot(q_h, _bf(C_prev)) + _dot(_bf(S), v_ext)
        den = num[:, DV:DV + 1]
        hh = num[:, :DV] / jnp.maximum(jnp.abs(den), jnp.exp(-m_t))
        ms = jnp.mean(hh * hh, axis=-1, keepdims=True)
        hn = hh * lax.rsqrt(ms + NORM_EPS) * nw_ref[:, h * DV:(h + 1) * DV]
        y_ref[:, h * DV:(h + 1) * DV] = _bf(hn * _sigmoid(oo[:, h * DV:(h + 1) * DV]))

        g_end = bc[L - 1:L]
        wst = g_end - bc + il_c
        m_new = jnp.maximum(g_end + m_prev, jnp.max(wst, axis=0, keepdims=True))
        kt = _bf(k_f * jnp.exp(wst - m_new))
        C_ref[h] = jnp.exp(g_end + m_prev - m_new) * C_prev + _dot_tn(kt, v_ext)
        m_ref[h:h + 1, :] = jnp.broadcast_to(m_new, (1, LANES))


def _mlstm(u, g_row, offs, B, T, prm):
    L = ML_L
    nc = T // L

    def col(off, w):
        return pl.BlockSpec((L, w), lambda b, c: (b * nc + c, off // w))

    def par(r, w):
        return pl.BlockSpec((r, w), lambda b, c: (0, 0))

    in_specs = [col(offs["ml_qk"], 2 * ML_QK), col(offs["ml_v"], ML_W), col(offs["ml_o"], ML_W),
                col(offs["ml_if"], LANES),
                pl.BlockSpec((1, SUBLANES, L), lambda b, c: (b, 0, c)),
                par(CONV_K, 2 * ML_QK), par(1, 2 * ML_QK), par(1, LANES), par(SUBLANES, 1), par(1, ML_W)]
    return pl.pallas_call(
        _mlstm_kernel,
        grid=(B, nc),
        in_specs=in_specs,
        out_specs=pl.BlockSpec((L, ML_W), lambda b, c: (b * nc + c, 0)),
        out_shape=jax.ShapeDtypeStruct((B * T, ML_W), BF16),
        scratch_shapes=[pltpu.VMEM((ML_H, ML_DK, ML_DV + LANES), F32),
                        pltpu.VMEM((SUBLANES, LANES), F32),
                        pltpu.VMEM((SUBLANES, 2 * ML_QK), F32)],
        compiler_params=_cparams(("parallel", "arbitrary")),
        name="mlstm",
    )(u, u, u, u, g_row, *prm)


def _hgrn_level_matrices():
    L = HG_L
    t = jnp.arange(L)[:, None]
    r = jnp.arange(L)[None, :]
    mats = [(r <= t)]
    for l in range(int(math.log2(L))):
        half = 1 << l
        base = (t // (2 * half)) * (2 * half)
        bnd = base + half - 1
        upper = (t - base) >= half
        m_up = upper & (r > bnd) & (r <= t)
        m_lo = (~upper) & (r > t) & (r <= bnd)
        mats.append(m_up | m_lo)
    return jnp.concatenate(mats, axis=0).astype(BF16)


def _hgrn_masks():
    L = HG_L
    t = jnp.arange(L)[:, None]
    s = jnp.arange(L)[None, :]
    ms = [(t == s)]
    for l in range(int(math.log2(L))):
        half = 1 << l
        same = (t // (2 * half)) == (s // (2 * half))
        ms.append(same & ((t % (2 * half)) >= half) & ((s % (2 * half)) < half))
    return jnp.stack(ms).astype(F32)


def _hgrn_kernel(q_ref, f_ref, i_ref, g_ref, lb_ref, lvl_ref, msk_ref, nw_ref, y_ref, S_ref):
    L = HG_L
    nl = int(math.log2(L))

    @pl.when(pl.program_id(1) == 0)
    def _():
        S_ref[...] = jnp.zeros_like(S_ref)

    lg = lb_ref[...]
    mx = jnp.max(lg, axis=0, keepdims=True)
    ex = jnp.exp(lg - mx)
    pr = ex / jnp.sum(ex, axis=0, keepdims=True)
    lb = (pr[0:1] + pr[1:2]) - pr[0:1]

    q = _silu(q_ref[...])
    fp = f_ref[...]
    iv = i_ref[...]
    a1 = jnp.log(lb)
    a2 = jnp.log1p(-lb) + _log_sigmoid(fp)
    log_f = jnp.maximum(a1, a2) + jnp.log1p(jnp.exp(-jnp.abs(a1 - a2)))
    k = (1.0 - lb) * _sigmoid(-fp)

    E = _dot_sel(lvl_ref[...], log_f)
    bcum = E[0:L]
    b_last = bcum[L - 1:L]
    qb = q * jnp.exp(bcum)
    kd = k * jnp.exp(b_last - bcum)
    e_last = jnp.exp(b_last)
    qs = [q]
    ks = [k]
    for l in range(nl):
        A = jnp.exp(E[(l + 1) * L:(l + 2) * L])
        qs.append(q * A)
        ks.append(k * A)
    gg = g_ref[...]

    for h in range(HG_H):
        sl = slice(h * HG_HEAD, (h + 1) * HG_HEAD)
        attn = jnp.zeros((L, L), F32)
        for l in range(nl + 1):
            attn = attn + msk_ref[l] * _dot_nt(_bf(qs[l][:, sl]), _bf(ks[l][:, sl]))
        i_h = _bf(iv[:, sl])
        St = S_ref[h]
        o = _dot(_bf(attn), i_h) + _dot_nt(_bf(qb[:, sl]), _bf(St))
        S_ref[h] = St * e_last[:, sl] + _dot_tn(i_h, _bf(kd[:, sl]))
        ms = jnp.mean(o * o, axis=-1, keepdims=True)
        on = o * lax.rsqrt(ms + NORM_EPS) * nw_ref[:, sl]
        y_ref[:, sl] = _bf(on * _silu(gg[:, sl]))


def _hgrn(u, offs, B, T, prm):
    L = HG_L
    nc = T // L
    nl = int(math.log2(L))

    def col(off, w):
        return pl.BlockSpec((L, w), lambda b, c: (b * nc + c, off // w))

    in_specs = [col(offs["hg_q"], HG_W), col(offs["hg_f"], HG_W), col(offs["hg_i"], HG_W), col(offs["hg_g"], HG_W),
                pl.BlockSpec((2, HG_W), lambda b, c: (0, 0)),
                pl.BlockSpec(((nl + 1) * L, L), lambda b, c: (0, 0)),
                pl.BlockSpec((nl + 1, L, L), lambda b, c: (0, 0, 0)),
                pl.BlockSpec((1, HG_W), lambda b, c: (0, 0))]
    return pl.pallas_call(
        _hgrn_kernel,
        grid=(B, nc),
        in_specs=in_specs,
        out_specs=pl.BlockSpec((L, HG_W), lambda b, c: (b * nc + c, 0)),
        out_shape=jax.ShapeDtypeStruct((B * T, HG_W), BF16),
        scratch_shapes=[pltpu.VMEM((HG_H, HG_HEAD, HG_HEAD), F32)],
        compiler_params=_cparams(("parallel", "arbitrary")),
        name="hgrn2",
    )(u, u, u, u, *prm)


def _mamba_kernel(z_ref, x_ref, b_ref, c_ref, dtc_ref, dtr_ref,
                  cwx_ref, cbx_ref, cwb_ref, cbb_ref, cwc_ref, cbc_ref,
                  dbr_ref, dbc_ref, alr_ref, alc_ref, dsk_ref, nw_ref, exp_ref,
                  y_ref, S_ref, tx_ref, tb_ref, tc_ref):
    L = MB_L
    GW = MB_W // MB_G
    E = MB_H // MB_G

    @pl.when(pl.program_id(1) == 0)
    def _():
        S_ref[...] = jnp.zeros_like(S_ref)
        tx_ref[...] = jnp.zeros_like(tx_ref)
        tb_ref[...] = jnp.zeros_like(tb_ref)
        tc_ref[...] = jnp.zeros_like(tc_ref)

    def conv(ref, tail_ref, w_ref, bias_ref):
        x = ref[...]
        tail = tail_ref[...]
        acc = x * w_ref[CONV_K - 1:CONV_K] + bias_ref[...]
        for j in range(1, CONV_K):
            acc = acc + _shift_rows(tail, x, j) * w_ref[CONV_K - 1 - j:CONV_K - j]
        tail_ref[...] = x[L - SUBLANES:]
        return _silu(acc)

    xs = conv(x_ref, tx_ref, cwx_ref, cbx_ref)
    Bm = conv(b_ref, tb_ref, cwb_ref, cbb_ref)
    Cm = conv(c_ref, tc_ref, cwc_ref, cbc_ref)

    dt_c = _softplus(dtc_ref[...] + dbr_ref[...])
    dt_r = _softplus(dtr_ref[0] + dbc_ref[...])
    adt_c = dt_c * (-jnp.exp(alr_ref[...]))
    adt_r = dt_r * (-jnp.exp(alc_ref[...]))
    ri = _iota((L, L), 0)
    ci = _iota((L, L), 1)
    causal = ri >= ci
    tril = jnp.where(causal, 1.0, 0.0).astype(BF16)
    triu = jnp.where(ri <= ci, 1.0, 0.0).astype(BF16)
    ac_c = _dot_sel(tril, adt_c)
    ac_r = _dot_sel_r(adt_r, triu)
    ex = exp_ref[...]
    dt_full = _dot_sel_r(dt_c, ex)
    ac_full = _dot_sel_r(ac_c, ex)
    X = xs * dt_full
    a_last = ac_full[L - 1:L]
    dec_out = jnp.exp(ac_full)
    Xd = X * jnp.exp(a_last - ac_full)
    e_last = jnp.exp(a_last)
    neg = jnp.float32(-jnp.inf)
    lane_g = _idiv(_iota((1, GW), 1), MB_HEAD)
    zz = z_ref[...]

    for g in range(MB_G):
        gs = slice(g * GW, (g + 1) * GW)
        Bg = _bf(Bm[:, g * MB_N:(g + 1) * MB_N])
        Cg = _bf(Cm[:, g * MB_N:(g + 1) * MB_N])
        CB = _dot_nt(Cg, Bg)
        Xg = _bf(X[:, gs])
        Sg = S_ref[g]
        y = _dot(Cg, _bf(Sg)) * dec_out[:, gs]
        for e in range(E):
            h = g * E + e
            Lm = jnp.exp(jnp.where(causal, ac_c[:, h:h + 1] - ac_r[h:h + 1, :], neg))
            yd = _dot(_bf(CB * Lm), Xg)
            y = y + jnp.where(lane_g == e, yd, 0.0)
        S_ref[g] = Sg * e_last[:, gs] + _dot_tn(Bg, _bf(Xd[:, gs]))
        y = y + xs[:, gs] * dsk_ref[:, gs]
        y = y * _silu(zz[:, gs])
        ms = jnp.mean(y * y, axis=-1, keepdims=True)
        y_ref[:, gs] = _bf(y * lax.rsqrt(ms + NORM_EPS) * nw_ref[:, gs])


def _mamba(u, dt_row, offs, B, T, prm):
    L = MB_L
    nc = T // L
    GN = MB_G * MB_N

    def col(off, w):
        return pl.BlockSpec((L, w), lambda b, c: (b * nc + c, off // w))

    def par(r, w):
        return pl.BlockSpec((r, w), lambda b, c: (0, 0))

    in_specs = [col(offs["mb_z"], MB_W), col(offs["mb_x"], MB_W), col(offs["mb_b"], GN), col(offs["mb_c"], GN),
                col(offs["mb_dt"], LANES),
                pl.BlockSpec((1, MB_H, L), lambda b, c: (b, 0, c)),
                par(CONV_K, MB_W), par(1, MB_W), par(CONV_K, GN), par(1, GN), par(CONV_K, GN), par(1, GN),
                par(1, LANES), par(MB_H, 1), par(1, LANES), par(MB_H, 1), par(1, MB_W), par(1, MB_W),
                par(LANES, MB_W)]
    return pl.pallas_call(
        _mamba_kernel,
        grid=(B, nc),
        in_specs=in_specs,
        out_specs=pl.BlockSpec((L, MB_W), lambda b, c: (b * nc + c, 0)),
        out_shape=jax.ShapeDtypeStruct((B * T, MB_W), BF16),
        scratch_shapes=[pltpu.VMEM((MB_G, MB_N, MB_W // MB_G), F32),
                        pltpu.VMEM((SUBLANES, MB_W), F32),
                        pltpu.VMEM((SUBLANES, GN), F32),
                        pltpu.VMEM((SUBLANES, GN), F32)],
        compiler_params=_cparams(("parallel", "arbitrary")),
        name="mamba2",
    )(u, u, u, u, u, dt_row, *prm)


def _layout(segs, n_total):
    offs, cur = {}, 0
    for name, _, _, pw in segs:
        offs[name] = cur
        cur += pw
    assert cur <= n_total

    def pack(w, dtype=BF16):
        cols = []
        for _, s, wd, pw in segs:
            cols.append(w[:, s:s + wd])
            if pw > wd:
                cols.append(jnp.zeros((w.shape[0], pw - wd), w.dtype))
        if n_total > cur:
            cols.append(jnp.zeros((w.shape[0], n_total - cur), w.dtype))
        return jnp.concatenate(cols, axis=1).astype(dtype)

    return offs, pack


_EV_SEGS = [("rw_r", 0, RW_W, RW_W), ("rw_k", RW_W, RW_W, RW_W), ("rw_v", 2 * RW_W, RW_W, RW_W),
            ("ml_qk", RW_IN, 2 * ML_QK, 2 * ML_QK), ("ml_v", RW_IN + 2 * ML_QK, ML_W, ML_W),
            ("ml_o", RW_IN + 2 * ML_QK + ML_W, ML_W, ML_W),
            ("rw_dg", 3 * RW_W + RW_DECAY_LORA + RW_ICLR_LORA, RW_GATE_LORA, RW_GATE_LORA),
            ("rw_dw", 3 * RW_W, RW_DECAY_LORA, LANES),
            ("rw_da", 3 * RW_W + RW_DECAY_LORA, RW_ICLR_LORA, LANES),
            ("ml_if", RW_IN + 2 * ML_QK + 2 * ML_W, 2 * ML_H, LANES)]
_EV_N = 7168
_OD_SEGS = [("hg_q", 0, HG_W, HG_W), ("hg_f", HG_W, HG_W, HG_W), ("hg_i", 2 * HG_W, HG_W, HG_W),
            ("hg_g", 3 * HG_W, HG_W, HG_W),
            ("mb_z", HG_IN, MB_W, MB_W), ("mb_x", HG_IN + MB_W, MB_W, MB_W),
            ("mb_b", HG_IN + 2 * MB_W, MB_G * MB_N, MB_G * MB_N),
            ("mb_c", HG_IN + 2 * MB_W + MB_G * MB_N, MB_G * MB_N, MB_G * MB_N),
            ("mb_dt", HG_IN + MB_W + MB_CONV_W, MB_H, LANES)]
_OD_N = 7680


def _row(v, width=None):
    v = v.reshape(1, -1).astype(F32)
    if width is not None and v.shape[1] < width:
        v = jnp.pad(v, ((0, 0), (0, width - v.shape[1])))
    return v


def _pad_rows(w, rows):
    return jnp.pad(w, ((0, rows - w.shape[0]), (0, 0)))


def _even_layer(x, B, T, p):
    offs, pack = _layout(_EV_SEGS, _EV_N)
    u = _norm_matmul(x, _row(p["norm1"]), pack(p["w_in"]))
    mu = p["rw_mu"]
    o_dw, o_da, o_dg = 3 * RW_W, 3 * RW_W + RW_DECAY_LORA, 3 * RW_W + RW_DECAY_LORA + RW_ICLR_LORA
    rw_prm = [_row(mu[0:RW_W]), _row(mu[RW_W:2 * RW_W]), _row(mu[2 * RW_W:3 * RW_W]),
              _row(mu[o_dw:o_da], LANES), _row(mu[o_da:o_dg], LANES), _row(mu[o_dg:]),
              _row(p["rw_w0"]), _bf(_pad_rows(p["rw_w2"], LANES)),
              _row(p["rw_a0"]), _bf(_pad_rows(p["rw_a2"], LANES)), _bf(p["rw_g2"]),
              _row(p["rw_k_k"]), _row(p["rw_k_a"]), _row(p["rw_r_k"]), _row(p["rw_ln_w"]), _row(p["rw_ln_b"])]
    y_a = _rwkv(u, offs, B, T, rw_prm)
    o_if = offs["ml_if"]
    g_row = jnp.swapaxes(u[:, o_if:o_if + 2 * ML_H].reshape(B, T, 2 * ML_H), 1, 2)
    gb = jnp.concatenate([p["ml_i_b"], p["ml_f_b"]]).astype(F32)
    ml_prm = [p["ml_conv_w"].astype(F32), _row(p["ml_conv_b"]), _row(gb, LANES), gb.reshape(-1, 1),
              _row(p["ml_norm_w"])]
    y_b = _mlstm(u, g_row, offs, B, T, ml_prm)
    y = jnp.concatenate([y_a, y_b], axis=1)
    x = _matmul_res(y, _bf(p["w_out"]), x)
    return _ffn(x, _row(p["norm2"]), _bf(p["ffn_w_gate"]), _bf(p["ffn_w_up"]), _bf(p["ffn_w_down"]))


def _odd_layer(x, B, T, p, final_w):
    offs, pack = _layout(_OD_SEGS, _OD_N)
    u = _norm_matmul(x, _row(p["norm1"]), pack(p["w_in"]))
    hg_prm = [p["hg_lb_logits"].astype(F32), _hgrn_level_matrices(), _hgrn_masks(), _row(p["hg_norm_w"])]
    y_c = _hgrn(u, offs, B, T, hg_prm)
    o_dt = offs["mb_dt"]
    dt_row = jnp.swapaxes(u[:, o_dt:o_dt + MB_H].reshape(B, T, MB_H), 1, 2)
    cw, cb = p["mb_conv_w"].astype(F32), p["mb_conv_b"].astype(F32)
    GN = MB_G * MB_N
    expand = (jnp.arange(LANES)[:, None] == (jnp.arange(MB_W)[None, :] // MB_HEAD)).astype(BF16)
    mb_prm = [cw[:, :MB_W], _row(cb[:MB_W]), cw[:, MB_W:MB_W + GN], _row(cb[MB_W:MB_W + GN]),
              cw[:, MB_W + GN:], _row(cb[MB_W + GN:]),
              _row(p["mb_dt_bias"], LANES), p["mb_dt_bias"].astype(F32).reshape(-1, 1),
              _row(p["mb_A_log"], LANES), p["mb_A_log"].astype(F32).reshape(-1, 1),
              _row(jnp.repeat(p["mb_D"], MB_HEAD)), _row(p["mb_norm_w"]), expand]
    y_d = _mamba(u, dt_row, offs, B, T, mb_prm)
    y = jnp.concatenate([y_c, y_d], axis=1)
    x = _matmul_res(y, _bf(p["w_out"]), x)
    wr = jnp.pad(p["moe_router"].astype(F32), ((0, 0), (0, LANES - N_EXPERTS)))
    idx, prob, h = _router(x, _row(p["norm2"]), wr)
    pos, src_tok, row_gate, tile_expert, tile_active = _route_plan(idx[:, :2], prob[:, :2], TM_MOE)
    xg = jnp.take(h, src_tok, axis=0)
    act = _moe_up(xg, row_gate, tile_expert, tile_active, _bf(p["moe_w_gate"]), _bf(p["moe_w_up"]))
    yg = _moe_down(act, tile_expert, tile_active, _bf(p["moe_w_down"]))
    y0 = jnp.take(yg, pos[:, 0], axis=0)
    y1 = jnp.take(yg, pos[:, 1], axis=0)
    return _combine(x, y0, y1, _row(final_w))


def kernel(x, final_norm_w, hg_lb_logits, ev_norm1_w, ev_w_in, ev_w_out, rw_mu, rw_w0, rw_w2, rw_a0, rw_a2, rw_g2, rw_k_k, rw_k_a, rw_r_k, rw_ln_w, rw_ln_b, ml_conv_w, ml_conv_b, ml_i_b, ml_f_b, ml_norm_w, ev_norm2_w, ffn_w_gate, ffn_w_up, ffn_w_down, od_norm1_w, od_w_in, od_w_out, hg_norm_w, mb_conv_w, mb_conv_b, mb_dt_bias, mb_A_log, mb_D, mb_norm_w, od_norm2_w, moe_router, moe_w_gate, moe_w_up, moe_w_down):
    B, T, D = x.shape
    xf = x.reshape(B * T, D)
    ev = dict(norm1=ev_norm1_w[0], w_in=ev_w_in[0], w_out=ev_w_out[0], rw_mu=rw_mu[0], rw_w0=rw_w0[0],
              rw_w2=rw_w2[0], rw_a0=rw_a0[0], rw_a2=rw_a2[0], rw_g2=rw_g2[0], rw_k_k=rw_k_k[0],
              rw_k_a=rw_k_a[0], rw_r_k=rw_r_k[0], rw_ln_w=rw_ln_w[0], rw_ln_b=rw_ln_b[0],
              ml_conv_w=ml_conv_w[0], ml_conv_b=ml_conv_b[0], ml_i_b=ml_i_b[0], ml_f_b=ml_f_b[0],
              ml_norm_w=ml_norm_w[0], norm2=ev_norm2_w[0], ffn_w_gate=ffn_w_gate[0], ffn_w_up=ffn_w_up[0],
              ffn_w_down=ffn_w_down[0])
    od = dict(norm1=od_norm1_w[0], w_in=od_w_in[0], w_out=od_w_out[0], hg_lb_logits=hg_lb_logits,
              hg_norm_w=hg_norm_w[0], mb_conv_w=mb_conv_w[0], mb_conv_b=mb_conv_b[0], mb_dt_bias=mb_dt_bias[0],
              mb_A_log=mb_A_log[0], mb_D=mb_D[0], mb_norm_w=mb_norm_w[0], norm2=od_norm2_w[0],
              moe_router=moe_router[0], moe_w_gate=moe_w_gate[0], moe_w_up=moe_w_up[0],
              moe_w_down=moe_w_down[0])
    xf = _even_layer(xf, B, T, ev)
    xf = _odd_layer(xf, B, T, od, final_norm_w)
    return xf.reshape(B, T, D)
```

```python
import functools
import math

import jax
import jax.numpy as jnp
from jax import lax
from jax.experimental import pallas as pl
from jax.experimental.pallas import tpu as pltpu

F32 = jnp.float32
BF16 = jnp.bfloat16

D_MODEL = 2048
NORM_EPS = 1e-6
RW_HEAD = 64
RW_W = 1024
RW_H = RW_W // RW_HEAD
RW_DECAY_LORA = 96
RW_ICLR_LORA = 96
RW_GATE_LORA = 256
RW_IN = 3 * RW_W + RW_DECAY_LORA + RW_ICLR_LORA + RW_GATE_LORA
RW_LN_EPS = 64e-5
RW_DECAY_SCALE = math.exp(-0.5)
ML_W = 1024
ML_H = 4
ML_DV = ML_W // ML_H
ML_DK = ML_DV // 2
ML_QK = ML_H * ML_DK
ML_IN = 2 * ML_QK + 2 * ML_W + 2 * ML_H
HG_W = 1024
HG_HEAD = 128
HG_H = HG_W // HG_HEAD
HG_IN = 4 * HG_W
MB_W = 1024
MB_HEAD = 64
MB_H = MB_W // MB_HEAD
MB_G = 4
MB_N = 128
MB_CONV_W = MB_W + 2 * MB_G * MB_N
N_EXPERTS = 8

LANES = 128
SUBLANES = 8
VMEM_LIMIT = 56 * 1024 * 1024

TM_PROJ = 1024
TN_PROJ = 512
TM_FFN = 512
TF_FFN = 512
TM_MOE = 256
TF_MOE = 1408
TN_MOE = 1024
RW_L = 64
RW_BLK = 16
ML_L = 128
HG_L = 64
MB_L = 128
CONV_K = 4


def _cparams(sem):
    return pltpu.CompilerParams(dimension_semantics=sem, vmem_limit_bytes=VMEM_LIMIT)


def _dot(a, b):
    return jnp.dot(a, b, preferred_element_type=F32)


def _dot_nt(a, b):
    return lax.dot_general(a, b, (((1,), (1,)), ((), ())), preferred_element_type=F32)


def _dot_tn(a, b):
    return lax.dot_general(a, b, (((0,), (0,)), ((), ())), preferred_element_type=F32)


def _bf(x):
    return x.astype(BF16)


def _split2(x):
    h = x.astype(BF16)
    l = (x - h.astype(F32)).astype(BF16)
    return h, l


def _split3(x):
    h = x.astype(BF16)
    r = x - h.astype(F32)
    m = r.astype(BF16)
    l = (r - m.astype(F32)).astype(BF16)
    return h, m, l


def _dot_sel(sel_bf16, x):
    h, m, l = _split3(x)
    return _dot(sel_bf16, h) + _dot(sel_bf16, m) + _dot(sel_bf16, l)


def _dot_sel_r(x, sel_bf16):
    h, m, l = _split3(x)
    return _dot(h, sel_bf16) + _dot(m, sel_bf16) + _dot(l, sel_bf16)


def _dot3(a, b):
    ah, al = _split2(a)
    bh, bl = _split2(b)
    return _dot(ah, bh) + _dot(ah, bl) + _dot(al, bh)


def _sigmoid(x):
    return 1.0 / (1.0 + jnp.exp(-x))


def _silu(x):
    return x * _sigmoid(x)


def _log_sigmoid(x):
    return -(jnp.maximum(-x, 0.0) + jnp.log1p(jnp.exp(-jnp.abs(x))))


def _softplus(x):
    return jnp.maximum(x, 0.0) + jnp.log1p(jnp.exp(-jnp.abs(x)))


def _iota(shape, dim):
    return lax.broadcasted_iota(jnp.int32, shape, dim)


def _idiv(x, d):
    sh = d.bit_length() - 1
    assert d == 1 << sh
    return lax.shift_right_logical(x, jnp.int32(sh))


def _shift_rows(tail, x, j):
    xc = jnp.concatenate([tail, x], axis=0)
    return pltpu.roll(xc, j, 0)[SUBLANES:]


def _norm_matmul_kernel(x_ref, nw_ref, w_ref, wt_ref, o_ref, ot_ref, h_ref):
    @pl.when(pl.program_id(1) == 0)
    def _():
        x = x_ref[...]
        ms = jnp.mean(x * x, axis=-1, keepdims=True)
        h_ref[...] = _bf(x * lax.rsqrt(ms + NORM_EPS) * nw_ref[...])
        ot_ref[...] = _dot_nt(wt_ref[...], h_ref[...])

    o_ref[...] = _dot(h_ref[...], w_ref[...])


def _norm_matmul(x, nw, w, wt):
    m, d = x.shape
    n = w.shape[1]
    r = wt.shape[0]
    return pl.pallas_call(
        _norm_matmul_kernel,
        grid=(m // TM_PROJ, n // TN_PROJ),
        in_specs=[pl.BlockSpec((TM_PROJ, d), lambda i, j: (i, 0)),
                  pl.BlockSpec((1, d), lambda i, j: (0, 0)),
                  pl.BlockSpec((d, TN_PROJ), lambda i, j: (0, j)),
                  pl.BlockSpec((r, d), lambda i, j: (0, 0))],
        out_specs=[pl.BlockSpec((TM_PROJ, TN_PROJ), lambda i, j: (i, j)),
                   pl.BlockSpec((r, TM_PROJ), lambda i, j: (0, i))],
        out_shape=[jax.ShapeDtypeStruct((m, n), F32),
                   jax.ShapeDtypeStruct((r, m), F32)],
        scratch_shapes=[pltpu.VMEM((TM_PROJ, d), BF16)],
        compiler_params=_cparams(("parallel", "arbitrary")),
        name="norm_matmul",
    )(x, nw, w, wt)


def _matmul_res_kernel(ya_ref, yb_ref, wa_ref, wb_ref, r_ref, o_ref):
    o_ref[...] = r_ref[...] + _dot(ya_ref[...], wa_ref[...]) + _dot(yb_ref[...], wb_ref[...])


def _matmul_res(ya, yb, w, res):
    m, k = ya.shape
    n = w.shape[1]
    return pl.pallas_call(
        _matmul_res_kernel,
        grid=(m // TM_PROJ, n // TN_PROJ),
        in_specs=[pl.BlockSpec((TM_PROJ, k), lambda i, j: (i, 0)),
                  pl.BlockSpec((TM_PROJ, k), lambda i, j: (i, 0)),
                  pl.BlockSpec((k, TN_PROJ), lambda i, j: (0, j)),
                  pl.BlockSpec((k, TN_PROJ), lambda i, j: (1, j)),
                  pl.BlockSpec((TM_PROJ, TN_PROJ), lambda i, j: (i, j))],
        out_specs=pl.BlockSpec((TM_PROJ, TN_PROJ), lambda i, j: (i, j)),
        out_shape=jax.ShapeDtypeStruct((m, n), F32),
        compiler_params=_cparams(("parallel", "arbitrary")),
        name="matmul_res",
    )(ya, yb, w, w, res)


def _ffn_kernel(x_ref, nw_ref, wg_ref, wu_ref, wd_ref, o_ref, h_ref):
    f = pl.program_id(1)

    @pl.when(f == 0)
    def _():
        x = x_ref[...]
        ms = jnp.mean(x * x, axis=-1, keepdims=True)
        h_ref[...] = _bf(x * lax.rsqrt(ms + NORM_EPS) * nw_ref[...])
        o_ref[...] = x

    h = h_ref[...]
    act = _silu(_dot(h, wg_ref[...])) * _dot(h, wu_ref[...])
    o_ref[...] += _dot(_bf(act), wd_ref[...])


def _ffn(x, nw, wg, wu, wd):
    m, d = x.shape
    f = wg.shape[1]
    return pl.pallas_call(
        _ffn_kernel,
        grid=(m // TM_FFN, f // TF_FFN),
        in_specs=[pl.BlockSpec((TM_FFN, d), lambda i, j: (i, 0)),
                  pl.BlockSpec((1, d), lambda i, j: (0, 0)),
                  pl.BlockSpec((d, TF_FFN), lambda i, j: (0, j)),
                  pl.BlockSpec((d, TF_FFN), lambda i, j: (0, j)),
                  pl.BlockSpec((TF_FFN, d), lambda i, j: (j, 0))],
        out_specs=pl.BlockSpec((TM_FFN, d), lambda i, j: (i, 0)),
        out_shape=jax.ShapeDtypeStruct((m, d), F32),
        scratch_shapes=[pltpu.VMEM((TM_FFN, d), BF16)],
        compiler_params=_cparams(("parallel", "arbitrary")),
        name="ffn_swiglu",
    )(x, nw, wg, wu, wd)


def _router_kernel(x_ref, nw_ref, wr_ref, i_ref, p_ref, h_ref):
    x = x_ref[...]
    ms = jnp.mean(x * x, axis=-1, keepdims=True)
    h = x * lax.rsqrt(ms + NORM_EPS) * nw_ref[...]
    h_ref[...] = _bf(h)
    wr = wr_ref[...]
    hh, hl = _split2(h)
    wh, wl = _split2(wr)
    logits = _dot(hh, wh) + _dot(hh, wl) + _dot(hl, wh)
    lane = _iota(logits.shape, 1)
    neg = jnp.float32(-jnp.inf)
    logits = jnp.where(lane < N_EXPERTS, logits, neg)
    v1 = jnp.max(logits, axis=-1, keepdims=True)
    i1 = jnp.min(jnp.where(logits == v1, lane, LANES), axis=-1, keepdims=True)
    rest = jnp.where(lane == i1, neg, logits)
    v2 = jnp.max(rest, axis=-1, keepdims=True)
    i2 = jnp.min(jnp.where(rest == v2, lane, LANES), axis=-1, keepdims=True)
    e2 = jnp.exp(v2 - v1)
    p1 = 1.0 / (1.0 + e2)
    p2 = e2 / (1.0 + e2)
    i_ref[...] = jnp.where(lane == 0, i1, jnp.where(lane == 1, i2, 0))
    p_ref[...] = jnp.where(lane == 0, p1, jnp.where(lane == 1, p2, 0.0))


def _router(x, nw, wr):
    m, d = x.shape
    return pl.pallas_call(
        _router_kernel,
        grid=(m // TM_FFN,),
        in_specs=[pl.BlockSpec((TM_FFN, d), lambda i: (i, 0)),
                  pl.BlockSpec((1, d), lambda i: (0, 0)),
                  pl.BlockSpec((d, LANES), lambda i: (0, 0))],
        out_specs=[pl.BlockSpec((TM_FFN, LANES), lambda i: (i, 0)),
                   pl.BlockSpec((TM_FFN, LANES), lambda i: (i, 0)),
                   pl.BlockSpec((TM_FFN, d), lambda i: (i, 0))],
        out_shape=[jax.ShapeDtypeStruct((m, LANES), jnp.int32),
                   jax.ShapeDtypeStruct((m, LANES), F32),
                   jax.ShapeDtypeStruct((m, d), BF16)],
        compiler_params=_cparams(("parallel",)),
        name="moe_router",
    )(x, nw, wr)


def _route_plan(top_idx, tm):
    m = top_idx.shape[0]
    n_rows = 2 * m + N_EXPERTS * tm
    e_flat = top_idx.reshape(-1)
    onehot = (e_flat[:, None] == jnp.arange(N_EXPERTS, dtype=jnp.int32)[None, :]).astype(jnp.int32)
    rank = jnp.cumsum(onehot, axis=0) - onehot
    counts = jnp.sum(onehot, axis=0)
    padded = ((counts + tm - 1) // tm) * tm
    ends = jnp.cumsum(padded)
    off = ends - padded
    pos = jnp.sum(onehot * (off[None, :] + rank), axis=1)
    src_tok = jnp.zeros((n_rows,), jnp.int32).at[pos].set(jnp.arange(2 * m, dtype=jnp.int32) // 2)
    tile_start = jnp.arange(n_rows // tm, dtype=jnp.int32) * tm
    tile_expert = jnp.minimum(jnp.sum((tile_start[:, None] >= ends[None, :]).astype(jnp.int32), axis=1),
                              N_EXPERTS - 1)
    tile_active = (tile_start < ends[-1]).astype(jnp.int32)
    return pos.reshape(m, 2), src_tok, tile_expert, tile_active


def _moe_up_kernel(te_ref, ta_ref, x_ref, wg_ref, wu_ref, a_ref):
    i = pl.program_id(1)

    @pl.when(ta_ref[i] == 1)
    def _():
        x = x_ref[...]
        act = _silu(_dot(x, wg_ref[0])) * _dot(x, wu_ref[0])
        a_ref[...] = _bf(act)

    @pl.when(ta_ref[i] == 0)
    def _():
        a_ref[...] = jnp.zeros_like(a_ref)


def _moe_up(xg, tile_expert, tile_active, wg, wu):
    n_rows, d = xg.shape
    fe = wg.shape[2]
    grid_spec = pltpu.PrefetchScalarGridSpec(
        num_scalar_prefetch=2,
        grid=(fe // TF_MOE, n_rows // TM_MOE),
        in_specs=[pl.BlockSpec((TM_MOE, d), lambda f, i, te, ta: (i, 0)),
                  pl.BlockSpec((1, d, TF_MOE), lambda f, i, te, ta: (te[i], 0, f)),
                  pl.BlockSpec((1, d, TF_MOE), lambda f, i, te, ta: (te[i], 0, f))],
        out_specs=pl.BlockSpec((TM_MOE, TF_MOE), lambda f, i, te, ta: (i, f)),
    )
    return pl.pallas_call(
        _moe_up_kernel,
        grid_spec=grid_spec,
        out_shape=jax.ShapeDtypeStruct((n_rows, fe), BF16),
        compiler_params=_cparams(("arbitrary", "arbitrary")),
        name="moe_up",
    )(tile_expert, tile_active, xg, wg, wu)


def _moe_down_kernel(te_ref, ta_ref, a_ref, wd_ref, y_ref):
    i = pl.program_id(1)

    @pl.when(ta_ref[i] == 1)
    def _():
        y_ref[...] = _bf(_dot(a_ref[...], wd_ref[0]))

    @pl.when(ta_ref[i] == 0)
    def _():
        y_ref[...] = jnp.zeros_like(y_ref)


def _moe_down(act, tile_expert, tile_active, wd):
    n_rows, fe = act.shape
    d = wd.shape[2]
    grid_spec = pltpu.PrefetchScalarGridSpec(
        num_scalar_prefetch=2,
        grid=(d // TN_MOE, n_rows // TM_MOE),
        in_specs=[pl.BlockSpec((TM_MOE, fe), lambda n, i, te, ta: (i, 0)),
                  pl.BlockSpec((1, fe, TN_MOE), lambda n, i, te, ta: (te[i], 0, n))],
        out_specs=pl.BlockSpec((TM_MOE, TN_MOE), lambda n, i, te, ta: (i, n)),
    )
    return pl.pallas_call(
        _moe_down_kernel,
        grid_spec=grid_spec,
        out_shape=jax.ShapeDtypeStruct((n_rows, d), BF16),
        compiler_params=_cparams(("arbitrary", "arbitrary")),
        name="moe_down",
    )(tile_expert, tile_active, act, wd)


def _combine_kernel(x_ref, y0_ref, y1_ref, p_ref, fw_ref, o_ref):
    p = p_ref[...]
    y = x_ref[...] + p[:, 0:1] * y0_ref[...].astype(F32) + p[:, 1:2] * y1_ref[...].astype(F32)
    ms = jnp.mean(y * y, axis=-1, keepdims=True)
    o_ref[...] = y * lax.rsqrt(ms + NORM_EPS) * fw_ref[...]


def _combine(x, y0, y1, prob, final_w):
    m, d = x.shape
    row = pl.BlockSpec((TM_FFN, d), lambda i: (i, 0))
    return pl.pallas_call(
        _combine_kernel,
        grid=(m // TM_FFN,),
        in_specs=[row, row, row, pl.BlockSpec((TM_FFN, LANES), lambda i: (i, 0)),
                  pl.BlockSpec((1, d), lambda i: (0, 0))],
        out_specs=row,
        out_shape=jax.ShapeDtypeStruct((m, d), F32),
        compiler_params=_cparams(("parallel",)),
        name="moe_combine",
    )(x, y0, y1, prob, final_w)


def _rwkv_kernel(r_ref, k_ref, v_ref, dw_ref, da_ref, dg_ref,
                 mur_ref, muk_ref, muv_ref, mudw_ref, muda_ref, mudg_ref,
                 w0_ref, w2_ref, a0_ref, a2_ref, g2_ref, kk_ref, ka_ref, rk_ref, lnw_ref, lnb_ref,
                 o_ref,
                 H_ref, tr_ref, tk_ref, tv_ref, tdw_ref, tda_ref, tdg_ref):
    L = RW_L
    L2 = 2 * L

    @pl.when(pl.program_id(1) == 0)
    def _():
        H_ref[...] = jnp.zeros_like(H_ref)
        for t in (tr_ref, tk_ref, tv_ref, tdw_ref, tda_ref, tdg_ref):
            t[...] = jnp.zeros_like(t)

    def shift_mix(ref, tail, mu):
        x = ref[...]
        prev = _shift_rows(tail[...], x, 1)
        tail[...] = x[L - SUBLANES:]
        return x + (prev - x) * mu[...]

    r = shift_mix(r_ref, tr_ref, mur_ref)
    k = shift_mix(k_ref, tk_ref, muk_ref)
    v = shift_mix(v_ref, tv_ref, muv_ref)
    dw = shift_mix(dw_ref, tdw_ref, mudw_ref)
    da = shift_mix(da_ref, tda_ref, muda_ref)
    dg = shift_mix(dg_ref, tdg_ref, mudg_ref)

    log_w = -RW_DECAY_SCALE * _sigmoid(w0_ref[...] + _dot(_bf(jnp.tanh(dw)), w2_ref[...]))
    a = _sigmoid(a0_ref[...] + _dot(_bf(da), a2_ref[...]))
    g = _dot(_bf(_sigmoid(dg)), g2_ref[...])

    ri = _iota((LANES, LANES), 0)
    ci = _iota((LANES, LANES), 1)
    hsum = jnp.where(_idiv(ri, RW_HEAD) == _idiv(ci, RW_HEAD), 1.0, 0.0).astype(BF16)
    same_head = _idiv(ri, L) == _idiv(ci, L)
    tril_s = jnp.where(same_head & (ri > ci), 1.0, 0.0)
    tril_i = jnp.where(same_head & (ri >= ci), 1.0, 0.0)
    blk_d = jnp.where(_idiv(ri, RW_BLK) == _idiv(ci, RW_BLK), 1.0, 0.0)
    eye = jnp.where(ri == ci, 1.0, 0.0)
    cum = jnp.where(_iota((L, L), 0) >= _iota((L, L), 1), 1.0, 0.0).astype(BF16)
    lane = _iota((1, LANES), 1)
    m0 = lane < RW_HEAD

    def head_sum(x):
        h, l = _split2(x)
        return _dot(h, hsum) + _dot(l, hsum)

    kk = k * kk_ref[...]
    k2 = k * (1.0 + (a - 1.0) * ka_ref[...])
    cw = _dot_sel(cum, log_w)
    e_prev = jnp.exp(cw - log_w)
    e_inv = jnp.exp(-cw)
    e_cw = jnp.exp(cw)
    wl = cw[L - 1:L]
    e_end = jnp.exp(wl - cw)
    e_wl = jnp.exp(wl)
    rk2 = r * k2 * rk_ref[...]

    def stack(x):
        x0 = jnp.where(m0, x, 0.0)
        return jnp.concatenate([x0, x - x0], axis=0)

    def mm3(xs, ys):
        return _dot(xs[0], ys[0]) + _dot(xs[0], ys[1]) + _dot(xs[1], ys[0])

    NP = RW_H // 2
    sls = [slice(p * LANES, (p + 1) * LANES) for p in range(NP)]

    def each(fn, *lists):
        return [fn(*args) for args in zip(*lists)]

    kk_l = [kk[:, sl] for sl in sls]
    kk_l = each(lambda x: x * lax.rsqrt(jnp.maximum(head_sum(x * x), 1e-12)), kk_l)
    kka_l = [x * a[:, sl] for x, sl in zip(kk_l, sls)]
    At = [_bf(stack(-x * e_prev[:, sl])) for x, sl in zip(kk_l, sls)]
    Bt = [_bf(stack(x * e_inv[:, sl])) for x, sl in zip(kka_l, sls)]
    Kt = [_bf(stack(k2[:, sl] * e_inv[:, sl])) for sl in sls]
    Rt_f = [stack(r[:, sl] * e_cw[:, sl]) for sl in sls]
    Rt = each(_bf, Rt_f)
    Bh = [_bf(stack(x * e_end[:, sl])) for x, sl in zip(kka_l, sls)]
    Kh = [_bf(stack(k2[:, sl] * e_end[:, sl])) for sl in sls]
    Vs = [_bf(stack(v[:, sl])) for sl in sls]

    Mab = each(lambda x, y: _dot_nt(x, y) * tril_s, At, Bt)
    Mak = each(lambda x, y: _bf(_dot_nt(x, y) * tril_s), At, Kt)
    Arb = each(lambda x, y: _bf(_dot_nt(x, y) * tril_i), Rt, Bt)
    Ark = each(lambda x, y: _bf(_dot_nt(x, y) * tril_i), Rt, Kt)

    MD = each(lambda m: m * blk_d, Mab)
    Nn = each(lambda m, d: _split2(m - d), Mab, MD)
    P = each(lambda d: eye + d, MD)
    MDs = each(_split2, MD)
    S = each(mm3, MDs, MDs)
    for it in range(3):
        Ss = each(_split2, S)
        P = each(lambda p_, s_: p_ + mm3(_split2(p_), s_), P, Ss)
        if it < 2:
            S = each(mm3, Ss, Ss)
    Ps = each(_split2, P)
    X = each(mm3, Ps, Nn)
    Xs = each(_split2, X)
    X2 = each(lambda x: _split2(mm3(x, x)), Xs)
    Y = each(lambda x: eye + x, X)
    Y = each(lambda y_, x2: y_ + mm3(_split2(y_), x2), Y, X2)
    T = each(lambda y_, p_: _split2(mm3(_split2(y_), p_)), Y, Ps)

    P1 = each(lambda t, x: _bf(_dot(t[0], x) + _dot(t[1], x)), T, At)
    MV = each(lambda m, x: _split2(_dot(m, x)), Mak, Vs)
    P2 = each(lambda t, x: _bf(mm3(t, x)), T, MV)
    G = [eye * e_wl[:, sl] + _dot_tn(b, p1) for sl, b, p1 in zip(sls, Bh, P1)]
    J = each(lambda b, p2, kh, vs: _dot_tn(b, p2) + _dot_tn(kh, vs), Bh, P2, Kh, Vs)
    Q = each(lambda rf, ar, p1: rf + _dot(ar, p1), Rt_f, Arb, P1)
    Z = each(lambda ar, p2, ak, vs: _dot(ar, p2) + _dot(ak, vs), Arb, P2, Ark, Vs)

    Hs = [_split2(H_ref[p]) for p in range(NP)]
    Yst = each(lambda q, h, z: mm3(_split2(q), h) + z, Q, Hs, Z)
    Hn = each(lambda g_, h, j: mm3(_split2(g_), h) + j, G, Hs, J)
    for p in range(NP):
        H_ref[p] = Hn[p]
    y_l = each(lambda y_: y_[:L] + y_[L:], Yst)

    mean = each(lambda y_: head_sum(y_) * (1.0 / RW_HEAD), y_l)
    yc = each(lambda y_, m: y_ - m, y_l, mean)
    var = each(lambda c: head_sum(c * c) * (1.0 / RW_HEAD), yc)
    yn = [c * lax.rsqrt(vr + RW_LN_EPS) * lnw_ref[:, sl] + lnb_ref[:, sl] for c, vr, sl in zip(yc, var, sls)]
    bonus = [head_sum(rk2[:, sl]) * v[:, sl] for sl in sls]
    for p in range(NP):
        o_ref[:, sls[p]] = _bf((yn[p] + bonus[p]) * g[:, sls[p]])


def _rwkv(u, offs, B, T, prm):
    L = RW_L
    nc = T // L

    def col(off, w):
        return pl.BlockSpec((L, w), lambda b, c: (b * nc + c, off // w))

    def par(w, rows=1):
        return pl.BlockSpec((rows, w), lambda b, c: (0, 0))

    in_specs = [col(offs["rw_r"], RW_W), col(offs["rw_k"], RW_W), col(offs["rw_v"], RW_W),
                col(offs["rw_dw"], LANES), col(offs["rw_da"], LANES), col(offs["rw_dg"], RW_GATE_LORA),
                par(RW_W), par(RW_W), par(RW_W), par(LANES), par(LANES), par(RW_GATE_LORA),
                par(RW_W), par(RW_W, LANES), par(RW_W), par(RW_W, LANES), par(RW_W, RW_GATE_LORA),
                par(RW_W), par(RW_W), par(RW_W), par(RW_W), par(RW_W)]
    return pl.pallas_call(
        _rwkv_kernel,
        grid=(B, nc),
        in_specs=in_specs,
        out_specs=pl.BlockSpec((L, RW_W), lambda b, c: (b * nc + c, 0)),
        out_shape=jax.ShapeDtypeStruct((B * T, RW_W), BF16),
        scratch_shapes=[pltpu.VMEM((RW_H // 2, LANES, LANES), F32),
                        pltpu.VMEM((SUBLANES, RW_W), F32), pltpu.VMEM((SUBLANES, RW_W), F32),
                        pltpu.VMEM((SUBLANES, RW_W), F32), pltpu.VMEM((SUBLANES, LANES), F32),
                        pltpu.VMEM((SUBLANES, LANES), F32), pltpu.VMEM((SUBLANES, RW_GATE_LORA), F32)],
        compiler_params=_cparams(("parallel", "arbitrary")),
        name="rwkv7",
    )(u, u, u, u, u, u, *prm)


def _mlstm_kernel(qk_ref, v_ref, o_ref, gc_ref, gr_ref, cw_ref, cb_ref, gbr_ref, gbc_ref, nw_ref,
                  y_ref, C_ref, m_ref, tail_ref):
    L = ML_L
    DK, DV = ML_DK, ML_DV

    @pl.when(pl.program_id(1) == 0)
    def _():
        C_ref[...] = jnp.zeros_like(C_ref)
        m_ref[...] = jnp.zeros_like(m_ref)
        tail_ref[...] = jnp.zeros_like(tail_ref)

    x = qk_ref[...]
    tail = tail_ref[...]
    acc = x * cw_ref[CONV_K - 1:CONV_K] + cb_ref[...]
    for j in range(1, CONV_K):
        acc = acc + _shift_rows(tail, x, j) * cw_ref[CONV_K - 1 - j:CONV_K - j]
    tail_ref[...] = x[L - SUBLANES:]
    qk = _silu(acc)

    gc = gc_ref[...] + gbr_ref[...]
    gr = gr_ref[...] + gbc_ref[...]
    fl_c = _log_sigmoid(gc)
    fl_r = _log_sigmoid(gr)
    ri = _iota((L, L), 0)
    ci = _iota((L, L), 1)
    causal = ri >= ci
    tril = jnp.where(causal, 1.0, 0.0).astype(BF16)
    triu = jnp.where(ri <= ci, 1.0, 0.0).astype(BF16)
    b_c = _dot_sel(tril, fl_c)
    b_r = _dot_sel_r(fl_r, triu)
    one_col = jnp.where(_iota((L, LANES), 1) == 0, 1.0, 0.0)
    neg = jnp.float32(-jnp.inf)
    vv = v_ref[...]
    oo = o_ref[...]

    for h in range(ML_H):
        q_h = _bf(qk[:, h * DK:(h + 1) * DK] * (DK ** -0.5))
        k_f = qk[:, ML_QK + h * DK:ML_QK + (h + 1) * DK]
        k_h = _bf(k_f)
        v_ext = _bf(jnp.concatenate([vv[:, h * DV:(h + 1) * DV], one_col], axis=1))
        bc = b_c[:, ML_H + h:ML_H + h + 1]
        br = b_r[ML_H + h:ML_H + h + 1, :]
        il_c = gc[:, h:h + 1]
        il_r = gr[h:h + 1, :]
        m_prev = m_ref[h:h + 1, 0:1]
        C_prev = C_ref[h]

        D = jnp.where(causal, bc - br + il_r, neg)
        m_t = jnp.maximum(bc + m_prev, jnp.max(D, axis=-1, keepdims=True))
        S = _dot_nt(q_h, k_h) * jnp.exp(D - m_t)
        inter = jnp.exp(bc + m_prev - m_t)
        num = inter * _dot(q_h, _bf(C_prev)) + _dot(_bf(S), v_ext)
        den = num[:, DV:DV + 1]
        hh = num[:, :DV] / jnp.maximum(jnp.abs(den), jnp.exp(-m_t))
        ms = jnp.mean(hh * hh, axis=-1, keepdims=True)
        hn = hh * lax.rsqrt(ms + NORM_EPS) * nw_ref[:, h * DV:(h + 1) * DV]
        y_ref[:, h * DV:(h + 1) * DV] = _bf(hn * _sigmoid(oo[:, h * DV:(h + 1) * DV]))

        g_end = bc[L - 1:L]
        wst = g_end - bc + il_c
        m_new = jnp.maximum(g_end + m_prev, jnp.max(wst, axis=0, keepdims=True))
        kt = _bf(k_f * jnp.exp(wst - m_new))
        C_ref[h] = jnp.exp(g_end + m_prev - m_new) * C_prev + _dot_tn(kt, v_ext)
        m_ref[h:h + 1, :] = jnp.broadcast_to(m_new, (1, LANES))


def _mlstm(u, g_row, offs, B, T, prm):
    L = ML_L
    nc = T // L

    def col(off, w):
        return pl.BlockSpec((L, w), lambda b, c: (b * nc + c, off // w))

    def par(r, w):
        return pl.BlockSpec((r, w), lambda b, c: (0, 0))

    in_specs = [col(offs["ml_qk"], 2 * ML_QK), col(offs["ml_v"], ML_W), col(offs["ml_o"], ML_W),
                col(offs["ml_if"], LANES),
                pl.BlockSpec((SUBLANES, L), lambda b, c: (0, b * nc + c)),
                par(CONV_K, 2 * ML_QK), par(1, 2 * ML_QK), par(1, LANES), par(SUBLANES, 1), par(1, ML_W)]
    return pl.pallas_call(
        _mlstm_kernel,
        grid=(B, nc),
        in_specs=in_specs,
        out_specs=pl.BlockSpec((L, ML_W), lambda b, c: (b * nc + c, 0)),
        out_shape=jax.ShapeDtypeStruct((B * T, ML_W), BF16),
        scratch_shapes=[pltpu.VMEM((ML_H, ML_DK, ML_DV + LANES), F32),
                        pltpu.VMEM((SUBLANES, LANES), F32),
                        pltpu.VMEM((SUBLANES, 2 * ML_QK), F32)],
        compiler_params=_cparams(("parallel", "arbitrary")),
        name="mlstm",
    )(u, u, u, u, g_row, *prm)


def _hgrn_level_matrices():
    L = HG_L
    t = jnp.arange(L)[:, None]
    r = jnp.arange(L)[None, :]
    mats = [(r <= t)]
    for l in range(int(math.log2(L))):
        half = 1 << l
        base = (t // (2 * half)) * (2 * half)
        bnd = base + half - 1
        upper = (t - base) >= half
        m_up = upper & (r > bnd) & (r <= t)
        m_lo = (~upper) & (r > t) & (r <= bnd)
        mats.append(m_up | m_lo)
    return jnp.concatenate(mats, axis=0).astype(BF16)


def _hgrn_masks():
    L = HG_L
    t = jnp.arange(L)[:, None]
    s = jnp.arange(L)[None, :]
    ms = [(t == s)]
    for l in range(int(math.log2(L))):
        half = 1 << l
        same = (t // (2 * half)) == (s // (2 * half))
        ms.append(same & ((t % (2 * half)) >= half) & ((s % (2 * half)) < half))
    return jnp.stack(ms).astype(F32)


def _hgrn_kernel(q_ref, f_ref, i_ref, g_ref, lb_ref, lvl_ref, msk_ref, nw_ref, y_ref, S_ref):
    L = HG_L
    nl = int(math.log2(L))

    @pl.when(pl.program_id(1) == 0)
    def _():
        S_ref[...] = jnp.zeros_like(S_ref)

    lg = lb_ref[...]
    mx = jnp.max(lg, axis=0, keepdims=True)
    ex = jnp.exp(lg - mx)
    pr = ex / jnp.sum(ex, axis=0, keepdims=True)
    lb = (pr[0:1] + pr[1:2]) - pr[0:1]

    q = _silu(q_ref[...])
    fp = f_ref[...]
    iv = i_ref[...]
    a1 = jnp.log(lb)
    a2 = jnp.log1p(-lb) + _log_sigmoid(fp)
    log_f = jnp.maximum(a1, a2) + jnp.log1p(jnp.exp(-jnp.abs(a1 - a2)))
    k = (1.0 - lb) * _sigmoid(-fp)

    E = _dot_sel(lvl_ref[...], log_f)
    bcum = E[0:L]
    b_last = bcum[L - 1:L]
    qb = q * jnp.exp(bcum)
    kd = k * jnp.exp(b_last - bcum)
    e_last = jnp.exp(b_last)
    qs = [q]
    ks = [k]
    for l in range(nl):
        A = jnp.exp(E[(l + 1) * L:(l + 2) * L])
        qs.append(q * A)
        ks.append(k * A)
    gg = g_ref[...]

    for h in range(HG_H):
        sl = slice(h * HG_HEAD, (h + 1) * HG_HEAD)
        attn = jnp.zeros((L, L), F32)
        for l in range(nl + 1):
            attn = attn + msk_ref[l] * _dot_nt(_bf(qs[l][:, sl]), _bf(ks[l][:, sl]))
        i_h = _bf(iv[:, sl])
        St = S_ref[h]
        o = _dot(_bf(attn), i_h) + _dot_nt(_bf(qb[:, sl]), _bf(St))
        S_ref[h] = St * e_last[:, sl] + _dot_tn(i_h, _bf(kd[:, sl]))
        ms = jnp.mean(o * o, axis=-1, keepdims=True)
        on = o * lax.rsqrt(ms + NORM_EPS) * nw_ref[:, sl]
        y_ref[:, sl] = _bf(on * _silu(gg[:, sl]))


def _hgrn(u, offs, B, T, prm):
    L = HG_L
    nc = T // L
    nl = int(math.log2(L))

    def col(off, w):
        return pl.BlockSpec((L, w), lambda b, c: (b * nc + c, off // w))

    in_specs = [col(offs["hg_q"], HG_W), col(offs["hg_f"], HG_W), col(offs["hg_i"], HG_W), col(offs["hg_g"], HG_W),
                pl.BlockSpec((2, HG_W), lambda b, c: (0, 0)),
                pl.BlockSpec(((nl + 1) * L, L), lambda b, c: (0, 0)),
                pl.BlockSpec((nl + 1, L, L), lambda b, c: (0, 0, 0)),
                pl.BlockSpec((1, HG_W), lambda b, c: (0, 0))]
    return pl.pallas_call(
        _hgrn_kernel,
        grid=(B, nc),
        in_specs=in_specs,
        out_specs=pl.BlockSpec((L, HG_W), lambda b, c: (b * nc + c, 0)),
        out_shape=jax.ShapeDtypeStruct((B * T, HG_W), BF16),
        scratch_shapes=[pltpu.VMEM((HG_H, HG_HEAD, HG_HEAD), F32)],
        compiler_params=_cparams(("parallel", "arbitrary")),
        name="hgrn2",
    )(u, u, u, u, *prm)


def _mamba_kernel(z_ref, x_ref, b_ref, c_ref, dtc_ref, dtr_ref,
                  cwx_ref, cbx_ref, cwb_ref, cbb_ref, cwc_ref, cbc_ref,
                  dbr_ref, dbc_ref, alr_ref, alc_ref, dsk_ref, nw_ref, exp_ref,
                  y_ref, S_ref, tx_ref, tb_ref, tc_ref):
    L = MB_L
    GW = MB_W // MB_G
    E = MB_H // MB_G

    @pl.when(pl.program_id(1) == 0)
    def _():
        S_ref[...] = jnp.zeros_like(S_ref)
        tx_ref[...] = jnp.zeros_like(tx_ref)
        tb_ref[...] = jnp.zeros_like(tb_ref)
        tc_ref[...] = jnp.zeros_like(tc_ref)

    def conv(ref, tail_ref, w_ref, bias_ref):
        x = ref[...]
        tail = tail_ref[...]
        acc = x * w_ref[CONV_K - 1:CONV_K] + bias_ref[...]
        for j in range(1, CONV_K):
            acc = acc + _shift_rows(tail, x, j) * w_ref[CONV_K - 1 - j:CONV_K - j]
        tail_ref[...] = x[L - SUBLANES:]
        return _silu(acc)

    xs = conv(x_ref, tx_ref, cwx_ref, cbx_ref)
    Bm = conv(b_ref, tb_ref, cwb_ref, cbb_ref)
    Cm = conv(c_ref, tc_ref, cwc_ref, cbc_ref)

    dt_c = _softplus(dtc_ref[...] + dbr_ref[...])
    dt_r = _softplus(dtr_ref[...] + dbc_ref[...])
    adt_c = dt_c * (-jnp.exp(alr_ref[...]))
    adt_r = dt_r * (-jnp.exp(alc_ref[...]))
    ri = _iota((L, L), 0)
    ci = _iota((L, L), 1)
    causal = ri >= ci
    tril = jnp.where(causal, 1.0, 0.0).astype(BF16)
    triu = jnp.where(ri <= ci, 1.0, 0.0).astype(BF16)
    ac_c = _dot_sel(tril, adt_c)
    ac_r = _dot_sel_r(adt_r, triu)
    ex = exp_ref[...]
    dt_full = _dot_sel_r(dt_c, ex)
    ac_full = _dot_sel_r(ac_c, ex)
    X = xs * dt_full
    a_last = ac_full[L - 1:L]
    dec_out = jnp.exp(ac_full)
    Xd = X * jnp.exp(a_last - ac_full)
    e_last = jnp.exp(a_last)
    neg = jnp.float32(-jnp.inf)
    lane_g = _idiv(_iota((1, GW), 1), MB_HEAD)
    zz = z_ref[...]

    for g in range(MB_G):
        gs = slice(g * GW, (g + 1) * GW)
        Bg = _bf(Bm[:, g * MB_N:(g + 1) * MB_N])
        Cg = _bf(Cm[:, g * MB_N:(g + 1) * MB_N])
        CB = _dot_nt(Cg, Bg)
        Xg = _bf(X[:, gs])
        Sg = S_ref[g]
        y = _dot(Cg, _bf(Sg)) * dec_out[:, gs]
        for e in range(E):
            h = g * E + e
            Lm = jnp.exp(jnp.where(causal, ac_c[:, h:h + 1] - ac_r[h:h + 1, :], neg))
            yd = _dot(_bf(CB * Lm), Xg)
            y = y + jnp.where(lane_g == e, yd, 0.0)
        S_ref[g] = Sg * e_last[:, gs] + _dot_tn(Bg, _bf(Xd[:, gs]))
        y = y + xs[:, gs] * dsk_ref[:, gs]
        y = y * _silu(zz[:, gs])
        ms = jnp.mean(y * y, axis=-1, keepdims=True)
        y_ref[:, gs] = _bf(y * lax.rsqrt(ms + NORM_EPS) * nw_ref[:, gs])


def _mamba(u, dt_row, offs, B, T, prm):
    L = MB_L
    nc = T // L
    GN = MB_G * MB_N

    def col(off, w):
        return pl.BlockSpec((L, w), lambda b, c: (b * nc + c, off // w))

    def par(r, w):
        return pl.BlockSpec((r, w), lambda b, c: (0, 0))

    in_specs = [col(offs["mb_z"], MB_W), col(offs["mb_x"], MB_W), col(offs["mb_b"], GN), col(offs["mb_c"], GN),
                col(offs["mb_dt"], LANES),
                pl.BlockSpec((MB_H, L), lambda b, c: (0, b * nc + c)),
                par(CONV_K, MB_W), par(1, MB_W), par(CONV_K, GN), par(1, GN), par(CONV_K, GN), par(1, GN),
                par(1, LANES), par(MB_H, 1), par(1, LANES), par(MB_H, 1), par(1, MB_W), par(1, MB_W),
                par(LANES, MB_W)]
    return pl.pallas_call(
        _mamba_kernel,
        grid=(B, nc),
        in_specs=in_specs,
        out_specs=pl.BlockSpec((L, MB_W), lambda b, c: (b * nc + c, 0)),
        out_shape=jax.ShapeDtypeStruct((B * T, MB_W), BF16),
        scratch_shapes=[pltpu.VMEM((MB_G, MB_N, MB_W // MB_G), F32),
                        pltpu.VMEM((SUBLANES, MB_W), F32),
                        pltpu.VMEM((SUBLANES, GN), F32),
                        pltpu.VMEM((SUBLANES, GN), F32)],
        compiler_params=_cparams(("parallel", "arbitrary")),
        name="mamba2",
    )(u, u, u, u, u, dt_row, *prm)


def _layout(segs, n_total):
    offs, cur = {}, 0
    for name, _, _, pw in segs:
        offs[name] = cur
        cur += pw
    assert cur <= n_total

    def pack(w, dtype=BF16):
        cols = []
        for _, s, wd, pw in segs:
            cols.append(w[:, s:s + wd])
            if pw > wd:
                cols.append(jnp.zeros((w.shape[0], pw - wd), w.dtype))
        if n_total > cur:
            cols.append(jnp.zeros((w.shape[0], n_total - cur), w.dtype))
        return jnp.concatenate(cols, axis=1).astype(dtype)

    return offs, pack


_EV_SEGS = [("rw_r", 0, RW_W, RW_W), ("rw_k", RW_W, RW_W, RW_W), ("rw_v", 2 * RW_W, RW_W, RW_W),
            ("ml_qk", RW_IN, 2 * ML_QK, 2 * ML_QK), ("ml_v", RW_IN + 2 * ML_QK, ML_W, ML_W),
            ("ml_o", RW_IN + 2 * ML_QK + ML_W, ML_W, ML_W),
            ("rw_dg", 3 * RW_W + RW_DECAY_LORA + RW_ICLR_LORA, RW_GATE_LORA, RW_GATE_LORA),
            ("rw_dw", 3 * RW_W, RW_DECAY_LORA, LANES),
            ("rw_da", 3 * RW_W + RW_DECAY_LORA, RW_ICLR_LORA, LANES),
            ("ml_if", RW_IN + 2 * ML_QK + 2 * ML_W, 2 * ML_H, LANES)]
_EV_N = 7168
_OD_SEGS = [("hg_q", 0, HG_W, HG_W), ("hg_f", HG_W, HG_W, HG_W), ("hg_i", 2 * HG_W, HG_W, HG_W),
            ("hg_g", 3 * HG_W, HG_W, HG_W),
            ("mb_z", HG_IN, MB_W, MB_W), ("mb_x", HG_IN + MB_W, MB_W, MB_W),
            ("mb_b", HG_IN + 2 * MB_W, MB_G * MB_N, MB_G * MB_N),
            ("mb_c", HG_IN + 2 * MB_W + MB_G * MB_N, MB_G * MB_N, MB_G * MB_N),
            ("mb_dt", HG_IN + MB_W + MB_CONV_W, MB_H, LANES)]
_OD_N = 7680


def _row(v, width=None):
    v = v.reshape(1, -1).astype(F32)
    if width is not None and v.shape[1] < width:
        v = jnp.pad(v, ((0, 0), (0, width - v.shape[1])))
    return v


def _pad_rows(w, rows):
    return jnp.pad(w, ((0, rows - w.shape[0]), (0, 0)))


def _even_layer(x, B, T, p):
    offs, pack = _layout(_EV_SEGS, _EV_N)
    s_if = RW_IN + 2 * ML_QK + 2 * ML_W
    u, g_row = _norm_matmul(x, _row(p["norm1"]), pack(p["w_in"]), _bf(p["w_in"][:, s_if:s_if + 2 * ML_H].T))
    mu = p["rw_mu"]
    o_dw, o_da, o_dg = 3 * RW_W, 3 * RW_W + RW_DECAY_LORA, 3 * RW_W + RW_DECAY_LORA + RW_ICLR_LORA
    rw_prm = [_row(mu[0:RW_W]), _row(mu[RW_W:2 * RW_W]), _row(mu[2 * RW_W:3 * RW_W]),
              _row(mu[o_dw:o_da], LANES), _row(mu[o_da:o_dg], LANES), _row(mu[o_dg:]),
              _row(p["rw_w0"]), _bf(_pad_rows(p["rw_w2"], LANES)),
              _row(p["rw_a0"]), _bf(_pad_rows(p["rw_a2"], LANES)), _bf(p["rw_g2"]),
              _row(p["rw_k_k"]), _row(p["rw_k_a"]), _row(p["rw_r_k"]), _row(p["rw_ln_w"]), _row(p["rw_ln_b"])]
    y_a = _rwkv(u, offs, B, T, rw_prm)
    gb =jnp.concatenate([p["ml_i_b"], p["ml_f_b"]]).astype(F32)
    ml_prm = [p["ml_conv_w"].astype(F32), _row(p["ml_conv_b"]), _row(gb, LANES), gb.reshape(-1, 1),
              _row(p["ml_norm_w"])]
    y_b = _mlstm(u, g_row, offs, B, T, ml_prm)
    x = _matmul_res(y_a, y_b, _bf(p["w_out"]), x)
    return _ffn(x, _row(p["norm2"]), _bf(p["ffn_w_gate"]), _bf(p["ffn_w_up"]), _bf(p["ffn_w_down"]))


def _odd_layer(x, B, T, p, final_w):
    offs, pack = _layout(_OD_SEGS, _OD_N)
    s_dt = HG_IN + MB_W + MB_CONV_W
    u, dt_row = _norm_matmul(x, _row(p["norm1"]), pack(p["w_in"]), _bf(p["w_in"][:, s_dt:s_dt + MB_H].T))
    hg_prm = [p["hg_lb_logits"].astype(F32), _hgrn_level_matrices(), _hgrn_masks(), _row(p["hg_norm_w"])]
    y_c = _hgrn(u, offs, B, T, hg_prm)
    cw, cb = p["mb_conv_w"].astype(F32), p["mb_conv_b"].astype(F32)
    GN = MB_G * MB_N
    expand = (jnp.arange(LANES)[:, None] == (jnp.arange(MB_W)[None, :] // MB_HEAD)).astype(BF16)
    mb_prm = [cw[:, :MB_W], _row(cb[:MB_W]), cw[:, MB_W:MB_W + GN], _row(cb[MB_W:MB_W + GN]),
              cw[:, MB_W + GN:], _row(cb[MB_W + GN:]),
              _row(p["mb_dt_bias"], LANES), p["mb_dt_bias"].astype(F32).reshape(-1, 1),
              _row(p["mb_A_log"], LANES), p["mb_A_log"].astype(F32).reshape(-1, 1),
              _row(jnp.repeat(p["mb_D"], MB_HEAD)), _row(p["mb_norm_w"]), expand]
    y_d = _mamba(u, dt_row, offs, B, T, mb_prm)
    x = _matmul_res(y_c, y_d, _bf(p["w_out"]), x)
    wr = jnp.pad(p["moe_router"].astype(F32), ((0, 0), (0, LANES - N_EXPERTS)))
    idx, prob, h = _router(x, _row(p["norm2"]), wr)
    pos, src_tok, tile_expert, tile_active = _route_plan(idx[:, :2], TM_MOE)
    h = jnp.pad(h, ((0, src_tok.shape[0] - h.shape[0]), (0, 0)))
    xg = jnp.take(h, src_tok, axis=0, mode="clip")
    act = _moe_up(xg, tile_expert, tile_active, _bf(p["moe_w_gate"]), _bf(p["moe_w_up"]))
    yg = _moe_down(act, tile_expert, tile_active, _bf(p["moe_w_down"]))
    y0 = jnp.take(yg, pos[:, 0], axis=0, mode="clip")
    y1 = jnp.take(yg, pos[:, 1], axis=0, mode="clip")
    return _combine(x, y0, y1, prob, _row(final_w))


def kernel(x, final_norm_w, hg_lb_logits, ev_norm1_w, ev_w_in, ev_w_out, rw_mu, rw_w0, rw_w2, rw_a0, rw_a2, rw_g2, rw_k_k, rw_k_a, rw_r_k, rw_ln_w, rw_ln_b, ml_conv_w, ml_conv_b, ml_i_b, ml_f_b, ml_norm_w, ev_norm2_w, ffn_w_gate, ffn_w_up, ffn_w_down, od_norm1_w, od_w_in, od_w_out, hg_norm_w, mb_conv_w, mb_conv_b, mb_dt_bias, mb_A_log, mb_D, mb_norm_w, od_norm2_w, moe_router, moe_w_gate, moe_w_up, moe_w_down):
    B, T, D = x.shape
    xf = x.reshape(B * T, D)
    ev = dict(norm1=ev_norm1_w[0], w_in=ev_w_in[0], w_out=ev_w_out[0], rw_mu=rw_mu[0], rw_w0=rw_w0[0],
              rw_w2=rw_w2[0], rw_a0=rw_a0[0], rw_a2=rw_a2[0], rw_g2=rw_g2[0], rw_k_k=rw_k_k[0],
              rw_k_a=rw_k_a[0], rw_r_k=rw_r_k[0], rw_ln_w=rw_ln_w[0], rw_ln_b=rw_ln_b[0],
              ml_conv_w=ml_conv_w[0], ml_conv_b=ml_conv_b[0], ml_i_b=ml_i_b[0], ml_f_b=ml_f_b[0],
              ml_norm_w=ml_norm_w[0], norm2=ev_norm2_w[0], ffn_w_gate=ffn_w_gate[0], ffn_w_up=ffn_w_up[0],
              ffn_w_down=ffn_w_down[0])
    od = dict(norm1=od_norm1_w[0], w_in=od_w_in[0], w_out=od_w_out[0], hg_lb_logits=hg_lb_logits,
              hg_norm_w=hg_norm_w[0], mb_conv_w=mb_conv_w[0], mb_conv_b=mb_conv_b[0], mb_dt_bias=mb_dt_bias[0],
              mb_A_log=mb_A_log[0], mb_D=mb_D[0], mb_norm_w=mb_norm_w[0], norm2=od_norm2_w[0],
              moe_router=moe_router[0], moe_w_gate=moe_w_gate[0], moe_w_up=moe_w_up[0],
              moe_w_down=moe_w_down[0])
    xf = _even_layer(xf, B, T, ev)
    xf = _odd_layer(xf, B, T, od, final_norm_w)
    return xf.reshape(B, T, D)
```

```python
import functools
import math

import jax
import jax.numpy as jnp
from jax import lax
from jax.experimental import pallas as pl
from jax.experimental.pallas import tpu as pltpu

F32 = jnp.float32
BF16 = jnp.bfloat16

D_MODEL = 2048
NORM_EPS = 1e-6
RW_HEAD = 64
RW_W = 1024
RW_H = RW_W // RW_HEAD
RW_DECAY_LORA = 96
RW_ICLR_LORA = 96
RW_GATE_LORA = 256
RW_IN = 3 * RW_W + RW_DECAY_LORA + RW_ICLR_LORA + RW_GATE_LORA
RW_LN_EPS = 64e-5
RW_DECAY_SCALE = math.exp(-0.5)
ML_W = 1024
ML_H = 4
ML_DV = ML_W // ML_H
ML_DK = ML_DV // 2
ML_QK = ML_H * ML_DK
ML_IN = 2 * ML_QK + 2 * ML_W + 2 * ML_H
HG_W = 1024
HG_HEAD = 128
HG_H = HG_W // HG_HEAD
HG_IN = 4 * HG_W
MB_W = 1024
MB_HEAD = 64
MB_H = MB_W // MB_HEAD
MB_G = 4
MB_N = 128
MB_CONV_W = MB_W + 2 * MB_G * MB_N
N_EXPERTS = 8

LANES = 128
SUBLANES = 8
VMEM_LIMIT = 56 * 1024 * 1024

TM_PROJ = 1024
TN_PROJ = 512
TM_FFN = 512
TF_FFN = 512
TM_MOE = 256
TF_MOE = 1408
TN_MOE = 1024
RW_L = 64
RW_BLK = 16
ML_L = 128
HG_L = 64
MB_L = 128
CONV_K = 4


def _cparams(sem):
    return pltpu.CompilerParams(dimension_semantics=sem, vmem_limit_bytes=VMEM_LIMIT)


def _dot(a, b):
    return jnp.dot(a, b, preferred_element_type=F32)


def _dot_nt(a, b):
    return lax.dot_general(a, b, (((1,), (1,)), ((), ())), preferred_element_type=F32)


def _dot_tn(a, b):
    return lax.dot_general(a, b, (((0,), (0,)), ((), ())), preferred_element_type=F32)


def _bf(x):
    return x.astype(BF16)


def _split2(x):
    h = x.astype(BF16)
    l = (x - h.astype(F32)).astype(BF16)
    return h, l


def _split3(x):
    h = x.astype(BF16)
    r = x - h.astype(F32)
    m = r.astype(BF16)
    l = (r - m.astype(F32)).astype(BF16)
    return h, m, l


def _dot_sel(sel_bf16, x):
    h, m, l = _split3(x)
    return _dot(sel_bf16, h) + _dot(sel_bf16, m) + _dot(sel_bf16, l)


def _dot_sel_r(x, sel_bf16):
    h, m, l = _split3(x)
    return _dot(h, sel_bf16) + _dot(m, sel_bf16) + _dot(l, sel_bf16)


def _dot3(a, b):
    ah, al = _split2(a)
    bh, bl = _split2(b)
    return _dot(ah, bh) + _dot(ah, bl) + _dot(al, bh)


def _sigmoid(x):
    return 1.0 / (1.0 + jnp.exp(-x))


def _silu(x):
    return x * _sigmoid(x)


def _log_sigmoid(x):
    return -(jnp.maximum(-x, 0.0) + jnp.log1p(jnp.exp(-jnp.abs(x))))


def _softplus(x):
    return jnp.maximum(x, 0.0) + jnp.log1p(jnp.exp(-jnp.abs(x)))


def _iota(shape, dim):
    return lax.broadcasted_iota(jnp.int32, shape, dim)


def _idiv(x, d):
    sh = d.bit_length() - 1
    assert d == 1 << sh
    return lax.shift_right_logical(x, jnp.int32(sh))


def _shift_rows(tail, x, j):
    xc = jnp.concatenate([tail, x], axis=0)
    return pltpu.roll(xc, j, 0)[SUBLANES:]


def _norm_matmul_kernel(x_ref, nw_ref, w_ref, wt_ref, o_ref, ot_ref, h_ref):
    @pl.when(pl.program_id(1) == 0)
    def _():
        x = x_ref[...]
        ms = jnp.mean(x * x, axis=-1, keepdims=True)
        h_ref[...] = _bf(x * lax.rsqrt(ms + NORM_EPS) * nw_ref[...])
        ot_ref[...] = _dot_nt(wt_ref[...], h_ref[...])

    o_ref[...] = _dot(h_ref[...], w_ref[...])


def _norm_matmul(x, nw, w, wt):
    m, d = x.shape
    n = w.shape[1]
    r = wt.shape[0]
    return pl.pallas_call(
        _norm_matmul_kernel,
        grid=(m // TM_PROJ, n // TN_PROJ),
        in_specs=[pl.BlockSpec((TM_PROJ, d), lambda i, j: (i, 0)),
                  pl.BlockSpec((1, d), lambda i, j: (0, 0)),
                  pl.BlockSpec((d, TN_PROJ), lambda i, j: (0, j)),
                  pl.BlockSpec((r, d), lambda i, j: (0, 0))],
        out_specs=[pl.BlockSpec((TM_PROJ, TN_PROJ), lambda i, j: (i, j)),
                   pl.BlockSpec((r, TM_PROJ), lambda i, j: (0, i))],
        out_shape=[jax.ShapeDtypeStruct((m, n), F32),
                   jax.ShapeDtypeStruct((r, m), F32)],
        scratch_shapes=[pltpu.VMEM((TM_PROJ, d), BF16)],
        compiler_params=_cparams(("parallel", "arbitrary")),
        name="norm_matmul",
    )(x, nw, w, wt)


def _matmul_res_kernel(ya_ref, yb_ref, wa_ref, wb_ref, r_ref, o_ref):
    o_ref[...] = r_ref[...] + _dot(ya_ref[...], wa_ref[...]) + _dot(yb_ref[...], wb_ref[...])


def _matmul_res(ya, yb, w, res):
    m, k = ya.shape
    n = w.shape[1]
    return pl.pallas_call(
        _matmul_res_kernel,
        grid=(m // TM_PROJ, n // TN_PROJ),
        in_specs=[pl.BlockSpec((TM_PROJ, k), lambda i, j: (i, 0)),
                  pl.BlockSpec((TM_PROJ, k), lambda i, j: (i, 0)),
                  pl.BlockSpec((k, TN_PROJ), lambda i, j: (0, j)),
                  pl.BlockSpec((k, TN_PROJ), lambda i, j: (1, j)),
                  pl.BlockSpec((TM_PROJ, TN_PROJ), lambda i, j: (i, j))],
        out_specs=pl.BlockSpec((TM_PROJ, TN_PROJ), lambda i, j: (i, j)),
        out_shape=jax.ShapeDtypeStruct((m, n), F32),
        compiler_params=_cparams(("parallel", "arbitrary")),
        name="matmul_res",
    )(ya, yb, w, w, res)


def _ffn_kernel(x_ref, nw_ref, wg_ref, wu_ref, wd_ref, o_ref, h_ref):
    f = pl.program_id(1)

    @pl.when(f == 0)
    def _():
        x = x_ref[...]
        ms = jnp.mean(x * x, axis=-1, keepdims=True)
        h_ref[...] = _bf(x * lax.rsqrt(ms + NORM_EPS) * nw_ref[...])
        o_ref[...] = x

    h = h_ref[...]
    act = _silu(_dot(h, wg_ref[...])) * _dot(h, wu_ref[...])
    o_ref[...] += _dot(_bf(act), wd_ref[...])


def _ffn(x, nw, wg, wu, wd):
    m, d = x.shape
    f = wg.shape[1]
    return pl.pallas_call(
        _ffn_kernel,
        grid=(m // TM_FFN, f // TF_FFN),
        in_specs=[pl.BlockSpec((TM_FFN, d), lambda i, j: (i, 0)),
                  pl.BlockSpec((1, d), lambda i, j: (0, 0)),
                  pl.BlockSpec((d, TF_FFN), lambda i, j: (0, j)),
                  pl.BlockSpec((d, TF_FFN), lambda i, j: (0, j)),
                  pl.BlockSpec((TF_FFN, d), lambda i, j: (j, 0))],
        out_specs=pl.BlockSpec((TM_FFN, d), lambda i, j: (i, 0)),
        out_shape=jax.ShapeDtypeStruct((m, d), F32),
        scratch_shapes=[pltpu.VMEM((TM_FFN, d), BF16)],
        compiler_params=_cparams(("parallel", "arbitrary")),
        name="ffn_swiglu",
    )(x, nw, wg, wu, wd)


def _router_kernel(x_ref, nw_ref, wr_ref, i_ref, p_ref, h_ref):
    x = x_ref[...]
    ms = jnp.mean(x * x, axis=-1, keepdims=True)
    h = x * lax.rsqrt(ms + NORM_EPS) * nw_ref[...]
    h_ref[...] = _bf(h)
    wr = wr_ref[...]
    hh, hl = _split2(h)
    wh, wl = _split2(wr)
    logits = _dot(hh, wh) + _dot(hh, wl) + _dot(hl, wh)
    lane = _iota(logits.shape, 1)
    neg = jnp.float32(-jnp.inf)
    logits = jnp.where(lane < N_EXPERTS, logits, neg)
    v1 = jnp.max(logits, axis=-1, keepdims=True)
    i1 = jnp.min(jnp.where(logits == v1, lane, LANES), axis=-1, keepdims=True)
    rest = jnp.where(lane == i1, neg, logits)
    v2 = jnp.max(rest, axis=-1, keepdims=True)
    i2 = jnp.min(jnp.where(rest == v2, lane, LANES), axis=-1, keepdims=True)
    e2 = jnp.exp(v2 - v1)
    p1 = 1.0 / (1.0 + e2)
    p2 = e2 / (1.0 + e2)
    i_ref[...] = jnp.where(lane == 0, i1, jnp.where(lane == 1, i2, 0))
    p_ref[...] = jnp.where(lane == 0, p1, jnp.where(lane == 1, p2, 0.0))


def _router(x, nw, wr):
    m, d = x.shape
    return pl.pallas_call(
        _router_kernel,
        grid=(m // TM_FFN,),
        in_specs=[pl.BlockSpec((TM_FFN, d), lambda i: (i, 0)),
                  pl.BlockSpec((1, d), lambda i: (0, 0)),
                  pl.BlockSpec((d, LANES), lambda i: (0, 0))],
        out_specs=[pl.BlockSpec((TM_FFN, LANES), lambda i: (i, 0)),
                   pl.BlockSpec((TM_FFN, LANES), lambda i: (i, 0)),
                   pl.BlockSpec((TM_FFN, d), lambda i: (i, 0))],
        out_shape=[jax.ShapeDtypeStruct((m, LANES), jnp.int32),
                   jax.ShapeDtypeStruct((m, LANES), F32),
                   jax.ShapeDtypeStruct((m, d), BF16)],
        compiler_params=_cparams(("parallel",)),
        name="moe_router",
    )(x, nw, wr)


def _route_plan(top_idx, tm):
    m = top_idx.shape[0]
    n_rows = 2 * m + N_EXPERTS * tm
    e_flat = top_idx.reshape(-1)
    onehot = (e_flat[:, None] == jnp.arange(N_EXPERTS, dtype=jnp.int32)[None, :]).astype(jnp.int32)
    rank = jnp.cumsum(onehot, axis=0) - onehot
    counts = jnp.sum(onehot, axis=0)
    padded = ((counts + tm - 1) // tm) * tm
    ends = jnp.cumsum(padded)
    off = ends - padded
    pos = jnp.sum(onehot * (off[None, :] + rank), axis=1)
    src_tok = jnp.zeros((n_rows,), jnp.int32).at[pos].set(jnp.arange(2 * m, dtype=jnp.int32) // 2)
    tile_start = jnp.arange(n_rows // tm, dtype=jnp.int32) * tm
    tile_expert = jnp.minimum(jnp.sum((tile_start[:, None] >= ends[None, :]).astype(jnp.int32), axis=1),
                              N_EXPERTS - 1)
    tile_active = (tile_start < ends[-1]).astype(jnp.int32)
    return pos.reshape(m, 2), src_tok, tile_expert, tile_active


def _new_weight_block(te_ref, i):
    return (i == 0) | (te_ref[i] != te_ref[jnp.maximum(i - 1, 0)])


def _moe_up_kernel(te_ref, ta_ref, x_ref, wg_ref, wu_ref, a_ref):
    i = pl.program_id(1)

    @pl.when(ta_ref[i] == 1)
    def _():
        x = x_ref[...]
        act = _silu(_dot(x, wg_ref[0])) * _dot(x, wu_ref[0])
        a_ref[...] = _bf(act)

    @pl.when(ta_ref[i] == 0)
    def _():
        a_ref[...] = jnp.zeros_like(a_ref)


def _moe_up(xg, tile_expert, tile_active, wg, wu):
    n_rows, d = xg.shape
    fe = wg.shape[2]
    grid_spec = pltpu.PrefetchScalarGridSpec(
        num_scalar_prefetch=2,
        grid=(fe // TF_MOE, n_rows // TM_MOE),
        in_specs=[pl.BlockSpec((TM_MOE, d), lambda f, i, te, ta: (i, 0)),
                  pl.BlockSpec((1, d, TF_MOE), lambda f, i, te, ta: (te[i], 0, f)),
                  pl.BlockSpec((1, d, TF_MOE), lambda f, i, te, ta: (te[i], 0, f))],
        out_specs=pl.BlockSpec((TM_MOE, TF_MOE), lambda f, i, te, ta: (i, f)),
    )
    return pl.pallas_call(
        _moe_up_kernel,
        grid_spec=grid_spec,
        out_shape=jax.ShapeDtypeStruct((n_rows, fe), BF16),
        compiler_params=_cparams(("arbitrary", "arbitrary")),
        name="moe_up",
    )(tile_expert, tile_active, xg, wg, wu)


def _moe_down_kernel(te_ref, ta_ref, a_ref, wd_ref, y_ref, wdb_ref):
    i = pl.program_id(1)

    @pl.when(_new_weight_block(te_ref, i))
    def _():
        wdb_ref[...] = _bf(wd_ref[0])

    @pl.when(ta_ref[i] == 1)
    def _():
        y_ref[...] = _bf(_dot(a_ref[...], wdb_ref[...]))

    @pl.when(ta_ref[i] == 0)
    def _():
        y_ref[...] = jnp.zeros_like(y_ref)


def _moe_down(act, tile_expert, tile_active, wd):
    n_rows, fe = act.shape
    d = wd.shape[2]
    grid_spec = pltpu.PrefetchScalarGridSpec(
        num_scalar_prefetch=2,
        grid=(d // TN_MOE, n_rows // TM_MOE),
        in_specs=[pl.BlockSpec((TM_MOE, fe), lambda n, i, te, ta: (i, 0)),
                  pl.BlockSpec((1, fe, TN_MOE), lambda n, i, te, ta: (te[i], 0, n))],
        out_specs=pl.BlockSpec((TM_MOE, TN_MOE), lambda n, i, te, ta: (i, n)),
        scratch_shapes=[pltpu.VMEM((fe, TN_MOE), BF16)],
    )
    return pl.pallas_call(
        _moe_down_kernel,
        grid_spec=grid_spec,
        out_shape=jax.ShapeDtypeStruct((n_rows, d), BF16),
        compiler_params=_cparams(("arbitrary", "arbitrary")),
        name="moe_down",
    )(tile_expert, tile_active, act, wd)


def _combine_kernel(x_ref, y0_ref, y1_ref, p_ref, fw_ref, o_ref):
    p = p_ref[...]
    y = x_ref[...] + p[:, 0:1] * y0_ref[...].astype(F32) + p[:, 1:2] * y1_ref[...].astype(F32)
    ms = jnp.mean(y * y, axis=-1, keepdims=True)
    o_ref[...] = y * lax.rsqrt(ms + NORM_EPS) * fw_ref[...]


def _combine(x, y0, y1, prob, final_w):
    m, d = x.shape
    row = pl.BlockSpec((TM_FFN, d), lambda i: (i, 0))
    return pl.pallas_call(
        _combine_kernel,
        grid=(m // TM_FFN,),
        in_specs=[row, row, row, pl.BlockSpec((TM_FFN, LANES), lambda i: (i, 0)),
                  pl.BlockSpec((1, d), lambda i: (0, 0))],
        out_specs=row,
        out_shape=jax.ShapeDtypeStruct((m, d), F32),
        compiler_params=_cparams(("parallel",)),
        name="moe_combine",
    )(x, y0, y1, prob, final_w)


def _rwkv_kernel(r_ref, k_ref, v_ref, dw_ref, da_ref, dg_ref,
                 mur_ref, muk_ref, muv_ref, mudw_ref, muda_ref, mudg_ref,
                 w0_ref, w2_ref, a0_ref, a2_ref, g2_ref, kk_ref, ka_ref, rk_ref, lnw_ref, lnb_ref,
                 o_ref,
                 H_ref, tr_ref, tk_ref, tv_ref, tdw_ref, tda_ref, tdg_ref):
    L = RW_L
    L2 = 2 * L

    @pl.when(pl.program_id(1) == 0)
    def _():
        H_ref[...] = jnp.zeros_like(H_ref)
        for t in (tr_ref, tk_ref, tv_ref, tdw_ref, tda_ref, tdg_ref):
            t[...] = jnp.zeros_like(t)

    def shift_mix(ref, tail, mu):
        x = ref[...]
        prev = _shift_rows(tail[...], x, 1)
        tail[...] = x[L - SUBLANES:]
        return x + (prev - x) * mu[...]

    r = shift_mix(r_ref, tr_ref, mur_ref)
    k = shift_mix(k_ref, tk_ref, muk_ref)
    v = shift_mix(v_ref, tv_ref, muv_ref)
    dw = shift_mix(dw_ref, tdw_ref, mudw_ref)
    da = shift_mix(da_ref, tda_ref, muda_ref)
    dg = shift_mix(dg_ref, tdg_ref, mudg_ref)

    log_w = -RW_DECAY_SCALE * _sigmoid(w0_ref[...] + _dot(_bf(jnp.tanh(dw)), w2_ref[...]))
    a = _sigmoid(a0_ref[...] + _dot(_bf(da), a2_ref[...]))
    g = _dot(_bf(_sigmoid(dg)), g2_ref[...])

    ri = _iota((LANES, LANES), 0)
    ci = _iota((LANES, LANES), 1)
    hsum = jnp.where(_idiv(ri, RW_HEAD) == _idiv(ci, RW_HEAD), 1.0, 0.0).astype(BF16)
    same_head = _idiv(ri, L) == _idiv(ci, L)
    tril_s = jnp.where(same_head & (ri > ci), 1.0, 0.0)
    tril_i = jnp.where(same_head & (ri >= ci), 1.0, 0.0)
    blk_d = jnp.where(_idiv(ri, RW_BLK) == _idiv(ci, RW_BLK), 1.0, 0.0)
    eye = jnp.where(ri == ci, 1.0, 0.0)
    cum = jnp.where(_iota((L, L), 0) >= _iota((L, L), 1), 1.0, 0.0).astype(BF16)
    lane = _iota((1, LANES), 1)
    m0 = lane < RW_HEAD

    def head_sum(x):
        h, l = _split2(x)
        return _dot(h, hsum) + _dot(l, hsum)

    kk = k * kk_ref[...]
    k2 = k * (1.0 + (a - 1.0) * ka_ref[...])
    cw = _dot_sel(cum, log_w)
    e_prev = jnp.exp(cw - log_w)
    e_inv = jnp.exp(-cw)
    e_cw = jnp.exp(cw)
    wl = cw[L - 1:L]
    e_end = jnp.exp(wl - cw)
    e_wl = jnp.exp(wl)
    rk2 = r * k2 * rk_ref[...]

    def stack(x):
        x0 = jnp.where(m0, x, 0.0)
        return jnp.concatenate([x0, x - x0], axis=0)

    def mm3(xs, ys):
        return _dot(xs[0], ys[0]) + _dot(xs[0], ys[1]) + _dot(xs[1], ys[0])

    NP = RW_H // 2
    sls = [slice(p * LANES, (p + 1) * LANES) for p in range(NP)]

    def each(fn, *lists):
        return [fn(*args) for args in zip(*lists)]

    kk_l = [kk[:, sl] for sl in sls]
    kk_l = each(lambda x: x * lax.rsqrt(jnp.maximum(head_sum(x * x), 1e-12)), kk_l)
    kka_l = [x * a[:, sl] for x, sl in zip(kk_l, sls)]
    At = [_bf(stack(-x * e_prev[:, sl])) for x, sl in zip(kk_l, sls)]
    Bt = [_bf(stack(x * e_inv[:, sl])) for x, sl in zip(kka_l, sls)]
    Kt = [_bf(stack(k2[:, sl] * e_inv[:, sl])) for sl in sls]
    Rt_f = [stack(r[:, sl] * e_cw[:, sl]) for sl in sls]
    Rt = each(_bf, Rt_f)
    Bh = [_bf(stack(x * e_end[:, sl])) for x, sl in zip(kka_l, sls)]
    Kh = [_bf(stack(k2[:, sl] * e_end[:, sl])) for sl in sls]
    Vs = [_bf(stack(v[:, sl])) for sl in sls]

    Mab = each(lambda x, y: _dot_nt(x, y) * tril_s, At, Bt)
    Mak = each(lambda x, y: _bf(_dot_nt(x, y) * tril_s), At, Kt)
    Arb = each(lambda x, y: _bf(_dot_nt(x, y) * tril_i), Rt, Bt)
    Ark = each(lambda x, y: _bf(_dot_nt(x, y) * tril_i), Rt, Kt)

    MD = each(lambda m: m * blk_d, Mab)
    Nn = each(lambda m, d: _split2(m - d), Mab, MD)
    P = each(lambda d: eye + d, MD)
    MDs = each(_split2, MD)
    S = each(mm3, MDs, MDs)
    for it in range(3):
        Ss = each(_split2, S)
        P = each(lambda p_, s_: p_ + mm3(_split2(p_), s_), P, Ss)
        if it < 2:
            S = each(mm3, Ss, Ss)
    Ps = each(_split2, P)
    X = each(mm3, Ps, Nn)
    Xs = each(_split2, X)
    X2 = each(lambda x: _split2(mm3(x, x)), Xs)
    Y = each(lambda x: eye + x, X)
    Y = each(lambda y_, x2: y_ + mm3(_split2(y_), x2), Y, X2)
    T = each(lambda y_, p_: _split2(mm3(_split2(y_), p_)), Y, Ps)

    P1 = each(lambda t, x: _bf(_dot(t[0], x) + _dot(t[1], x)), T, At)
    MV = each(lambda m, x: _split2(_dot(m, x)), Mak, Vs)
    P2 = each(lambda t, x: _bf(mm3(t, x)), T, MV)
    G = [eye * e_wl[:, sl] + _dot_tn(b, p1) for sl, b, p1 in zip(sls, Bh, P1)]
    J = each(lambda b, p2, kh, vs: _dot_tn(b, p2) + _dot_tn(kh, vs), Bh, P2, Kh, Vs)
    Q = each(lambda rf, ar, p1: rf + _dot(ar, p1), Rt_f, Arb, P1)
    Z = each(lambda ar, p2, ak, vs: _dot(ar, p2) + _dot(ak, vs), Arb, P2, Ark, Vs)

    Hs = [_split2(H_ref[p]) for p in range(NP)]
    Yst = each(lambda q, h, z: mm3(_split2(q), h) + z, Q, Hs, Z)
    Hn = each(lambda g_, h, j: mm3(_split2(g_), h) + j, G, Hs, J)
    for p in range(NP):
        H_ref[p] = Hn[p]
    y_l = each(lambda y_: y_[:L] + y_[L:], Yst)

    mean = each(lambda y_: head_sum(y_) * (1.0 / RW_HEAD), y_l)
    yc = each(lambda y_, m: y_ - m, y_l, mean)
    var = each(lambda c: head_sum(c * c) * (1.0 / RW_HEAD), yc)
    yn = [c * lax.rsqrt(vr + RW_LN_EPS) * lnw_ref[:, sl] + lnb_ref[:, sl] for c, vr, sl in zip(yc, var, sls)]
    bonus = [head_sum(rk2[:, sl]) * v[:, sl] for sl in sls]
    for p in range(NP):
        o_ref[:, sls[p]] = _bf((yn[p] + bonus[p]) * g[:, sls[p]])


def _rwkv(u, offs, B, T, prm):
    L = RW_L
    nc = T // L

    def col(off, w):
        return pl.BlockSpec((L, w), lambda b, c: (b * nc + c, off // w))

    def par(w, rows=1):
        return pl.BlockSpec((rows, w), lambda b, c: (0, 0))

    in_specs = [col(offs["rw_r"], RW_W), col(offs["rw_k"], RW_W), col(offs["rw_v"], RW_W),
                col(offs["rw_dw"], LANES), col(offs["rw_da"], LANES), col(offs["rw_dg"], RW_GATE_LORA),
                par(RW_W), par(RW_W), par(RW_W), par(LANES), par(LANES), par(RW_GATE_LORA),
                par(RW_W), par(RW_W, LANES), par(RW_W), par(RW_W, LANES), par(RW_W, RW_GATE_LORA),
                par(RW_W), par(RW_W), par(RW_W), par(RW_W), par(RW_W)]
    return pl.pallas_call(
        _rwkv_kernel,
        grid=(B, nc),
        in_specs=in_specs,
        out_specs=pl.BlockSpec((L, RW_W), lambda b, c: (b * nc + c, 0)),
        out_shape=jax.ShapeDtypeStruct((B * T, RW_W), BF16),
        scratch_shapes=[pltpu.VMEM((RW_H // 2, LANES, LANES), F32),
                        pltpu.VMEM((SUBLANES, RW_W), F32), pltpu.VMEM((SUBLANES, RW_W), F32),
                        pltpu.VMEM((SUBLANES, RW_W), F32), pltpu.VMEM((SUBLANES, LANES), F32),
                        pltpu.VMEM((SUBLANES, LANES), F32), pltpu.VMEM((SUBLANES, RW_GATE_LORA), F32)],
        compiler_params=_cparams(("parallel", "arbitrary")),
        name="rwkv7",
    )(u, u, u, u, u, u, *prm)


def _mlstm_kernel(qk_ref, v_ref, o_ref, gc_ref, gr_ref, cw_ref, cb_ref, gbr_ref, gbc_ref, nw_ref,
                  y_ref, C_ref, m_ref, tail_ref):
    L = ML_L
    DK, DV = ML_DK, ML_DV

    @pl.when(pl.program_id(1) == 0)
    def _():
        C_ref[...] = jnp.zeros_like(C_ref)
        m_ref[...] = jnp.zeros_like(m_ref)
        tail_ref[...] = jnp.zeros_like(tail_ref)

    x = qk_ref[...]
    tail = tail_ref[...]
    acc = x * cw_ref[CONV_K - 1:CONV_K] + cb_ref[...]
    for j in range(1, CONV_K):
        acc = acc + _shift_rows(tail, x, j) * cw_ref[CONV_K - 1 - j:CONV_K - j]
    tail_ref[...] = x[L - SUBLANES:]
    qk = _silu(acc)

    gc = gc_ref[...] + gbr_ref[...]
    gr = gr_ref[...] + gbc_ref[...]
    fl_c = _log_sigmoid(gc)
    fl_r = _log_sigmoid(gr)
    ri = _iota((L, L), 0)
    ci = _iota((L, L), 1)
    causal = ri >= ci
    tril = jnp.where(causal, 1.0, 0.0).astype(BF16)
    triu = jnp.where(ri <= ci, 1.0, 0.0).astype(BF16)
    b_c = _dot_sel(tril, fl_c)
    b_r = _dot_sel_r(fl_r, triu)
    one_col = jnp.where(_iota((L, LANES), 1) == 0, 1.0, 0.0)
    neg = jnp.float32(-jnp.inf)
    vv = v_ref[...]
    oo = o_ref[...]

    for h in range(ML_H):
        q_h = _bf(qk[:, h * DK:(h + 1) * DK] * (DK ** -0.5))
        k_f = qk[:, ML_QK + h * DK:ML_QK + (h + 1) * DK]
        k_h = _bf(k_f)
        v_ext = _bf(jnp.concatenate([vv[:, h * DV:(h + 1) * DV], one_col], axis=1))
        bc = b_c[:, ML_H + h:ML_H + h + 1]
        br = b_r[ML_H + h:ML_H + h + 1, :]
        il_c = gc[:, h:h + 1]
        il_r = gr[h:h + 1, :]
        m_prev = m_ref[h:h + 1, 0:1]
        C_prev = C_ref[h]

        D = jnp.where(causal, bc - br + il_r, neg)
        m_t = jnp.maximum(bc + m_prev, jnp.max(D, axis=-1, keepdims=True))
        S = _dot_nt(q_h, k_h) * jnp.exp(D - m_t)
        inter = jnp.exp(bc + m_prev - m_t)
        num = inter * _dot(q_h, _bf(C_prev)) + _dot(_bf(S), v_ext)
        den = num[:, DV:DV + 1]
        hh = num[:, :DV] / jnp.maximum(jnp.abs(den), jnp.exp(-m_t))
        ms = jnp.mean(hh * hh, axis=-1, keepdims=True)
        hn = hh * lax.rsqrt(ms + NORM_EPS) * nw_ref[:, h * DV:(h + 1) * DV]
        y_ref[:, h * DV:(h + 1) * DV] = _bf(hn * _sigmoid(oo[:, h * DV:(h + 1) * DV]))

        g_end = bc[L - 1:L]
        wst = g_end - bc + il_c
        m_new = jnp.maximum(g_end + m_prev, jnp.max(wst, axis=0, keepdims=True))
        kt = _bf(k_f * jnp.exp(wst - m_new))
        C_ref[h] = jnp.exp(g_end + m_prev - m_new) * C_prev + _dot_tn(kt, v_ext)
        m_ref[h:h + 1, :] = jnp.broadcast_to(m_new, (1, LANES))


def _mlstm(u, g_row, offs, B, T, prm):
    L = ML_L
    nc = T // L

    def col(off, w):
        return pl.BlockSpec((L, w), lambda b, c: (b * nc + c, off // w))

    def par(r, w):
        return pl.BlockSpec((r, w), lambda b, c: (0, 0))

    in_specs = [col(offs["ml_qk"], 2 * ML_QK), col(offs["ml_v"], ML_W), col(offs["ml_o"], ML_W),
                col(offs["ml_if"], LANES),
                pl.BlockSpec((SUBLANES, L), lambda b, c: (0, b * nc + c)),
                par(CONV_K, 2 * ML_QK), par(1, 2 * ML_QK), par(1, LANES), par(SUBLANES, 1), par(1, ML_W)]
    return pl.pallas_call(
        _mlstm_kernel,
        grid=(B, nc),
        in_specs=in_specs,
        out_specs=pl.BlockSpec((L, ML_W), lambda b, c: (b * nc + c, 0)),
        out_shape=jax.ShapeDtypeStruct((B * T, ML_W), BF16),
        scratch_shapes=[pltpu.VMEM((ML_H, ML_DK, ML_DV + LANES), F32),
                        pltpu.VMEM((SUBLANES, LANES), F32),
                        pltpu.VMEM((SUBLANES, 2 * ML_QK), F32)],
        compiler_params=_cparams(("parallel", "arbitrary")),
        name="mlstm",
    )(u, u, u, u, g_row, *prm)


def _hgrn_level_matrices():
    L = HG_L
    t = jnp.arange(L)[:, None]
    r = jnp.arange(L)[None, :]
    mats = [(r <= t)]
    for l in range(int(math.log2(L))):
        half = 1 << l
        base = (t // (2 * half)) * (2 * half)
        bnd = base + half - 1
        upper = (t - base) >= half
        m_up = upper & (r > bnd) & (r <= t)
        m_lo = (~upper) & (r > t) & (r <= bnd)
        mats.append(m_up | m_lo)
    return jnp.concatenate(mats, axis=0).astype(BF16)


def _hgrn_masks():
    L = HG_L
    t = jnp.arange(L)[:, None]
    s = jnp.arange(L)[None, :]
    ms = [(t == s)]
    for l in range(int(math.log2(L))):
        half = 1 << l
        same = (t // (2 * half)) == (s // (2 * half))
        ms.append(same & ((t % (2 * half)) >= half) & ((s % (2 * half)) < half))
    return jnp.stack(ms).astype(F32)


def _hgrn_kernel(q_ref, f_ref, i_ref, g_ref, lb_ref, lvl_ref, msk_ref, nw_ref, y_ref, S_ref):
    L = HG_L
    nl = int(math.log2(L))

    @pl.when(pl.program_id(1) == 0)
    def _():
        S_ref[...] = jnp.zeros_like(S_ref)

    lg = lb_ref[...]
    mx = jnp.max(lg, axis=0, keepdims=True)
    ex = jnp.exp(lg - mx)
    pr = ex / jnp.sum(ex, axis=0, keepdims=True)
    lb = (pr[0:1] + pr[1:2]) - pr[0:1]

    q = _silu(q_ref[...])
    fp = f_ref[...]
    iv = i_ref[...]
    a1 = jnp.log(lb)
    a2 = jnp.log1p(-lb) + _log_sigmoid(fp)
    log_f = jnp.maximum(a1, a2) + jnp.log1p(jnp.exp(-jnp.abs(a1 - a2)))
    k = (1.0 - lb) * _sigmoid(-fp)

    E = _dot_sel(lvl_ref[...], log_f)
    bcum = E[0:L]
    b_last = bcum[L - 1:L]
    qb = q * jnp.exp(bcum)
    kd = k * jnp.exp(b_last - bcum)
    e_last = jnp.exp(b_last)
    qs = [q]
    ks = [k]
    for l in range(nl):
        A = jnp.exp(E[(l + 1) * L:(l + 2) * L])
        qs.append(q * A)
        ks.append(k * A)
    gg = g_ref[...]

    sls = [slice(h * HG_HEAD, (h + 1) * HG_HEAD) for h in range(HG_H)]
    qsb = [_bf(x) for x in qs]
    ksb = [_bf(x) for x in ks]
    msk = [msk_ref[l] for l in range(nl + 1)]
    attn = [msk[0] * _dot_nt(qsb[0][:, sl], ksb[0][:, sl]) for sl in sls]
    for l in range(1, nl + 1):
        attn = [a + msk[l] * _dot_nt(qsb[l][:, sl], ksb[l][:, sl]) for a, sl in zip(attn, sls)]
    ivb = _bf(iv)
    qbb = _bf(qb)
    kdb = _bf(kd)
    St = [S_ref[h] for h in range(HG_H)]
    o = [_dot(_bf(a), ivb[:, sl]) + _dot_nt(qbb[:, sl], _bf(s)) for a, sl, s in zip(attn, sls, St)]
    Sn = [s * e_last[:, sl] + _dot_tn(ivb[:, sl], kdb[:, sl]) for s, sl in zip(St, sls)]
    for h in range(HG_H):
        S_ref[h] = Sn[h]
    for h, sl in enumerate(sls):
        ms = jnp.mean(o[h] * o[h], axis=-1, keepdims=True)
        on = o[h] * lax.rsqrt(ms + NORM_EPS) * nw_ref[:, sl]
        y_ref[:, sl] = _bf(on * _silu(gg[:, sl]))


def _hgrn(u, offs, B, T, prm):
    L = HG_L
    nc = T // L
    nl = int(math.log2(L))

    def col(off, w):
        return pl.BlockSpec((L, w), lambda b, c: (b * nc + c, off // w))

    in_specs = [col(offs["hg_q"], HG_W), col(offs["hg_f"], HG_W), col(offs["hg_i"], HG_W), col(offs["hg_g"], HG_W),
                pl.BlockSpec((2, HG_W), lambda b, c: (0, 0)),
                pl.BlockSpec(((nl + 1) * L, L), lambda b, c: (0, 0)),
                pl.BlockSpec((nl + 1, L, L), lambda b, c: (0, 0, 0)),
                pl.BlockSpec((1, HG_W), lambda b, c: (0, 0))]
    return pl.pallas_call(
        _hgrn_kernel,
        grid=(B, nc),
        in_specs=in_specs,
        out_specs=pl.BlockSpec((L, HG_W), lambda b, c: (b * nc + c, 0)),
        out_shape=jax.ShapeDtypeStruct((B * T, HG_W), BF16),
        scratch_shapes=[pltpu.VMEM((HG_H, HG_HEAD, HG_HEAD), F32)],
        compiler_params=_cparams(("parallel", "arbitrary")),
        name="hgrn2",
    )(u, u, u, u, *prm)


def _mamba_kernel(z_ref, x_ref, b_ref, c_ref, dtc_ref, dtr_ref,
                  cwx_ref, cbx_ref, cwb_ref, cbb_ref, cwc_ref, cbc_ref,
                  dbr_ref, dbc_ref, alr_ref, alc_ref, dsk_ref, nw_ref, exp_ref,
                  y_ref, S_ref, tx_ref, tb_ref, tc_ref):
    L = MB_L
    GW = MB_W // MB_G
    E = MB_H // MB_G

    @pl.when(pl.program_id(1) == 0)
    def _():
        S_ref[...] = jnp.zeros_like(S_ref)
        tx_ref[...] = jnp.zeros_like(tx_ref)
        tb_ref[...] = jnp.zeros_like(tb_ref)
        tc_ref[...] = jnp.zeros_like(tc_ref)

    def conv(ref, tail_ref, w_ref, bias_ref):
        x = ref[...]
        tail = tail_ref[...]
        acc = x * w_ref[CONV_K - 1:CONV_K] + bias_ref[...]
        for j in range(1, CONV_K):
            acc = acc + _shift_rows(tail, x, j) * w_ref[CONV_K - 1 - j:CONV_K - j]
        tail_ref[...] = x[L - SUBLANES:]
        return _silu(acc)

    xs = conv(x_ref, tx_ref, cwx_ref, cbx_ref)
    Bm = conv(b_ref, tb_ref, cwb_ref, cbb_ref)
    Cm = conv(c_ref, tc_ref, cwc_ref, cbc_ref)

    dt_c = _softplus(dtc_ref[...] + dbr_ref[...])
    dt_r = _softplus(dtr_ref[...] + dbc_ref[...])
    adt_c = dt_c * (-jnp.exp(alr_ref[...]))
    adt_r = dt_r * (-jnp.exp(alc_ref[...]))
    ri = _iota((L, L), 0)
    ci = _iota((L, L), 1)
    causal = ri >= ci
    tril = jnp.where(causal, 1.0, 0.0).astype(BF16)
    triu = jnp.where(ri <= ci, 1.0, 0.0).astype(BF16)
    ac_c = _dot_sel(tril, adt_c)
    ac_r = _dot_sel_r(adt_r, triu)
    ex = exp_ref[...]
    dt_full = _dot_sel_r(dt_c, ex)
    ac_full = _dot_sel_r(ac_c, ex)
    X = xs * dt_full
    a_last = ac_full[L - 1:L]
    dec_out = jnp.exp(ac_full)
    Xd = X * jnp.exp(a_last - ac_full)
    e_last = jnp.exp(a_last)
    neg = jnp.float32(-jnp.inf)
    lane_g = _idiv(_iota((1, GW), 1), MB_HEAD)
    zz = z_ref[...]

    for g in range(MB_G):
        gs = slice(g * GW, (g + 1) * GW)
        Bg = _bf(Bm[:, g * MB_N:(g + 1) * MB_N])
        Cg = _bf(Cm[:, g * MB_N:(g + 1) * MB_N])
        CB = _dot_nt(Cg, Bg)
        Xg = _bf(X[:, gs])
        Sg = S_ref[g]
        y = _dot(Cg, _bf(Sg)) * dec_out[:, gs]
        for e in range(E):
            h = g * E + e
            Lm = jnp.exp(jnp.where(causal, ac_c[:, h:h + 1] - ac_r[h:h + 1, :], neg))
            yd = _dot(_bf(CB * Lm), Xg)
            y = y + jnp.where(lane_g == e, yd, 0.0)
        S_ref[g] = Sg * e_last[:, gs] + _dot_tn(Bg, _bf(Xd[:, gs]))
        y = y + xs[:, gs] * dsk_ref[:, gs]
        y = y * _silu(zz[:, gs])
        ms = jnp.mean(y * y, axis=-1, keepdims=True)
        y_ref[:, gs] = _bf(y * lax.rsqrt(ms + NORM_EPS) * nw_ref[:, gs])


def _mamba(u, dt_row, offs, B, T, prm):
    L = MB_L
    nc = T // L
    GN = MB_G * MB_N

    def col(off, w):
        return pl.BlockSpec((L, w), lambda b, c: (b * nc + c, off // w))

    def par(r, w):
        return pl.BlockSpec((r, w), lambda b, c: (0, 0))

    in_specs = [col(offs["mb_z"], MB_W), col(offs["mb_x"], MB_W), col(offs["mb_b"], GN), col(offs["mb_c"], GN),
                col(offs["mb_dt"], LANES),
                pl.BlockSpec((MB_H, L), lambda b, c: (0, b * nc + c)),
                par(CONV_K, MB_W), par(1, MB_W), par(CONV_K, GN), par(1, GN), par(CONV_K, GN), par(1, GN),
                par(1, LANES), par(MB_H, 1), par(1, LANES), par(MB_H, 1), par(1, MB_W), par(1, MB_W),
                par(LANES, MB_W)]
    return pl.pallas_call(
        _mamba_kernel,
        grid=(B, nc),
        in_specs=in_specs,
        out_specs=pl.BlockSpec((L, MB_W), lambda b, c: (b * nc + c, 0)),
        out_shape=jax.ShapeDtypeStruct((B * T, MB_W), BF16),
        scratch_shapes=[pltpu.VMEM((MB_G, MB_N, MB_W // MB_G), F32),
                        pltpu.VMEM((SUBLANES, MB_W), F32),
                        pltpu.VMEM((SUBLANES, GN), F32),
                        pltpu.VMEM((SUBLANES, GN), F32)],
        compiler_params=_cparams(("parallel", "arbitrary")),
        name="mamba2",
    )(u, u, u, u, u, dt_row, *prm)


def _layout(segs, n_total):
    offs, cur = {}, 0
    for name, _, _, pw in segs:
        offs[name] = cur
        cur += pw
    assert cur <= n_total

    def pack(w, dtype=BF16):
        cols = []
        for _, s, wd, pw in segs:
            cols.append(w[:, s:s + wd])
            if pw > wd:
                cols.append(jnp.zeros((w.shape[0], pw - wd), w.dtype))
        if n_total > cur:
            cols.append(jnp.zeros((w.shape[0], n_total - cur), w.dtype))
        return jnp.concatenate(cols, axis=1).astype(dtype)

    return offs, pack


_EV_SEGS = [("rw_r", 0, RW_W, RW_W), ("rw_k", RW_W, RW_W, RW_W), ("rw_v", 2 * RW_W, RW_W, RW_W),
            ("ml_qk", RW_IN, 2 * ML_QK, 2 * ML_QK), ("ml_v", RW_IN + 2 * ML_QK, ML_W, ML_W),
            ("ml_o", RW_IN + 2 * ML_QK + ML_W, ML_W, ML_W),
            ("rw_dg", 3 * RW_W + RW_DECAY_LORA + RW_ICLR_LORA, RW_GATE_LORA, RW_GATE_LORA),
            ("rw_dw", 3 * RW_W, RW_DECAY_LORA, LANES),
            ("rw_da", 3 * RW_W + RW_DECAY_LORA, RW_ICLR_LORA, LANES),
            ("ml_if", RW_IN + 2 * ML_QK + 2 * ML_W, 2 * ML_H, LANES)]
_EV_N = 7168
_OD_SEGS = [("hg_q", 0, HG_W, HG_W), ("hg_f", HG_W, HG_W, HG_W), ("hg_i", 2 * HG_W, HG_W, HG_W),
            ("hg_g", 3 * HG_W, HG_W, HG_W),
            ("mb_z", HG_IN, MB_W, MB_W), ("mb_x", HG_IN + MB_W, MB_W, MB_W),
            ("mb_b", HG_IN + 2 * MB_W, MB_G * MB_N, MB_G * MB_N),
            ("mb_c", HG_IN + 2 * MB_W + MB_G * MB_N, MB_G * MB_N, MB_G * MB_N),
            ("mb_dt", HG_IN + MB_W + MB_CONV_W, MB_H, LANES)]
_OD_N = 7680


def _row(v, width=None):
    v = v.reshape(1, -1).astype(F32)
    if width is not None and v.shape[1] < width:
        v = jnp.pad(v, ((0, 0), (0, width - v.shape[1])))
    return v


def _pad_rows(w, rows):
    return jnp.pad(w, ((0, rows - w.shape[0]), (0, 0)))


def _even_layer(x, B, T, p):
    offs, pack = _layout(_EV_SEGS, _EV_N)
    s_if = RW_IN + 2 * ML_QK + 2 * ML_W
    u, g_row = _norm_matmul(x, _row(p["norm1"]), pack(p["w_in"]), _bf(p["w_in"][:, s_if:s_if + 2 * ML_H].T))
    mu = p["rw_mu"]
    o_dw, o_da, o_dg = 3 * RW_W, 3 * RW_W + RW_DECAY_LORA, 3 * RW_W + RW_DECAY_LORA + RW_ICLR_LORA
    rw_prm = [_row(mu[0:RW_W]), _row(mu[RW_W:2 * RW_W]), _row(mu[2 * RW_W:3 * RW_W]),
              _row(mu[o_dw:o_da], LANES), _row(mu[o_da:o_dg], LANES), _row(mu[o_dg:]),
              _row(p["rw_w0"]), _bf(_pad_rows(p["rw_w2"], LANES)),
              _row(p["rw_a0"]), _bf(_pad_rows(p["rw_a2"], LANES)), _bf(p["rw_g2"]),
              _row(p["rw_k_k"]), _row(p["rw_k_a"]), _row(p["rw_r_k"]), _row(p["rw_ln_w"]), _row(p["rw_ln_b"])]
    y_a = _rwkv(u, offs, B, T, rw_prm)
    gb =jnp.concatenate([p["ml_i_b"], p["ml_f_b"]]).astype(F32)
    ml_prm = [p["ml_conv_w"].astype(F32), _row(p["ml_conv_b"]), _row(gb, LANES), gb.reshape(-1, 1),
              _row(p["ml_norm_w"])]
    y_b = _mlstm(u, g_row, offs, B, T, ml_prm)
    x = _matmul_res(y_a, y_b, _bf(p["w_out"]), x)
    return _ffn(x, _row(p["norm2"]), _bf(p["ffn_w_gate"]), _bf(p["ffn_w_up"]), _bf(p["ffn_w_down"]))


def _odd_layer(x, B, T, p, final_w):
    offs, pack = _layout(_OD_SEGS, _OD_N)
    s_dt = HG_IN + MB_W + MB_CONV_W
    u, dt_row = _norm_matmul(x, _row(p["norm1"]), pack(p["w_in"]), _bf(p["w_in"][:, s_dt:s_dt + MB_H].T))
    hg_prm = [p["hg_lb_logits"].astype(F32), _hgrn_level_matrices(), _hgrn_masks(), _row(p["hg_norm_w"])]
    y_c = _hgrn(u, offs, B, T, hg_prm)
    cw, cb = p["mb_conv_w"].astype(F32), p["mb_conv_b"].astype(F32)
    GN = MB_G * MB_N
    expand = (jnp.arange(LANES)[:, None] == (jnp.arange(MB_W)[None, :] // MB_HEAD)).astype(BF16)
    mb_prm = [cw[:, :MB_W], _row(cb[:MB_W]), cw[:, MB_W:MB_W + GN], _row(cb[MB_W:MB_W + GN]),
              cw[:, MB_W + GN:], _row(cb[MB_W + GN:]),
              _row(p["mb_dt_bias"], LANES), p["mb_dt_bias"].astype(F32).reshape(-1, 1),
              _row(p["mb_A_log"], LANES), p["mb_A_log"].astype(F32).reshape(-1, 1),
              _row(jnp.repeat(p["mb_D"], MB_HEAD)), _row(p["mb_norm_w"]), expand]
    y_d = _mamba(u, dt_row, offs, B, T, mb_prm)
    x = _matmul_res(y_c, y_d, _bf(p["w_out"]), x)
    wr = jnp.pad(p["moe_router"].astype(F32), ((0, 0), (0, LANES - N_EXPERTS)))
    idx, prob, h = _router(x, _row(p["norm2"]), wr)
    pos, src_tok, tile_expert, tile_active = _route_plan(idx[:, :2], TM_MOE)
    h = jnp.pad(h, ((0, src_tok.shape[0] - h.shape[0]), (0, 0)))
    xg = jnp.take(h, src_tok, axis=0, mode="clip")
    act = _moe_up(xg, tile_expert, tile_active, _bf(p["moe_w_gate"]), _bf(p["moe_w_up"]))
    yg = _moe_down(act, tile_expert, tile_active, p["moe_w_down"].astype(F32))
    y0 = jnp.take(yg, pos[:, 0], axis=0, mode="clip")
    y1 = jnp.take(yg, pos[:, 1], axis=0, mode="clip")
    return _combine(x, y0, y1, prob, _row(final_w))


def kernel(x, final_norm_w, hg_lb_logits, ev_norm1_w, ev_w_in, ev_w_out, rw_mu, rw_w0, rw_w2, rw_a0, rw_a2, rw_g2, rw_k_k, rw_k_a, rw_r_k, rw_ln_w, rw_ln_b, ml_conv_w, ml_conv_b, ml_i_b, ml_f_b, ml_norm_w, ev_norm2_w, ffn_w_gate, ffn_w_up, ffn_w_down, od_norm1_w, od_w_in, od_w_out, hg_norm_w, mb_conv_w, mb_conv_b, mb_dt_bias, mb_A_log, mb_D, mb_norm_w, od_norm2_w, moe_router, moe_w_gate, moe_w_up, moe_w_down):
    B, T, D = x.shape
    xf = x.reshape(B * T, D)
    ev = dict(norm1=ev_norm1_w[0], w_in=ev_w_in[0], w_out=ev_w_out[0], rw_mu=rw_mu[0], rw_w0=rw_w0[0],
              rw_w2=rw_w2[0], rw_a0=rw_a0[0], rw_a2=rw_a2[0], rw_g2=rw_g2[0], rw_k_k=rw_k_k[0],
              rw_k_a=rw_k_a[0], rw_r_k=rw_r_k[0], rw_ln_w=rw_ln_w[0], rw_ln_b=rw_ln_b[0],
              ml_conv_w=ml_conv_w[0], ml_conv_b=ml_conv_b[0], ml_i_b=ml_i_b[0], ml_f_b=ml_f_b[0],
              ml_norm_w=ml_norm_w[0], norm2=ev_norm2_w[0], ffn_w_gate=ffn_w_gate[0], ffn_w_up=ffn_w_up[0],
              ffn_w_down=ffn_w_down[0])
    od = dict(norm1=od_norm1_w[0], w_in=od_w_in[0], w_out=od_w_out[0], hg_lb_logits=hg_lb_logits,
              hg_norm_w=hg_norm_w[0], mb_conv_w=mb_conv_w[0], mb_conv_b=mb_conv_b[0], mb_dt_bias=mb_dt_bias[0],
              mb_A_log=mb_A_log[0], mb_D=mb_D[0], mb_norm_w=mb_norm_w[0], norm2=od_norm2_w[0],
              moe_router=moe_router[0], moe_w_gate=moe_w_gate[0], moe_w_up=moe_w_up[0],
              moe_w_down=moe_w_down[0])
    xf = _even_layer(xf, B, T, ev)
    xf = _odd_layer(xf, B, T, od, final_norm_w)
    return xf.reshape(B, T, D)
```

```python
import functools
import math

import jax
import jax.numpy as jnp
from jax import lax
from jax.experimental import pallas as pl
from jax.experimental.pallas import tpu as pltpu

F32 = jnp.float32
BF16 = jnp.bfloat16

D_MODEL = 2048
NORM_EPS = 1e-6
RW_HEAD = 64
RW_W = 1024
RW_H = RW_W // RW_HEAD
RW_DECAY_LORA = 96
RW_ICLR_LORA = 96
RW_GATE_LORA = 256
RW_IN = 3 * RW_W + RW_DECAY_LORA + RW_ICLR_LORA + RW_GATE_LORA
RW_LN_EPS = 64e-5
RW_DECAY_SCALE = math.exp(-0.5)
ML_W = 1024
ML_H = 4
ML_DV = ML_W // ML_H
ML_DK = ML_DV // 2
ML_QK = ML_H * ML_DK
ML_IN = 2 * ML_QK + 2 * ML_W + 2 * ML_H
HG_W = 1024
HG_HEAD = 128
HG_H = HG_W // HG_HEAD
HG_IN = 4 * HG_W
MB_W = 1024
MB_HEAD = 64
MB_H = MB_W // MB_HEAD
MB_G = 4
MB_N = 128
MB_CONV_W = MB_W + 2 * MB_G * MB_N
N_EXPERTS = 8

LANES = 128
SUBLANES = 8
VMEM_LIMIT = 56 * 1024 * 1024

TM_PROJ = 1024
TN_PROJ = 512
TM_FFN = 512
TF_FFN = 512
TM_MOE = 256
TF_MOE = 1408
TN_MOE = 1024
RW_L = 64
RW_BLK = 16
ML_L = 128
HG_L = 64
MB_L = 128
CONV_K = 4


def _cparams(sem):
    return pltpu.CompilerParams(dimension_semantics=sem, vmem_limit_bytes=VMEM_LIMIT)


def _dot(a, b):
    return jnp.dot(a, b, preferred_element_type=F32)


def _dot_nt(a, b):
    return lax.dot_general(a, b, (((1,), (1,)), ((), ())), preferred_element_type=F32)


def _dot_tn(a, b):
    return lax.dot_general(a, b, (((0,), (0,)), ((), ())), preferred_element_type=F32)


def _bf(x):
    return x.astype(BF16)


def _split2(x):
    h = x.astype(BF16)
    l = (x - h.astype(F32)).astype(BF16)
    return h, l


def _split3(x):
    h = x.astype(BF16)
    r = x - h.astype(F32)
    m = r.astype(BF16)
    l = (r - m.astype(F32)).astype(BF16)
    return h, m, l


def _dot_sel(sel_bf16, x):
    h, m, l = _split3(x)
    return _dot(sel_bf16, h) + _dot(sel_bf16, m) + _dot(sel_bf16, l)


def _dot_sel_r(x, sel_bf16):
    h, m, l = _split3(x)
    return _dot(h, sel_bf16) + _dot(m, sel_bf16) + _dot(l, sel_bf16)


def _dot3(a, b):
    ah, al = _split2(a)
    bh, bl = _split2(b)
    return _dot(ah, bh) + _dot(ah, bl) + _dot(al, bh)


def _sigmoid(x):
    return 1.0 / (1.0 + jnp.exp(-x))


def _silu(x):
    return x * _sigmoid(x)


def _log_sigmoid(x):
    return -(jnp.maximum(-x, 0.0) + jnp.log1p(jnp.exp(-jnp.abs(x))))


def _softplus(x):
    return jnp.maximum(x, 0.0) + jnp.log1p(jnp.exp(-jnp.abs(x)))


def _iota(shape, dim):
    return lax.broadcasted_iota(jnp.int32, shape, dim)


def _idiv(x, d):
    sh = d.bit_length() - 1
    assert d == 1 << sh
    return lax.shift_right_logical(x, jnp.int32(sh))


def _shift_rows(tail, x, j):
    xc = jnp.concatenate([tail, x], axis=0)
    return pltpu.roll(xc, j, 0)[SUBLANES:]


def _norm_matmul_kernel(x_ref, nw_ref, w_ref, wt_ref, o_ref, ot_ref, h_ref):
    @pl.when(pl.program_id(1) == 0)
    def _():
        x = x_ref[...]
        ms = jnp.mean(x * x, axis=-1, keepdims=True)
        h_ref[...] = _bf(x * lax.rsqrt(ms + NORM_EPS) * nw_ref[...])
        ot_ref[...] = _dot_nt(wt_ref[...], h_ref[...])

    o_ref[...] = _dot(h_ref[...], w_ref[...])


def _norm_matmul(x, nw, w, wt):
    m, d = x.shape
    n = w.shape[1]
    r = wt.shape[0]
    return pl.pallas_call(
        _norm_matmul_kernel,
        grid=(m // TM_PROJ, n // TN_PROJ),
        in_specs=[pl.BlockSpec((TM_PROJ, d), lambda i, j: (i, 0)),
                  pl.BlockSpec((1, d), lambda i, j: (0, 0)),
                  pl.BlockSpec((d, TN_PROJ), lambda i, j: (0, j)),
                  pl.BlockSpec((r, d), lambda i, j: (0, 0))],
        out_specs=[pl.BlockSpec((TM_PROJ, TN_PROJ), lambda i, j: (i, j)),
                   pl.BlockSpec((r, TM_PROJ), lambda i, j: (0, i))],
        out_shape=[jax.ShapeDtypeStruct((m, n), F32),
                   jax.ShapeDtypeStruct((r, m), F32)],
        scratch_shapes=[pltpu.VMEM((TM_PROJ, d), BF16)],
        compiler_params=_cparams(("parallel", "arbitrary")),
        name="norm_matmul",
    )(x, nw, w, wt)


def _matmul_res_kernel(ya_ref, yb_ref, wa_ref, wb_ref, r_ref, o_ref):
    o_ref[...] = r_ref[...] + _dot(ya_ref[...], wa_ref[...]) + _dot(yb_ref[...], wb_ref[...])


def _matmul_res(ya, yb, w, res):
    m, k = ya.shape
    n = w.shape[1]
    return pl.pallas_call(
        _matmul_res_kernel,
        grid=(m // TM_PROJ, n // TN_PROJ),
        in_specs=[pl.BlockSpec((TM_PROJ, k), lambda i, j: (i, 0)),
                  pl.BlockSpec((TM_PROJ, k), lambda i, j: (i, 0)),
                  pl.BlockSpec((k, TN_PROJ), lambda i, j: (0, j)),
                  pl.BlockSpec((k, TN_PROJ), lambda i, j: (1, j)),
                  pl.BlockSpec((TM_PROJ, TN_PROJ), lambda i, j: (i, j))],
        out_specs=pl.BlockSpec((TM_PROJ, TN_PROJ), lambda i, j: (i, j)),
        out_shape=jax.ShapeDtypeStruct((m, n), F32),
        compiler_params=_cparams(("parallel", "arbitrary")),
        name="matmul_res",
    )(ya, yb, w, w, res)


def _ffn_kernel(x_ref, nw_ref, wg_ref, wu_ref, wd_ref, o_ref, h_ref):
    f = pl.program_id(1)

    @pl.when(f == 0)
    def _():
        x = x_ref[...]
        ms = jnp.mean(x * x, axis=-1, keepdims=True)
        h_ref[...] = _bf(x * lax.rsqrt(ms + NORM_EPS) * nw_ref[...])
        o_ref[...] = x

    h = h_ref[...]
    act = _silu(_dot(h, wg_ref[...])) * _dot(h, wu_ref[...])
    o_ref[...] += _dot(_bf(act), wd_ref[...])


def _ffn(x, nw, wg, wu, wd):
    m, d = x.shape
    f = wg.shape[1]
    return pl.pallas_call(
        _ffn_kernel,
        grid=(m // TM_FFN, f // TF_FFN),
        in_specs=[pl.BlockSpec((TM_FFN, d), lambda i, j: (i, 0)),
                  pl.BlockSpec((1, d), lambda i, j: (0, 0)),
                  pl.BlockSpec((d, TF_FFN), lambda i, j: (0, j)),
                  pl.BlockSpec((d, TF_FFN), lambda i, j: (0, j)),
                  pl.BlockSpec((TF_FFN, d), lambda i, j: (j, 0))],
        out_specs=pl.BlockSpec((TM_FFN, d), lambda i, j: (i, 0)),
        out_shape=jax.ShapeDtypeStruct((m, d), F32),
        scratch_shapes=[pltpu.VMEM((TM_FFN, d), BF16)],
        compiler_params=_cparams(("parallel", "arbitrary")),
        name="ffn_swiglu",
    )(x, nw, wg, wu, wd)


def _router_kernel(x_ref, nw_ref, wr_ref, i_ref, p_ref, h_ref):
    x = x_ref[...]
    ms = jnp.mean(x * x, axis=-1, keepdims=True)
    h = x * lax.rsqrt(ms + NORM_EPS) * nw_ref[...]
    h_ref[...] = _bf(h)
    wr = wr_ref[...]
    hh, hl = _split2(h)
    wh, wl = _split2(wr)
    logits = _dot(hh, wh) + _dot(hh, wl) + _dot(hl, wh)
    lane = _iota(logits.shape, 1)
    neg = jnp.float32(-jnp.inf)
    logits = jnp.where(lane < N_EXPERTS, logits, neg)
    v1 = jnp.max(logits, axis=-1, keepdims=True)
    i1 = jnp.min(jnp.where(logits == v1, lane, LANES), axis=-1, keepdims=True)
    rest = jnp.where(lane == i1, neg, logits)
    v2 = jnp.max(rest, axis=-1, keepdims=True)
    i2 = jnp.min(jnp.where(rest == v2, lane, LANES), axis=-1, keepdims=True)
    e2 = jnp.exp(v2 - v1)
    p1 = 1.0 / (1.0 + e2)
    p2 = e2 / (1.0 + e2)
    i_ref[...] = jnp.where(lane == 0, i1, jnp.where(lane == 1, i2, 0))
    p_ref[...] = jnp.where(lane == 0, p1, jnp.where(lane == 1, p2, 0.0))


def _router(x, nw, wr):
    m, d = x.shape
    return pl.pallas_call(
        _router_kernel,
        grid=(m // TM_FFN,),
        in_specs=[pl.BlockSpec((TM_FFN, d), lambda i: (i, 0)),
                  pl.BlockSpec((1, d), lambda i: (0, 0)),
                  pl.BlockSpec((d, LANES), lambda i: (0, 0))],
        out_specs=[pl.BlockSpec((TM_FFN, LANES), lambda i: (i, 0)),
                   pl.BlockSpec((TM_FFN, LANES), lambda i: (i, 0)),
                   pl.BlockSpec((TM_FFN, d), lambda i: (i, 0))],
        out_shape=[jax.ShapeDtypeStruct((m, LANES), jnp.int32),
                   jax.ShapeDtypeStruct((m, LANES), F32),
                   jax.ShapeDtypeStruct((m, d), BF16)],
        compiler_params=_cparams(("parallel",)),
        name="moe_router",
    )(x, nw, wr)


def _route_plan(top_idx, tm):
    m = top_idx.shape[0]
    n_rows = 2 * m + N_EXPERTS * tm
    e_flat = top_idx.reshape(-1)
    onehot = (e_flat[:, None] == jnp.arange(N_EXPERTS, dtype=jnp.int32)[None, :]).astype(jnp.int32)
    rank = jnp.cumsum(onehot, axis=0) - onehot
    counts = jnp.sum(onehot, axis=0)
    padded = ((counts + tm - 1) // tm) * tm
    ends = jnp.cumsum(padded)
    off = ends - padded
    pos = jnp.sum(onehot * (off[None, :] + rank), axis=1)
    src_tok = jnp.zeros((n_rows,), jnp.int32).at[pos].set(jnp.arange(2 * m, dtype=jnp.int32) // 2)
    tile_start = jnp.arange(n_rows // tm, dtype=jnp.int32) * tm
    tile_expert = jnp.minimum(jnp.sum((tile_start[:, None] >= ends[None, :]).astype(jnp.int32), axis=1),
                              N_EXPERTS - 1)
    tile_active = (tile_start < ends[-1]).astype(jnp.int32)
    return pos.reshape(m, 2), src_tok, tile_expert, tile_active


def _new_weight_block(te_ref, i):
    return (i == 0) | (te_ref[i] != te_ref[jnp.maximum(i - 1, 0)])


def _moe_up_kernel(te_ref, ta_ref, x_ref, wg_ref, wu_ref, a_ref, wgb_ref, wub_ref):
    i = pl.program_id(1)

    @pl.when(_new_weight_block(te_ref, i))
    def _():
        wgb_ref[...] = _bf(wg_ref[0])
        wub_ref[...] = _bf(wu_ref[0])

    @pl.when(ta_ref[i] == 1)
    def _():
        x = x_ref[...]
        act = _silu(_dot(x, wgb_ref[...])) * _dot(x, wub_ref[...])
        a_ref[...] = _bf(act)

    @pl.when(ta_ref[i] == 0)
    def _():
        a_ref[...] = jnp.zeros_like(a_ref)


def _moe_up(xg, tile_expert, tile_active, wg, wu):
    n_rows, d = xg.shape
    fe = wg.shape[2]
    wspec = pl.BlockSpec((1, d, TF_MOE), lambda f, i, te, ta: (te[i], 0, f), pipeline_mode=pl.Buffered(1))
    grid_spec = pltpu.PrefetchScalarGridSpec(
        num_scalar_prefetch=2,
        grid=(fe // TF_MOE, n_rows // TM_MOE),
        in_specs=[pl.BlockSpec((TM_MOE, d), lambda f, i, te, ta: (i, 0)), wspec, wspec],
        out_specs=pl.BlockSpec((TM_MOE, TF_MOE), lambda f, i, te, ta: (i, f)),
        scratch_shapes=[pltpu.VMEM((d, TF_MOE), BF16), pltpu.VMEM((d, TF_MOE), BF16)],
    )
    return pl.pallas_call(
        _moe_up_kernel,
        grid_spec=grid_spec,
        out_shape=jax.ShapeDtypeStruct((n_rows, fe), BF16),
        compiler_params=_cparams(("arbitrary", "arbitrary")),
        name="moe_up",
    )(tile_expert, tile_active, xg, wg, wu)


def _moe_down_kernel(te_ref, ta_ref, a_ref, wd_ref, y_ref, wdb_ref):
    i = pl.program_id(1)

    @pl.when(_new_weight_block(te_ref, i))
    def _():
        wdb_ref[...] = _bf(wd_ref[0])

    @pl.when(ta_ref[i] == 1)
    def _():
        y_ref[...] = _bf(_dot(a_ref[...], wdb_ref[...]))

    @pl.when(ta_ref[i] == 0)
    def _():
        y_ref[...] = jnp.zeros_like(y_ref)


def _moe_down(act, tile_expert, tile_active, wd):
    n_rows, fe = act.shape
    d = wd.shape[2]
    grid_spec = pltpu.PrefetchScalarGridSpec(
        num_scalar_prefetch=2,
        grid=(d // TN_MOE, n_rows // TM_MOE),
        in_specs=[pl.BlockSpec((TM_MOE, fe), lambda n, i, te, ta: (i, 0)),
                  pl.BlockSpec((1, fe, TN_MOE), lambda n, i, te, ta: (te[i], 0, n))],
        out_specs=pl.BlockSpec((TM_MOE, TN_MOE), lambda n, i, te, ta: (i, n)),
        scratch_shapes=[pltpu.VMEM((fe, TN_MOE), BF16)],
    )
    return pl.pallas_call(
        _moe_down_kernel,
        grid_spec=grid_spec,
        out_shape=jax.ShapeDtypeStruct((n_rows, d), BF16),
        compiler_params=_cparams(("arbitrary", "arbitrary")),
        name="moe_down",
    )(tile_expert, tile_active, act, wd)


def _combine_kernel(x_ref, y0_ref, y1_ref, p_ref, fw_ref, o_ref):
    p = p_ref[...]
    y = x_ref[...] + p[:, 0:1] * y0_ref[...].astype(F32) + p[:, 1:2] * y1_ref[...].astype(F32)
    ms = jnp.mean(y * y, axis=-1, keepdims=True)
    o_ref[...] = y * lax.rsqrt(ms + NORM_EPS) * fw_ref[...]


def _combine(x, y0, y1, prob, final_w):
    m, d = x.shape
    row = pl.BlockSpec((TM_FFN, d), lambda i: (i, 0))
    return pl.pallas_call(
        _combine_kernel,
        grid=(m // TM_FFN,),
        in_specs=[row, row, row, pl.BlockSpec((TM_FFN, LANES), lambda i: (i, 0)),
                  pl.BlockSpec((1, d), lambda i: (0, 0))],
        out_specs=row,
        out_shape=jax.ShapeDtypeStruct((m, d), F32),
        compiler_params=_cparams(("parallel",)),
        name="moe_combine",
    )(x, y0, y1, prob, final_w)


def _rwkv_kernel(r_ref, k_ref, v_ref, dw_ref, da_ref, dg_ref,
                 mur_ref, muk_ref, muv_ref, mudw_ref, muda_ref, mudg_ref,
                 w0_ref, w2_ref, a0_ref, a2_ref, g2_ref, kk_ref, ka_ref, rk_ref, lnw_ref, lnb_ref,
                 o_ref,
                 H_ref, tr_ref, tk_ref, tv_ref, tdw_ref, tda_ref, tdg_ref):
    L = RW_L
    L2 = 2 * L

    @pl.when(pl.program_id(1) == 0)
    def _():
        H_ref[...] = jnp.zeros_like(H_ref)
        for t in (tr_ref, tk_ref, tv_ref, tdw_ref, tda_ref, tdg_ref):
            t[...] = jnp.zeros_like(t)

    def shift_mix(ref, tail, mu):
        x = ref[...]
        prev = _shift_rows(tail[...], x, 1)
        tail[...] = x[L - SUBLANES:]
        return x + (prev - x) * mu[...]

    r = shift_mix(r_ref, tr_ref, mur_ref)
    k = shift_mix(k_ref, tk_ref, muk_ref)
    v = shift_mix(v_ref, tv_ref, muv_ref)
    dw = shift_mix(dw_ref, tdw_ref, mudw_ref)
    da = shift_mix(da_ref, tda_ref, muda_ref)
    dg = shift_mix(dg_ref, tdg_ref, mudg_ref)

    log_w = -RW_DECAY_SCALE * _sigmoid(w0_ref[...] + _dot(_bf(jnp.tanh(dw)), w2_ref[...]))
    a = _sigmoid(a0_ref[...] + _dot(_bf(da), a2_ref[...]))
    g = _dot(_bf(_sigmoid(dg)), g2_ref[...])

    ri = _iota((LANES, LANES), 0)
    ci = _iota((LANES, LANES), 1)
    hsum = jnp.where(_idiv(ri, RW_HEAD) == _idiv(ci, RW_HEAD), 1.0, 0.0).astype(BF16)
    same_head = _idiv(ri, L) == _idiv(ci, L)
    tril_s = jnp.where(same_head & (ri > ci), 1.0, 0.0)
    tril_i = jnp.where(same_head & (ri >= ci), 1.0, 0.0)
    blk_d = jnp.where(_idiv(ri, RW_BLK) == _idiv(ci, RW_BLK), 1.0, 0.0)
    eye = jnp.where(ri == ci, 1.0, 0.0)
    cum = jnp.where(_iota((L, L), 0) >= _iota((L, L), 1), 1.0, 0.0).astype(BF16)
    lane = _iota((1, LANES), 1)
    m0 = lane < RW_HEAD

    def head_sum(x):
        h, l = _split2(x)
        return _dot(h, hsum) + _dot(l, hsum)

    kk = k * kk_ref[...]
    k2 = k * (1.0 + (a - 1.0) * ka_ref[...])
    cw = _dot_sel(cum, log_w)
    e_prev = jnp.exp(cw - log_w)
    e_inv = jnp.exp(-cw)
    e_cw = jnp.exp(cw)
    wl = cw[L - 1:L]
    e_end = jnp.exp(wl - cw)
    e_wl = jnp.exp(wl)
    rk2 = r * k2 * rk_ref[...]

    def stack(x):
        x0 = jnp.where(m0, x, 0.0)
        return jnp.concatenate([x0, x - x0], axis=0)

    def mm3(xs, ys):
        return _dot(xs[0], ys[0]) + _dot(xs[0], ys[1]) + _dot(xs[1], ys[0])

    NP = RW_H // 2
    sls = [slice(p * LANES, (p + 1) * LANES) for p in range(NP)]

    def each(fn, *lists):
        return [fn(*args) for args in zip(*lists)]

    kk_l = [kk[:, sl] for sl in sls]
    kk_l = each(lambda x: x * lax.rsqrt(jnp.maximum(head_sum(x * x), 1e-12)), kk_l)
    kka_l = [x * a[:, sl] for x, sl in zip(kk_l, sls)]
    At = [_bf(stack(-x * e_prev[:, sl])) for x, sl in zip(kk_l, sls)]
    Bt = [_bf(stack(x * e_inv[:, sl])) for x, sl in zip(kka_l, sls)]
    Kt = [_bf(stack(k2[:, sl] * e_inv[:, sl])) for sl in sls]
    Rt_f = [stack(r[:, sl] * e_cw[:, sl]) for sl in sls]
    Rt = each(_bf, Rt_f)
    Bh = [_bf(stack(x * e_end[:, sl])) for x, sl in zip(kka_l, sls)]
    Kh = [_bf(stack(k2[:, sl] * e_end[:, sl])) for sl in sls]
    Vs = [_bf(stack(v[:, sl])) for sl in sls]

    Mab = each(lambda x, y: _dot_nt(x, y) * tril_s, At, Bt)
    Mak = each(lambda x, y: _bf(_dot_nt(x, y) * tril_s), At, Kt)
    Arb = each(lambda x, y: _bf(_dot_nt(x, y) * tril_i), Rt, Bt)
    Ark = each(lambda x, y: _bf(_dot_nt(x, y) * tril_i), Rt, Kt)

    MD = each(lambda m: m * blk_d, Mab)
    Nn = each(lambda m, d: _split2(m - d), Mab, MD)
    P = each(lambda d: eye + d, MD)
    MDs = each(_split2, MD)
    S = each(mm3, MDs, MDs)
    for it in range(3):
        Ss = each(_split2, S)
        P = each(lambda p_, s_: p_ + mm3(_split2(p_), s_), P, Ss)
        if it < 2:
            S = each(mm3, Ss, Ss)
    Ps = each(_split2, P)
    X = each(mm3, Ps, Nn)
    Xs = each(_split2, X)
    X2 = each(lambda x: _split2(mm3(x, x)), Xs)
    Y = each(lambda x: eye + x, X)
    Y = each(lambda y_, x2: y_ + mm3(_split2(y_), x2), Y, X2)
    T = each(lambda y_, p_: _split2(mm3(_split2(y_), p_)), Y, Ps)

    P1 = each(lambda t, x: _bf(_dot(t[0], x) + _dot(t[1], x)), T, At)
    MV = each(lambda m, x: _split2(_dot(m, x)), Mak, Vs)
    P2 = each(lambda t, x: _bf(mm3(t, x)), T, MV)
    G = [eye * e_wl[:, sl] + _dot_tn(b, p1) for sl, b, p1 in zip(sls, Bh, P1)]
    J = each(lambda b, p2, kh, vs: _dot_tn(b, p2) + _dot_tn(kh, vs), Bh, P2, Kh, Vs)
    Q = each(lambda rf, ar, p1: rf + _dot(ar, p1), Rt_f, Arb, P1)
    Z = each(lambda ar, p2, ak, vs: _dot(ar, p2) + _dot(ak, vs), Arb, P2, Ark, Vs)

    Hs = [_split2(H_ref[p]) for p in range(NP)]
    Yst = each(lambda q, h, z: mm3(_split2(q), h) + z, Q, Hs, Z)
    Hn = each(lambda g_, h, j: mm3(_split2(g_), h) + j, G, Hs, J)
    for p in range(NP):
        H_ref[p] = Hn[p]
    y_l = each(lambda y_: y_[:L] + y_[L:], Yst)

    mean = each(lambda y_: head_sum(y_) * (1.0 / RW_HEAD), y_l)
    yc = each(lambda y_, m: y_ - m, y_l, mean)
    var = each(lambda c: head_sum(c * c) * (1.0 / RW_HEAD), yc)
    yn = [c * lax.rsqrt(vr + RW_LN_EPS) * lnw_ref[:, sl] + lnb_ref[:, sl] for c, vr, sl in zip(yc, var, sls)]
    bonus = [head_sum(rk2[:, sl]) * v[:, sl] for sl in sls]
    for p in range(NP):
        o_ref[:, sls[p]] = _bf((yn[p] + bonus[p]) * g[:, sls[p]])


def _rwkv(u, offs, B, T, prm):
    L = RW_L
    nc = T // L

    def col(off, w):
        return pl.BlockSpec((L, w), lambda b, c: (b * nc + c, off // w))

    def par(w, rows=1):
        return pl.BlockSpec((rows, w), lambda b, c: (0, 0))

    in_specs = [col(offs["rw_r"], RW_W), col(offs["rw_k"], RW_W), col(offs["rw_v"], RW_W),
                col(offs["rw_dw"], LANES), col(offs["rw_da"], LANES), col(offs["rw_dg"], RW_GATE_LORA),
                par(RW_W), par(RW_W), par(RW_W), par(LANES), par(LANES), par(RW_GATE_LORA),
                par(RW_W), par(RW_W, LANES), par(RW_W), par(RW_W, LANES), par(RW_W, RW_GATE_LORA),
                par(RW_W), par(RW_W), par(RW_W), par(RW_W), par(RW_W)]
    return pl.pallas_call(
        _rwkv_kernel,
        grid=(B, nc),
        in_specs=in_specs,
        out_specs=pl.BlockSpec((L, RW_W), lambda b, c: (b * nc + c, 0)),
        out_shape=jax.ShapeDtypeStruct((B * T, RW_W), BF16),
        scratch_shapes=[pltpu.VMEM((RW_H // 2, LANES, LANES), F32),
                        pltpu.VMEM((SUBLANES, RW_W), F32), pltpu.VMEM((SUBLANES, RW_W), F32),
                        pltpu.VMEM((SUBLANES, RW_W), F32), pltpu.VMEM((SUBLANES, LANES), F32),
                        pltpu.VMEM((SUBLANES, LANES), F32), pltpu.VMEM((SUBLANES, RW_GATE_LORA), F32)],
        compiler_params=_cparams(("parallel", "arbitrary")),
        name="rwkv7",
    )(u, u, u, u, u, u, *prm)


def _mlstm_kernel(qk_ref, v_ref, o_ref, gc_ref, gr_ref, cw_ref, cb_ref, gbr_ref, gbc_ref, nw_ref,
                  y_ref, C_ref, m_ref, tail_ref):
    L = ML_L
    DK, DV = ML_DK, ML_DV

    @pl.when(pl.program_id(1) == 0)
    def _():
        C_ref[...] = jnp.zeros_like(C_ref)
        m_ref[...] = jnp.zeros_like(m_ref)
        tail_ref[...] = jnp.zeros_like(tail_ref)

    x = qk_ref[...]
    tail = tail_ref[...]
    acc = x * cw_ref[CONV_K - 1:CONV_K] + cb_ref[...]
    for j in range(1, CONV_K):
        acc = acc + _shift_rows(tail, x, j) * cw_ref[CONV_K - 1 - j:CONV_K - j]
    tail_ref[...] = x[L - SUBLANES:]
    qk = _silu(acc)

    gc = gc_ref[...] + gbr_ref[...]
    gr = gr_ref[...] + gbc_ref[...]
    fl_c = _log_sigmoid(gc)
    fl_r = _log_sigmoid(gr)
    ri = _iota((L, L), 0)
    ci = _iota((L, L), 1)
    causal = ri >= ci
    tril = jnp.where(causal, 1.0, 0.0).astype(BF16)
    triu = jnp.where(ri <= ci, 1.0, 0.0).astype(BF16)
    b_c = _dot_sel(tril, fl_c)
    b_r = _dot_sel_r(fl_r, triu)
    one_col = jnp.where(_iota((L, LANES), 1) == 0, 1.0, 0.0)
    neg = jnp.float32(-jnp.inf)
    vv = v_ref[...]
    oo = o_ref[...]

    for h in range(ML_H):
        q_h = _bf(qk[:, h * DK:(h + 1) * DK] * (DK ** -0.5))
        k_f = qk[:, ML_QK + h * DK:ML_QK + (h + 1) * DK]
        k_h = _bf(k_f)
        v_ext = _bf(jnp.concatenate([vv[:, h * DV:(h + 1) * DV], one_col], axis=1))
        bc = b_c[:, ML_H + h:ML_H + h + 1]
        br = b_r[ML_H + h:ML_H + h + 1, :]
        il_c = gc[:, h:h + 1]
        il_r = gr[h:h + 1, :]
        m_prev = m_ref[h:h + 1, 0:1]
        C_prev = C_ref[h]

        D = jnp.where(causal, bc - br + il_r, neg)
        m_t = jnp.maximum(bc + m_prev, jnp.max(D, axis=-1, keepdims=True))
        S = _dot_nt(q_h, k_h) * jnp.exp(D - m_t)
        inter = jnp.exp(bc + m_prev - m_t)
        num = inter * _dot(q_h, _bf(C_prev)) + _dot(_bf(S), v_ext)
        den = num[:, DV:DV + 1]
        hh = num[:, :DV] / jnp.maximum(jnp.abs(den), jnp.exp(-m_t))
        ms = jnp.mean(hh * hh, axis=-1, keepdims=True)
        hn = hh * lax.rsqrt(ms + NORM_EPS) * nw_ref[:, h * DV:(h + 1) * DV]
        y_ref[:, h * DV:(h + 1) * DV] = _bf(hn * _sigmoid(oo[:, h * DV:(h + 1) * DV]))

        g_end = bc[L - 1:L]
        wst = g_end - bc + il_c
        m_new = jnp.maximum(g_end + m_prev, jnp.max(wst, axis=0, keepdims=True))
        kt = _bf(k_f * jnp.exp(wst - m_new))
        C_ref[h] = jnp.exp(g_end + m_prev - m_new) * C_prev + _dot_tn(kt, v_ext)
        m_ref[h:h + 1, :] = jnp.broadcast_to(m_new, (1, LANES))


def _mlstm(u, g_row, offs, B, T, prm):
    L = ML_L
    nc = T // L

    def col(off, w):
        return pl.BlockSpec((L, w), lambda b, c: (b * nc + c, off // w))

    def par(r, w):
        return pl.BlockSpec((r, w), lambda b, c: (0, 0))

    in_specs = [col(offs["ml_qk"], 2 * ML_QK), col(offs["ml_v"], ML_W), col(offs["ml_o"], ML_W),
                col(offs["ml_if"], LANES),
                pl.BlockSpec((SUBLANES, L), lambda b, c: (0, b * nc + c)),
                par(CONV_K, 2 * ML_QK), par(1, 2 * ML_QK), par(1, LANES), par(SUBLANES, 1), par(1, ML_W)]
    return pl.pallas_call(
        _mlstm_kernel,
        grid=(B, nc),
        in_specs=in_specs,
        out_specs=pl.BlockSpec((L, ML_W), lambda b, c: (b * nc + c, 0)),
        out_shape=jax.ShapeDtypeStruct((B * T, ML_W), BF16),
        scratch_shapes=[pltpu.VMEM((ML_H, ML_DK, ML_DV + LANES), F32),
                        pltpu.VMEM((SUBLANES, LANES), F32),
                        pltpu.VMEM((SUBLANES, 2 * ML_QK), F32)],
        compiler_params=_cparams(("parallel", "arbitrary")),
        name="mlstm",
    )(u, u, u, u, g_row, *prm)


def _hgrn_level_matrices():
    L = HG_L
    t = jnp.arange(L)[:, None]
    r = jnp.arange(L)[None, :]
    mats = [(r <= t)]
    for l in range(int(math.log2(L))):
        half = 1 << l
        base = (t // (2 * half)) * (2 * half)
        bnd = base + half - 1
        upper = (t - base) >= half
        m_up = upper & (r > bnd) & (r <= t)
        m_lo = (~upper) & (r > t) & (r <= bnd)
        mats.append(m_up | m_lo)
    return jnp.concatenate(mats, axis=0).astype(BF16)


def _hgrn_masks():
    L = HG_L
    t = jnp.arange(L)[:, None]
    s = jnp.arange(L)[None, :]
    ms = [(t == s)]
    for l in range(int(math.log2(L))):
        half = 1 << l
        same = (t // (2 * half)) == (s // (2 * half))
        ms.append(same & ((t % (2 * half)) >= half) & ((s % (2 * half)) < half))
    return jnp.stack(ms).astype(F32)


def _hgrn_kernel(q_ref, f_ref, i_ref, g_ref, lb_ref, lvl_ref, msk_ref, nw_ref, y_ref, S_ref):
    L = HG_L
    nl = int(math.log2(L))

    @pl.when(pl.program_id(1) == 0)
    def _():
        S_ref[...] = jnp.zeros_like(S_ref)

    lg = lb_ref[...]
    mx = jnp.max(lg, axis=0, keepdims=True)
    ex = jnp.exp(lg - mx)
    pr = ex / jnp.sum(ex, axis=0, keepdims=True)
    lb = (pr[0:1] + pr[1:2]) - pr[0:1]

    q = _silu(q_ref[...])
    fp = f_ref[...]
    iv = i_ref[...]
    a1 = jnp.log(lb)
    a2 = jnp.log1p(-lb) + _log_sigmoid(fp)
    log_f = jnp.maximum(a1, a2) + jnp.log1p(jnp.exp(-jnp.abs(a1 - a2)))
    k = (1.0 - lb) * _sigmoid(-fp)

    E = _dot_sel(lvl_ref[...], log_f)
    bcum = E[0:L]
    b_last = bcum[L - 1:L]
    qb = q * jnp.exp(bcum)
    kd = k * jnp.exp(b_last - bcum)
    e_last = jnp.exp(b_last)
    qs = [q]
    ks = [k]
    for l in range(nl):
        A = jnp.exp(E[(l + 1) * L:(l + 2) * L])
        qs.append(q * A)
        ks.append(k * A)
    gg = g_ref[...]

    sls = [slice(h * HG_HEAD, (h + 1) * HG_HEAD) for h in range(HG_H)]
    qsb = [_bf(x) for x in qs]
    ksb = [_bf(x) for x in ks]
    msk = [msk_ref[l] for l in range(nl + 1)]
    attn = [msk[0] * _dot_nt(qsb[0][:, sl], ksb[0][:, sl]) for sl in sls]
    for l in range(1, nl + 1):
        attn = [a + msk[l] * _dot_nt(qsb[l][:, sl], ksb[l][:, sl]) for a, sl in zip(attn, sls)]
    ivb = _bf(iv)
    qbb = _bf(qb)
    kdb = _bf(kd)
    St = [S_ref[h] for h in range(HG_H)]
    o = [_dot(_bf(a), ivb[:, sl]) + _dot_nt(qbb[:, sl], _bf(s)) for a, sl, s in zip(attn, sls, St)]
    Sn = [s * e_last[:, sl] + _dot_tn(ivb[:, sl], kdb[:, sl]) for s, sl in zip(St, sls)]
    for h in range(HG_H):
        S_ref[h] = Sn[h]
    for h, sl in enumerate(sls):
        ms = jnp.mean(o[h] * o[h], axis=-1, keepdims=True)
        on = o[h] * lax.rsqrt(ms + NORM_EPS) * nw_ref[:, sl]
        y_ref[:, sl] = _bf(on * _silu(gg[:, sl]))


def _hgrn(u, offs, B, T, prm):
    L = HG_L
    nc = T // L
    nl = int(math.log2(L))

    def col(off, w):
        return pl.BlockSpec((L, w), lambda b, c: (b * nc + c, off // w))

    in_specs = [col(offs["hg_q"], HG_W), col(offs["hg_f"], HG_W), col(offs["hg_i"], HG_W), col(offs["hg_g"], HG_W),
                pl.BlockSpec((2, HG_W), lambda b, c: (0, 0)),
                pl.BlockSpec(((nl + 1) * L, L), lambda b, c: (0, 0)),
                pl.BlockSpec((nl + 1, L, L), lambda b, c: (0, 0, 0)),
                pl.BlockSpec((1, HG_W), lambda b, c: (0, 0))]
    return pl.pallas_call(
        _hgrn_kernel,
        grid=(B, nc),
        in_specs=in_specs,
        out_specs=pl.BlockSpec((L, HG_W), lambda b, c: (b * nc + c, 0)),
        out_shape=jax.ShapeDtypeStruct((B * T, HG_W), BF16),
        scratch_shapes=[pltpu.VMEM((HG_H, HG_HEAD, HG_HEAD), F32)],
        compiler_params=_cparams(("parallel", "arbitrary")),
        name="hgrn2",
    )(u, u, u, u, *prm)


def _mamba_kernel(z_ref, x_ref, b_ref, c_ref, dtc_ref, dtr_ref,
                  cwx_ref, cbx_ref, cwb_ref, cbb_ref, cwc_ref, cbc_ref,
                  dbr_ref, dbc_ref, alr_ref, alc_ref, dsk_ref, nw_ref, exp_ref,
                  y_ref, S_ref, tx_ref, tb_ref, tc_ref):
    L = MB_L
    GW = MB_W // MB_G
    E = MB_H // MB_G

    @pl.when(pl.program_id(1) == 0)
    def _():
        S_ref[...] = jnp.zeros_like(S_ref)
        tx_ref[...] = jnp.zeros_like(tx_ref)
        tb_ref[...] = jnp.zeros_like(tb_ref)
        tc_ref[...] = jnp.zeros_like(tc_ref)

    def conv(ref, tail_ref, w_ref, bias_ref):
        x = ref[...]
        tail = tail_ref[...]
        acc = x * w_ref[CONV_K - 1:CONV_K] + bias_ref[...]
        for j in range(1, CONV_K):
            acc = acc + _shift_rows(tail, x, j) * w_ref[CONV_K - 1 - j:CONV_K - j]
        tail_ref[...] = x[L - SUBLANES:]
        return _silu(acc)

    xs = conv(x_ref, tx_ref, cwx_ref, cbx_ref)
    Bm = conv(b_ref, tb_ref, cwb_ref, cbb_ref)
    Cm = conv(c_ref, tc_ref, cwc_ref, cbc_ref)

    dt_c = _softplus(dtc_ref[...] + dbr_ref[...])
    dt_r = _softplus(dtr_ref[...] + dbc_ref[...])
    adt_c = dt_c * (-jnp.exp(alr_ref[...]))
    adt_r = dt_r * (-jnp.exp(alc_ref[...]))
    ri = _iota((L, L), 0)
    ci = _iota((L, L), 1)
    causal = ri >= ci
    tril = jnp.where(causal, 1.0, 0.0).astype(BF16)
    triu = jnp.where(ri <= ci, 1.0, 0.0).astype(BF16)
    ac_c = _dot_sel(tril, adt_c)
    ac_r = _dot_sel_r(adt_r, triu)
    ex = exp_ref[...]
    dt_full = _dot_sel_r(dt_c, ex)
    ac_full = _dot_sel_r(ac_c, ex)
    X = xs * dt_full
    a_last = ac_full[L - 1:L]
    dec_out = jnp.exp(ac_full)
    Xd = X * jnp.exp(a_last - ac_full)
    e_last = jnp.exp(a_last)
    neg = jnp.float32(-jnp.inf)
    lane_g = _idiv(_iota((1, GW), 1), MB_HEAD)
    zz = z_ref[...]

    for g in range(MB_G):
        gs = slice(g * GW, (g + 1) * GW)
        Bg = _bf(Bm[:, g * MB_N:(g + 1) * MB_N])
        Cg = _bf(Cm[:, g * MB_N:(g + 1) * MB_N])
        CB = _dot_nt(Cg, Bg)
        Xg = _bf(X[:, gs])
        Sg = S_ref[g]
        y = _dot(Cg, _bf(Sg)) * dec_out[:, gs]
        for e in range(E):
            h = g * E + e
            Lm = jnp.exp(jnp.where(causal, ac_c[:, h:h + 1] - ac_r[h:h + 1, :], neg))
            yd = _dot(_bf(CB * Lm), Xg)
            y = y + jnp.where(lane_g == e, yd, 0.0)
        S_ref[g] = Sg * e_last[:, gs] + _dot_tn(Bg, _bf(Xd[:, gs]))
        y = y + xs[:, gs] * dsk_ref[:, gs]
        y = y * _silu(zz[:, gs])
        ms = jnp.mean(y * y, axis=-1, keepdims=True)
        y_ref[:, gs] = _bf(y * lax.rsqrt(ms + NORM_EPS) * nw_ref[:, gs])


def _mamba(u, dt_row, offs, B, T, prm):
    L = MB_L
    nc = T // L
    GN = MB_G * MB_N

    def col(off, w):
        return pl.BlockSpec((L, w), lambda b, c: (b * nc + c, off // w))

    def par(r, w):
        return pl.BlockSpec((r, w), lambda b, c: (0, 0))

    in_specs = [col(offs["mb_z"], MB_W), col(offs["mb_x"], MB_W), col(offs["mb_b"], GN), col(offs["mb_c"], GN),
                col(offs["mb_dt"], LANES),
                pl.BlockSpec((MB_H, L), lambda b, c: (0, b * nc + c)),
                par(CONV_K, MB_W), par(1, MB_W), par(CONV_K, GN), par(1, GN), par(CONV_K, GN), par(1, GN),
                par(1, LANES), par(MB_H, 1), par(1, LANES), par(MB_H, 1), par(1, MB_W), par(1, MB_W),
                par(LANES, MB_W)]
    return pl.pallas_call(
        _mamba_kernel,
        grid=(B, nc),
        in_specs=in_specs,
        out_specs=pl.BlockSpec((L, MB_W), lambda b, c: (b * nc + c, 0)),
        out_shape=jax.ShapeDtypeStruct((B * T, MB_W), BF16),
        scratch_shapes=[pltpu.VMEM((MB_G, MB_N, MB_W // MB_G), F32),
                        pltpu.VMEM((SUBLANES, MB_W), F32),
                        pltpu.VMEM((SUBLANES, GN), F32),
                        pltpu.VMEM((SUBLANES, GN), F32)],
        compiler_params=_cparams(("parallel", "arbitrary")),
        name="mamba2",
    )(u, u, u, u, u, dt_row, *prm)


def _layout(segs, n_total):
    offs, cur = {}, 0
    for name, _, _, pw in segs:
        offs[name] = cur
        cur += pw
    assert cur <= n_total

    def pack(w, dtype=BF16):
        cols = []
        for _, s, wd, pw in segs:
            cols.append(w[:, s:s + wd])
            if pw > wd:
                cols.append(jnp.zeros((w.shape[0], pw - wd), w.dtype))
        if n_total > cur:
            cols.append(jnp.zeros((w.shape[0], n_total - cur), w.dtype))
        return jnp.concatenate(cols, axis=1).astype(dtype)

    return offs, pack


_EV_SEGS = [("rw_r", 0, RW_W, RW_W), ("rw_k", RW_W, RW_W, RW_W), ("rw_v", 2 * RW_W, RW_W, RW_W),
            ("ml_qk", RW_IN, 2 * ML_QK, 2 * ML_QK), ("ml_v", RW_IN + 2 * ML_QK, ML_W, ML_W),
            ("ml_o", RW_IN + 2 * ML_QK + ML_W, ML_W, ML_W),
            ("rw_dg", 3 * RW_W + RW_DECAY_LORA + RW_ICLR_LORA, RW_GATE_LORA, RW_GATE_LORA),
            ("rw_dw", 3 * RW_W, RW_DECAY_LORA, LANES),
            ("rw_da", 3 * RW_W + RW_DECAY_LORA, RW_ICLR_LORA, LANES),
            ("ml_if", RW_IN + 2 * ML_QK + 2 * ML_W, 2 * ML_H, LANES)]
_EV_N = 7168
_OD_SEGS = [("hg_q", 0, HG_W, HG_W), ("hg_f", HG_W, HG_W, HG_W), ("hg_i", 2 * HG_W, HG_W, HG_W),
            ("hg_g", 3 * HG_W, HG_W, HG_W),
            ("mb_z", HG_IN, MB_W, MB_W), ("mb_x", HG_IN + MB_W, MB_W, MB_W),
            ("mb_b", HG_IN + 2 * MB_W, MB_G * MB_N, MB_G * MB_N),
            ("mb_c", HG_IN + 2 * MB_W + MB_G * MB_N, MB_G * MB_N, MB_G * MB_N),
            ("mb_dt", HG_IN + MB_W + MB_CONV_W, MB_H, LANES)]
_OD_N = 7680


def _row(v, width=None):
    v = v.reshape(1, -1).astype(F32)
    if width is not None and v.shape[1] < width:
        v = jnp.pad(v, ((0, 0), (0, width - v.shape[1])))
    return v


def _pad_rows(w, rows):
    return jnp.pad(w, ((0, rows - w.shape[0]), (0, 0)))


def _even_layer(x, B, T, p):
    offs, pack = _layout(_EV_SEGS, _EV_N)
    s_if = RW_IN + 2 * ML_QK + 2 * ML_W
    u, g_row = _norm_matmul(x, _row(p["norm1"]), pack(p["w_in"]), _bf(p["w_in"][:, s_if:s_if + 2 * ML_H].T))
    mu = p["rw_mu"]
    o_dw, o_da, o_dg = 3 * RW_W, 3 * RW_W + RW_DECAY_LORA, 3 * RW_W + RW_DECAY_LORA + RW_ICLR_LORA
    rw_prm = [_row(mu[0:RW_W]), _row(mu[RW_W:2 * RW_W]), _row(mu[2 * RW_W:3 * RW_W]),
              _row(mu[o_dw:o_da], LANES), _row(mu[o_da:o_dg], LANES), _row(mu[o_dg:]),
              _row(p["rw_w0"]), _bf(_pad_rows(p["rw_w2"], LANES)),
              _row(p["rw_a0"]), _bf(_pad_rows(p["rw_a2"], LANES)), _bf(p["rw_g2"]),
              _row(p["rw_k_k"]), _row(p["rw_k_a"]), _row(p["rw_r_k"]), _row(p["rw_ln_w"]), _row(p["rw_ln_b"])]
    y_a = _rwkv(u, offs, B, T, rw_prm)
    gb =jnp.concatenate([p["ml_i_b"], p["ml_f_b"]]).astype(F32)
    ml_prm = [p["ml_conv_w"].astype(F32), _row(p["ml_conv_b"]), _row(gb, LANES), gb.reshape(-1, 1),
              _row(p["ml_norm_w"])]
    y_b = _mlstm(u, g_row, offs, B, T, ml_prm)
    x = _matmul_res(y_a, y_b, _bf(p["w_out"]), x)
    return _ffn(x, _row(p["norm2"]), _bf(p["ffn_w_gate"]), _bf(p["ffn_w_up"]), _bf(p["ffn_w_down"]))


def _odd_layer(x, B, T, p, final_w):
    offs, pack = _layout(_OD_SEGS, _OD_N)
    s_dt = HG_IN + MB_W + MB_CONV_W
    u, dt_row = _norm_matmul(x, _row(p["norm1"]), pack(p["w_in"]), _bf(p["w_in"][:, s_dt:s_dt + MB_H].T))
    hg_prm = [p["hg_lb_logits"].astype(F32), _hgrn_level_matrices(), _hgrn_masks(), _row(p["hg_norm_w"])]
    y_c = _hgrn(u, offs, B, T, hg_prm)
    cw, cb = p["mb_conv_w"].astype(F32), p["mb_conv_b"].astype(F32)
    GN = MB_G * MB_N
    expand = (jnp.arange(LANES)[:, None] == (jnp.arange(MB_W)[None, :] // MB_HEAD)).astype(BF16)
    mb_prm = [cw[:, :MB_W], _row(cb[:MB_W]), cw[:, MB_W:MB_W + GN], _row(cb[MB_W:MB_W + GN]),
              cw[:, MB_W + GN:], _row(cb[MB_W + GN:]),
              _row(p["mb_dt_bias"], LANES), p["mb_dt_bias"].astype(F32).reshape(-1, 1),
              _row(p["mb_A_log"], LANES), p["mb_A_log"].astype(F32).reshape(-1, 1),
              _row(jnp.repeat(p["mb_D"], MB_HEAD)), _row(p["mb_norm_w"]), expand]
    y_d = _mamba(u, dt_row, offs, B, T, mb_prm)
    x = _matmul_res(y_c, y_d, _bf(p["w_out"]), x)
    wr = jnp.pad(p["moe_router"].astype(F32), ((0, 0), (0, LANES - N_EXPERTS)))
    idx, prob, h = _router(x, _row(p["norm2"]), wr)
    pos, src_tok, tile_expert, tile_active = _route_plan(idx[:, :2], TM_MOE)
    h = jnp.pad(h, ((0, src_tok.shape[0] - h.shape[0]), (0, 0)))
    xg = jnp.take(h, src_tok, axis=0, mode="clip")
    act = _moe_up(xg, tile_expert, tile_active, p["moe_w_gate"].astype(F32), p["moe_w_up"].astype(F32))
    yg = _moe_down(act, tile_expert, tile_active, p["moe_w_down"].astype(F32))
    y0 = jnp.take(yg, pos[:, 0], axis=0, mode="clip")
    y1 = jnp.take(yg, pos[:, 1], axis=0, mode="clip")
    return _combine(x, y0, y1, prob, _row(final_w))


def kernel(x, final_norm_w, hg_lb_logits, ev_norm1_w, ev_w_in, ev_w_out, rw_mu, rw_w0, rw_w2, rw_a0, rw_a2, rw_g2, rw_k_k, rw_k_a, rw_r_k, rw_ln_w, rw_ln_b, ml_conv_w, ml_conv_b, ml_i_b, ml_f_b, ml_norm_w, ev_norm2_w, ffn_w_gate, ffn_w_up, ffn_w_down, od_norm1_w, od_w_in, od_w_out, hg_norm_w, mb_conv_w, mb_conv_b, mb_dt_bias, mb_A_log, mb_D, mb_norm_w, od_norm2_w, moe_router, moe_w_gate, moe_w_up, moe_w_down):
    B, T, D = x.shape
    xf = x.reshape(B * T, D)
    ev = dict(norm1=ev_norm1_w[0], w_in=ev_w_in[0], w_out=ev_w_out[0], rw_mu=rw_mu[0], rw_w0=rw_w0[0],
              rw_w2=rw_w2[0], rw_a0=rw_a0[0], rw_a2=rw_a2[0], rw_g2=rw_g2[0], rw_k_k=rw_k_k[0],
              rw_k_a=rw_k_a[0], rw_r_k=rw_r_k[0], rw_ln_w=rw_ln_w[0], rw_ln_b=rw_ln_b[0],
              ml_conv_w=ml_conv_w[0], ml_conv_b=ml_conv_b[0], ml_i_b=ml_i_b[0], ml_f_b=ml_f_b[0],
              ml_norm_w=ml_norm_w[0], norm2=ev_norm2_w[0], ffn_w_gate=ffn_w_gate[0], ffn_w_up=ffn_w_up[0],
              ffn_w_down=ffn_w_down[0])
    od = dict(norm1=od_norm1_w[0], w_in=od_w_in[0], w_out=od_w_out[0], hg_lb_logits=hg_lb_logits,
              hg_norm_w=hg_norm_w[0], mb_conv_w=mb_conv_w[0], mb_conv_b=mb_conv_b[0], mb_dt_bias=mb_dt_bias[0],
              mb_A_log=mb_A_log[0], mb_D=mb_D[0], mb_norm_w=mb_norm_w[0], norm2=od_norm2_w[0],
              moe_router=moe_router[0], moe_w_gate=moe_w_gate[0], moe_w_up=moe_w_up[0],
              moe_w_down=moe_w_down[0])
    xf = _even_layer(xf, B, T, ev)
    xf = _odd_layer(xf, B, T, od, final_norm_w)
    return xf.reshape(B, T, D)
```

```python
import functools
import math

import jax
import jax.numpy as jnp
from jax import lax
from jax.experimental import pallas as pl
from jax.experimental.pallas import tpu as pltpu

F32 = jnp.float32
BF16 = jnp.bfloat16

D_MODEL = 2048
NORM_EPS = 1e-6
RW_HEAD = 64
RW_W = 1024
RW_H = RW_W // RW_HEAD
RW_DECAY_LORA = 96
RW_ICLR_LORA = 96
RW_GATE_LORA = 256
RW_IN = 3 * RW_W + RW_DECAY_LORA + RW_ICLR_LORA + RW_GATE_LORA
RW_LN_EPS = 64e-5
RW_DECAY_SCALE = math.exp(-0.5)
ML_W = 1024
ML_H = 4
ML_DV = ML_W // ML_H
ML_DK = ML_DV // 2
ML_QK = ML_H * ML_DK
ML_IN = 2 * ML_QK + 2 * ML_W + 2 * ML_H
HG_W = 1024
HG_HEAD = 128
HG_H = HG_W // HG_HEAD
HG_IN = 4 * HG_W
MB_W = 1024
MB_HEAD = 64
MB_H = MB_W // MB_HEAD
MB_G = 4
MB_N = 128
MB_CONV_W = MB_W + 2 * MB_G * MB_N
N_EXPERTS = 8

LANES = 128
SUBLANES = 8
VMEM_LIMIT = 56 * 1024 * 1024

TM_PROJ = 1024
TN_PROJ = (1024, 768, 512)
TM_FFN = 512
TF_FFN = 512
TM_MOE = 256
TF_MOE = 1408
TN_MOE = 1024
RW_L = 64
RW_BLK = 16
ML_L = 128
HG_L = 64
MB_L = 128
CONV_K = 4


def _cparams(sem):
    return pltpu.CompilerParams(dimension_semantics=sem, vmem_limit_bytes=VMEM_LIMIT)


def _proj_tn(n):
    return next(t for t in TN_PROJ if n % t == 0)


def _dot(a, b):
    return jnp.dot(a, b, preferred_element_type=F32)


def _dot_nt(a, b):
    return lax.dot_general(a, b, (((1,), (1,)), ((), ())), preferred_element_type=F32)


def _dot_tn(a, b):
    return lax.dot_general(a, b, (((0,), (0,)), ((), ())), preferred_element_type=F32)


def _bf(x):
    return x.astype(BF16)


def _split2(x):
    h = x.astype(BF16)
    l = (x - h.astype(F32)).astype(BF16)
    return h, l


def _split3(x):
    h = x.astype(BF16)
    r = x - h.astype(F32)
    m = r.astype(BF16)
    l = (r - m.astype(F32)).astype(BF16)
    return h, m, l


def _dot_sel(sel_bf16, x):
    h, m, l = _split3(x)
    return _dot(sel_bf16, h) + _dot(sel_bf16, m) + _dot(sel_bf16, l)


def _dot_sel_r(x, sel_bf16):
    h, m, l = _split3(x)
    return _dot(h, sel_bf16) + _dot(m, sel_bf16) + _dot(l, sel_bf16)


def _dot3(a, b):
    ah, al = _split2(a)
    bh, bl = _split2(b)
    return _dot(ah, bh) + _dot(ah, bl) + _dot(al, bh)


def _sigmoid(x):
    return 1.0 / (1.0 + jnp.exp(-x))


def _silu(x):
    return x * _sigmoid(x)


def _log_sigmoid(x):
    return -(jnp.maximum(-x, 0.0) + jnp.log1p(jnp.exp(-jnp.abs(x))))


def _softplus(x):
    return jnp.maximum(x, 0.0) + jnp.log1p(jnp.exp(-jnp.abs(x)))


def _iota(shape, dim):
    return lax.broadcasted_iota(jnp.int32, shape, dim)


def _idiv(x, d):
    sh = d.bit_length() - 1
    assert d == 1 << sh
    return lax.shift_right_logical(x, jnp.int32(sh))


def _shift_rows(tail, x, j):
    xc = jnp.concatenate([tail, x], axis=0)
    return pltpu.roll(xc, j, 0)[SUBLANES:]


def _norm_matmul_kernel(x_ref, nw_ref, w_ref, wt_ref, o_ref, ot_ref, h_ref):
    @pl.when(pl.program_id(1) == 0)
    def _():
        x = x_ref[...]
        ms = jnp.mean(x * x, axis=-1, keepdims=True)
        h_ref[...] = _bf(x * lax.rsqrt(ms + NORM_EPS) * nw_ref[...])
        ot_ref[...] = _dot_nt(wt_ref[...], h_ref[...])

    o_ref[...] = _dot(h_ref[...], w_ref[...])


def _norm_matmul(x, nw, w, wt):
    m, d = x.shape
    n = w.shape[1]
    r = wt.shape[0]
    tn = _proj_tn(n)
    return pl.pallas_call(
        _norm_matmul_kernel,
        grid=(m // TM_PROJ, n // tn),
        in_specs=[pl.BlockSpec((TM_PROJ, d), lambda i, j: (i, 0)),
                  pl.BlockSpec((1, d), lambda i, j: (0, 0)),
                  pl.BlockSpec((d, tn), lambda i, j: (0, j)),
                  pl.BlockSpec((r, d), lambda i, j: (0, 0))],
        out_specs=[pl.BlockSpec((TM_PROJ, tn), lambda i, j: (i, j)),
                   pl.BlockSpec((r, TM_PROJ), lambda i, j: (0, i))],
        out_shape=[jax.ShapeDtypeStruct((m, n), F32),
                   jax.ShapeDtypeStruct((r, m), F32)],
        scratch_shapes=[pltpu.VMEM((TM_PROJ, d), BF16)],
        compiler_params=_cparams(("parallel", "arbitrary")),
        name="norm_matmul",
    )(x, nw, w, wt)


def _matmul_res_kernel(ya_ref, yb_ref, wa_ref, wb_ref, r_ref, o_ref):
    o_ref[...] = r_ref[...] + _dot(ya_ref[...], wa_ref[...]) + _dot(yb_ref[...], wb_ref[...])


def _matmul_res(ya, yb, w, res):
    m, k = ya.shape
    n = w.shape[1]
    tn = _proj_tn(n)
    return pl.pallas_call(
        _matmul_res_kernel,
        grid=(m // TM_PROJ, n // tn),
        in_specs=[pl.BlockSpec((TM_PROJ, k), lambda i, j: (i, 0)),
                  pl.BlockSpec((TM_PROJ, k), lambda i, j: (i, 0)),
                  pl.BlockSpec((k, tn), lambda i, j: (0, j)),
                  pl.BlockSpec((k, tn), lambda i, j: (1, j)),
                  pl.BlockSpec((TM_PROJ, tn), lambda i, j: (i, j))],
        out_specs=pl.BlockSpec((TM_PROJ, tn), lambda i, j: (i, j)),
        out_shape=jax.ShapeDtypeStruct((m, n), F32),
        compiler_params=_cparams(("parallel", "arbitrary")),
        name="matmul_res",
    )(ya, yb, w, w, res)


def _ffn_kernel(x_ref, nw_ref, wg_ref, wu_ref, wd_ref, o_ref, h_ref):
    f = pl.program_id(1)

    @pl.when(f == 0)
    def _():
        x = x_ref[...]
        ms = jnp.mean(x * x, axis=-1, keepdims=True)
        h_ref[...] = _bf(x * lax.rsqrt(ms + NORM_EPS) * nw_ref[...])
        o_ref[...] = x

    h = h_ref[...]
    act = _silu(_dot(h, wg_ref[...])) * _dot(h, wu_ref[...])
    o_ref[...] += _dot(_bf(act), wd_ref[...])


def _ffn(x, nw, wg, wu, wd):
    m, d = x.shape
    f = wg.shape[1]
    return pl.pallas_call(
        _ffn_kernel,
        grid=(m // TM_FFN, f // TF_FFN),
        in_specs=[pl.BlockSpec((TM_FFN, d), lambda i, j: (i, 0)),
                  pl.BlockSpec((1, d), lambda i, j: (0, 0)),
                  pl.BlockSpec((d, TF_FFN), lambda i, j: (0, j)),
                  pl.BlockSpec((d, TF_FFN), lambda i, j: (0, j)),
                  pl.BlockSpec((TF_FFN, d), lambda i, j: (j, 0))],
        out_specs=pl.BlockSpec((TM_FFN, d), lambda i, j: (i, 0)),
        out_shape=jax.ShapeDtypeStruct((m, d), F32),
        scratch_shapes=[pltpu.VMEM((TM_FFN, d), BF16)],
        compiler_params=_cparams(("parallel", "arbitrary")),
        name="ffn_swiglu",
    )(x, nw, wg, wu, wd)


def _router_kernel(x_ref, nw_ref, wr_ref, i_ref, p_ref, h_ref, *, n_blocks):
    step = pl.program_id(0)

    @pl.when(step < n_blocks)
    def _():
        _router_block(x_ref, nw_ref, wr_ref, i_ref, p_ref, h_ref)

    @pl.when(step >= n_blocks)
    def _():
        h_ref[...] = jnp.zeros_like(h_ref)


def _router_block(x_ref, nw_ref, wr_ref, i_ref, p_ref, h_ref):
    x = x_ref[...]
    ms = jnp.mean(x * x, axis=-1, keepdims=True)
    h = x * lax.rsqrt(ms + NORM_EPS) * nw_ref[...]
    h_ref[...] = _bf(h)
    wr = wr_ref[...]
    hh, hl = _split2(h)
    wh, wl = _split2(wr)
    logits = _dot(hh, wh) + _dot(hh, wl) + _dot(hl, wh)
    lane = _iota(logits.shape, 1)
    neg = jnp.float32(-jnp.inf)
    logits = jnp.where(lane < N_EXPERTS, logits, neg)
    v1 = jnp.max(logits, axis=-1, keepdims=True)
    i1 = jnp.min(jnp.where(logits == v1, lane, LANES), axis=-1, keepdims=True)
    rest = jnp.where(lane == i1, neg, logits)
    v2 = jnp.max(rest, axis=-1, keepdims=True)
    i2 = jnp.min(jnp.where(rest == v2, lane, LANES), axis=-1, keepdims=True)
    e2 = jnp.exp(v2 - v1)
    p1 = 1.0 / (1.0 + e2)
    p2 = e2 / (1.0 + e2)
    i_ref[...] = jnp.where(lane == 0, i1, jnp.where(lane == 1, i2, 0))
    p_ref[...] = jnp.where(lane == 0, p1, jnp.where(lane == 1, p2, 0.0))


def _router(x, nw, wr, table_rows):
    m, d = x.shape
    nb = m // TM_FFN
    tok = lambda i: (jnp.minimum(i, nb - 1), 0)
    return pl.pallas_call(
        functools.partial(_router_kernel, n_blocks=nb),
        grid=(table_rows // TM_FFN,),
        in_specs=[pl.BlockSpec((TM_FFN, d), tok),
                  pl.BlockSpec((1, d), lambda i: (0, 0)),
                  pl.BlockSpec((d, LANES), lambda i: (0, 0))],
        out_specs=[pl.BlockSpec((TM_FFN, LANES), tok),
                   pl.BlockSpec((TM_FFN, LANES), tok),
                   pl.BlockSpec((TM_FFN, d), lambda i: (i, 0))],
        out_shape=[jax.ShapeDtypeStruct((m, LANES), jnp.int32),
                   jax.ShapeDtypeStruct((m, LANES), F32),
                   jax.ShapeDtypeStruct((table_rows, d), BF16)],
        compiler_params=_cparams(("arbitrary",)),
        name="moe_router",
    )(x, nw, wr)


def _route_plan(top_idx, tm):
    m = top_idx.shape[0]
    n_rows = 2 * m + N_EXPERTS * tm
    e_flat = top_idx.reshape(-1)
    onehot = (e_flat[:, None] == jnp.arange(N_EXPERTS, dtype=jnp.int32)[None, :]).astype(jnp.int32)
    rank = jnp.cumsum(onehot, axis=0) - onehot
    counts = jnp.sum(onehot, axis=0)
    padded = ((counts + tm - 1) // tm) * tm
    ends = jnp.cumsum(padded)
    off = ends - padded
    pos = jnp.sum(onehot * (off[None, :] + rank), axis=1)
    src_tok = jnp.zeros((n_rows,), jnp.int32).at[pos].set(jnp.arange(2 * m, dtype=jnp.int32) // 2,
                                                          unique_indices=True)
    tile_start = jnp.arange(n_rows // tm, dtype=jnp.int32) * tm
    tile_expert = jnp.minimum(jnp.sum((tile_start[:, None] >= ends[None, :]).astype(jnp.int32), axis=1),
                              N_EXPERTS - 1)
    tile_active = (tile_start < ends[-1]).astype(jnp.int32)
    return pos.reshape(m, 2), src_tok, tile_expert, tile_active


def _new_weight_block(te_ref, i):
    return (i == 0) | (te_ref[i] != te_ref[jnp.maximum(i - 1, 0)])


def _moe_up_kernel(te_ref, ta_ref, x_ref, wg_ref, wu_ref, a_ref, wgb_ref, wub_ref):
    i = pl.program_id(1)

    @pl.when(_new_weight_block(te_ref, i))
    def _():
        wgb_ref[...] = _bf(wg_ref[0])
        wub_ref[...] = _bf(wu_ref[0])

    @pl.when(ta_ref[i] == 1)
    def _():
        x = x_ref[...]
        act = _silu(_dot(x, wgb_ref[...])) * _dot(x, wub_ref[...])
        a_ref[...] = _bf(act)

    @pl.when(ta_ref[i] == 0)
    def _():
        a_ref[...] = jnp.zeros_like(a_ref)


def _moe_up(xg, tile_expert, tile_active, wg, wu):
    n_rows, d = xg.shape
    fe = wg.shape[2]
    wspec = pl.BlockSpec((1, d, TF_MOE), lambda f, i, te, ta: (te[i], 0, f), pipeline_mode=pl.Buffered(1))
    grid_spec = pltpu.PrefetchScalarGridSpec(
        num_scalar_prefetch=2,
        grid=(fe // TF_MOE, n_rows // TM_MOE),
        in_specs=[pl.BlockSpec((TM_MOE, d), lambda f, i, te, ta: (i, 0)), wspec, wspec],
        out_specs=pl.BlockSpec((TM_MOE, TF_MOE), lambda f, i, te, ta: (i, f)),
        scratch_shapes=[pltpu.VMEM((d, TF_MOE), BF16), pltpu.VMEM((d, TF_MOE), BF16)],
    )
    return pl.pallas_call(
        _moe_up_kernel,
        grid_spec=grid_spec,
        out_shape=jax.ShapeDtypeStruct((n_rows, fe), BF16),
        compiler_params=_cparams(("arbitrary", "arbitrary")),
        name="moe_up",
    )(tile_expert, tile_active, xg, wg, wu)


def _moe_down_kernel(te_ref, ta_ref, a_ref, wd_ref, y_ref, wdb_ref):
    i = pl.program_id(1)

    @pl.when(_new_weight_block(te_ref, i))
    def _():
        wdb_ref[...] = _bf(wd_ref[0])

    @pl.when(ta_ref[i] == 1)
    def _():
        y_ref[...] = _bf(_dot(a_ref[...], wdb_ref[...]))

    @pl.when(ta_ref[i] == 0)
    def _():
        y_ref[...] = jnp.zeros_like(y_ref)


def _moe_down(act, tile_expert, tile_active, wd):
    n_rows, fe = act.shape
    d = wd.shape[2]
    grid_spec = pltpu.PrefetchScalarGridSpec(
        num_scalar_prefetch=2,
        grid=(d // TN_MOE, n_rows // TM_MOE),
        in_specs=[pl.BlockSpec((TM_MOE, fe), lambda n, i, te, ta: (i, 0)),
                  pl.BlockSpec((1, fe, TN_MOE), lambda n, i, te, ta: (te[i], 0, n))],
        out_specs=pl.BlockSpec((TM_MOE, TN_MOE), lambda n, i, te, ta: (i, n)),
        scratch_shapes=[pltpu.VMEM((fe, TN_MOE), BF16)],
    )
    return pl.pallas_call(
        _moe_down_kernel,
        grid_spec=grid_spec,
        out_shape=jax.ShapeDtypeStruct((n_rows, d), BF16),
        compiler_params=_cparams(("arbitrary", "arbitrary")),
        name="moe_down",
    )(tile_expert, tile_active, act, wd)


def _combine_kernel(x_ref, y0_ref, y1_ref, p_ref, fw_ref, o_ref):
    p = p_ref[...]
    y = x_ref[...] + p[:, 0:1] * y0_ref[...].astype(F32) + p[:, 1:2] * y1_ref[...].astype(F32)
    ms = jnp.mean(y * y, axis=-1, keepdims=True)
    o_ref[...] = y * lax.rsqrt(ms + NORM_EPS) * fw_ref[...]


def _combine(x, y0, y1, prob, final_w):
    m, d = x.shape
    row = pl.BlockSpec((TM_FFN, d), lambda i: (i, 0))
    return pl.pallas_call(
        _combine_kernel,
        grid=(m // TM_FFN,),
        in_specs=[row, row, row, pl.BlockSpec((TM_FFN, LANES), lambda i: (i, 0)),
                  pl.BlockSpec((1, d), lambda i: (0, 0))],
        out_specs=row,
        out_shape=jax.ShapeDtypeStruct((m, d), F32),
        compiler_params=_cparams(("parallel",)),
        name="moe_combine",
    )(x, y0, y1, prob, final_w)


def _rwkv_kernel(r_ref, k_ref, v_ref, dw_ref, da_ref, dg_ref,
                 mur_ref, muk_ref, muv_ref, mudw_ref, muda_ref, mudg_ref,
                 w0_ref, w2_ref, a0_ref, a2_ref, g2_ref, kk_ref, ka_ref, rk_ref, lnw_ref, lnb_ref,
                 o_ref,
                 H_ref, tr_ref, tk_ref, tv_ref, tdw_ref, tda_ref, tdg_ref):
    L = RW_L
    L2 = 2 * L

    @pl.when(pl.program_id(1) == 0)
    def _():
        H_ref[...] = jnp.zeros_like(H_ref)
        for t in (tr_ref, tk_ref, tv_ref, tdw_ref, tda_ref, tdg_ref):
            t[...] = jnp.zeros_like(t)

    def shift_mix(ref, tail, mu):
        x = ref[...]
        prev = _shift_rows(tail[...], x, 1)
        tail[...] = x[L - SUBLANES:]
        return x + (prev - x) * mu[...]

    r = shift_mix(r_ref, tr_ref, mur_ref)
    k = shift_mix(k_ref, tk_ref, muk_ref)
    v = shift_mix(v_ref, tv_ref, muv_ref)
    dw = shift_mix(dw_ref, tdw_ref, mudw_ref)
    da = shift_mix(da_ref, tda_ref, muda_ref)
    dg = shift_mix(dg_ref, tdg_ref, mudg_ref)

    log_w = -RW_DECAY_SCALE * _sigmoid(w0_ref[...] + _dot(_bf(jnp.tanh(dw)), w2_ref[...]))
    a = _sigmoid(a0_ref[...] + _dot(_bf(da), a2_ref[...]))
    g = _dot(_bf(_sigmoid(dg)), g2_ref[...])

    ri = _iota((LANES, LANES), 0)
    ci = _iota((LANES, LANES), 1)
    hsum = jnp.where(_idiv(ri, RW_HEAD) == _idiv(ci, RW_HEAD), 1.0, 0.0).astype(BF16)
    same_head = _idiv(ri, L) == _idiv(ci, L)
    tril_s = jnp.where(same_head & (ri > ci), 1.0, 0.0)
    tril_i = jnp.where(same_head & (ri >= ci), 1.0, 0.0)
    blk_d = jnp.where(_idiv(ri, RW_BLK) == _idiv(ci, RW_BLK), 1.0, 0.0)
    eye = jnp.where(ri == ci, 1.0, 0.0)
    cum = jnp.where(_iota((L, L), 0) >= _iota((L, L), 1), 1.0, 0.0).astype(BF16)
    lane = _iota((1, LANES), 1)
    m0 = lane < RW_HEAD

    def head_sum(x):
        h, l = _split2(x)
        return _dot(h, hsum) + _dot(l, hsum)

    kk = k * kk_ref[...]
    k2 = k * (1.0 + (a - 1.0) * ka_ref[...])
    cw = _dot_sel(cum, log_w)
    e_prev = jnp.exp(cw - log_w)
    e_inv = jnp.exp(-cw)
    e_cw = jnp.exp(cw)
    wl = cw[L - 1:L]
    e_end = jnp.exp(wl - cw)
    e_wl = jnp.exp(wl)
    rk2 = r * k2 * rk_ref[...]

    def stack(x):
        x0 = jnp.where(m0, x, 0.0)
        return jnp.concatenate([x0, x - x0], axis=0)

    def mm3(xs, ys):
        return _dot(xs[0], ys[0]) + _dot(xs[0], ys[1]) + _dot(xs[1], ys[0])

    NP = RW_H // 2
    sls = [slice(p * LANES, (p + 1) * LANES) for p in range(NP)]

    def each(fn, *lists):
        return [fn(*args) for args in zip(*lists)]

    kk_l = [kk[:, sl] for sl in sls]
    kk_l = each(lambda x: x * lax.rsqrt(jnp.maximum(head_sum(x * x), 1e-12)), kk_l)
    kka_l = [x * a[:, sl] for x, sl in zip(kk_l, sls)]
    At = [_bf(stack(-x * e_prev[:, sl])) for x, sl in zip(kk_l, sls)]
    Bt = [_bf(stack(x * e_inv[:, sl])) for x, sl in zip(kka_l, sls)]
    Kt = [_bf(stack(k2[:, sl] * e_inv[:, sl])) for sl in sls]
    Rt_f = [stack(r[:, sl] * e_cw[:, sl]) for sl in sls]
    Rt = each(_bf, Rt_f)
    Bh = [_bf(stack(x * e_end[:, sl])) for x, sl in zip(kka_l, sls)]
    Kh = [_bf(stack(k2[:, sl] * e_end[:, sl])) for sl in sls]
    Vs = [_bf(stack(v[:, sl])) for sl in sls]

    Mab = each(lambda x, y: _dot_nt(x, y) * tril_s, At, Bt)
    Mak = each(lambda x, y: _bf(_dot_nt(x, y) * tril_s), At, Kt)
    Arb = each(lambda x, y: _bf(_dot_nt(x, y) * tril_i), Rt, Bt)
    Ark = each(lambda x, y: _bf(_dot_nt(x, y) * tril_i), Rt, Kt)

    MD = each(lambda m: m * blk_d, Mab)
    Nn = each(lambda m, d: _split2(m - d), Mab, MD)
    P = each(lambda d: eye + d, MD)
    MDs = each(_split2, MD)
    S = each(mm3, MDs, MDs)
    for it in range(3):
        Ss = each(_split2, S)
        P = each(lambda p_, s_: p_ + mm3(_split2(p_), s_), P, Ss)
        if it < 2:
            S = each(mm3, Ss, Ss)
    Ps = each(_split2, P)
    X = each(mm3, Ps, Nn)
    Xs = each(_split2, X)
    X2 = each(lambda x: _split2(mm3(x, x)), Xs)
    Y = each(lambda x: eye + x, X)
    Y = each(lambda y_, x2: y_ + mm3(_split2(y_), x2), Y, X2)
    T = each(lambda y_, p_: _split2(mm3(_split2(y_), p_)), Y, Ps)

    P1 = each(lambda t, x: _bf(_dot(t[0], x) + _dot(t[1], x)), T, At)
    MV = each(lambda m, x: _split2(_dot(m, x)), Mak, Vs)
    P2 = each(lambda t, x: _bf(mm3(t, x)), T, MV)
    G = [eye * e_wl[:, sl] + _dot_tn(b, p1) for sl, b, p1 in zip(sls, Bh, P1)]
    J = each(lambda b, p2, kh, vs: _dot_tn(b, p2) + _dot_tn(kh, vs), Bh, P2, Kh, Vs)
    Q = each(lambda rf, ar, p1: rf + _dot(ar, p1), Rt_f, Arb, P1)
    Z = each(lambda ar, p2, ak, vs: _dot(ar, p2) + _dot(ak, vs), Arb, P2, Ark, Vs)

    Hs = [_split2(H_ref[p]) for p in range(NP)]
    Yst = each(lambda q, h, z: mm3(_split2(q), h) + z, Q, Hs, Z)
    Hn = each(lambda g_, h, j: mm3(_split2(g_), h) + j, G, Hs, J)
    for p in range(NP):
        H_ref[p] = Hn[p]
    y_l = each(lambda y_: y_[:L] + y_[L:], Yst)

    mean = each(lambda y_: head_sum(y_) * (1.0 / RW_HEAD), y_l)
    yc = each(lambda y_, m: y_ - m, y_l, mean)
    var = each(lambda c: head_sum(c * c) * (1.0 / RW_HEAD), yc)
    yn = [c * lax.rsqrt(vr + RW_LN_EPS) * lnw_ref[:, sl] + lnb_ref[:, sl] for c, vr, sl in zip(yc, var, sls)]
    bonus = [head_sum(rk2[:, sl]) * v[:, sl] for sl in sls]
    for p in range(NP):
        o_ref[:, sls[p]] = _bf((yn[p] + bonus[p]) * g[:, sls[p]])


def _rwkv(u, offs, B, T, prm):
    L = RW_L
    nc = T // L

    def col(off, w):
        return pl.BlockSpec((L, w), lambda b, c: (b * nc + c, off // w))

    def par(w, rows=1):
        return pl.BlockSpec((rows, w), lambda b, c: (0, 0))

    in_specs = [col(offs["rw_r"], RW_W), col(offs["rw_k"], RW_W), col(offs["rw_v"], RW_W),
                col(offs["rw_dw"], LANES), col(offs["rw_da"], LANES), col(offs["rw_dg"], RW_GATE_LORA),
                par(RW_W), par(RW_W), par(RW_W), par(LANES), par(LANES), par(RW_GATE_LORA),
                par(RW_W), par(RW_W, LANES), par(RW_W), par(RW_W, LANES), par(RW_W, RW_GATE_LORA),
                par(RW_W), par(RW_W), par(RW_W), par(RW_W), par(RW_W)]
    return pl.pallas_call(
        _rwkv_kernel,
        grid=(B, nc),
        in_specs=in_specs,
        out_specs=pl.BlockSpec((L, RW_W), lambda b, c: (b * nc + c, 0)),
        out_shape=jax.ShapeDtypeStruct((B * T, RW_W), BF16),
        scratch_shapes=[pltpu.VMEM((RW_H // 2, LANES, LANES), F32),
                        pltpu.VMEM((SUBLANES, RW_W), F32), pltpu.VMEM((SUBLANES, RW_W), F32),
                        pltpu.VMEM((SUBLANES, RW_W), F32), pltpu.VMEM((SUBLANES, LANES), F32),
                        pltpu.VMEM((SUBLANES, LANES), F32), pltpu.VMEM((SUBLANES, RW_GATE_LORA), F32)],
        compiler_params=_cparams(("parallel", "arbitrary")),
        name="rwkv7",
    )(u, u, u, u, u, u, *prm)


def _mlstm_kernel(qk_ref, v_ref, o_ref, gc_ref, gr_ref, cw_ref, cb_ref, gbr_ref, gbc_ref, nw_ref,
                  y_ref, C_ref, m_ref, tail_ref):
    L = ML_L
    DK, DV = ML_DK, ML_DV

    @pl.when(pl.program_id(1) == 0)
    def _():
        C_ref[...] = jnp.zeros_like(C_ref)
        m_ref[...] = jnp.zeros_like(m_ref)
        tail_ref[...] = jnp.zeros_like(tail_ref)

    x = qk_ref[...]
    tail = tail_ref[...]
    acc = x * cw_ref[CONV_K - 1:CONV_K] + cb_ref[...]
    for j in range(1, CONV_K):
        acc = acc + _shift_rows(tail, x, j) * cw_ref[CONV_K - 1 - j:CONV_K - j]
    tail_ref[...] = x[L - SUBLANES:]
    qk = _silu(acc)

    gc = gc_ref[...] + gbr_ref[...]
    gr = gr_ref[...] + gbc_ref[...]
    fl_c = _log_sigmoid(gc)
    fl_r = _log_sigmoid(gr)
    ri = _iota((L, L), 0)
    ci = _iota((L, L), 1)
    causal = ri >= ci
    tril = jnp.where(causal, 1.0, 0.0).astype(BF16)
    triu = jnp.where(ri <= ci, 1.0, 0.0).astype(BF16)
    b_c = _dot_sel(tril, fl_c)
    b_r = _dot_sel_r(fl_r, triu)
    one_col = jnp.where(_iota((L, LANES), 1) == 0, 1.0, 0.0)
    neg = jnp.float32(-jnp.inf)
    vv = v_ref[...]
    oo = o_ref[...]

    for h in range(ML_H):
        q_h = _bf(qk[:, h * DK:(h + 1) * DK] * (DK ** -0.5))
        k_f = qk[:, ML_QK + h * DK:ML_QK + (h + 1) * DK]
        k_h = _bf(k_f)
        v_ext = _bf(jnp.concatenate([vv[:, h * DV:(h + 1) * DV], one_col], axis=1))
        bc = b_c[:, ML_H + h:ML_H + h + 1]
        br = b_r[ML_H + h:ML_H + h + 1, :]
        il_c = gc[:, h:h + 1]
        il_r = gr[h:h + 1, :]
        m_prev = m_ref[h:h + 1, 0:1]
        C_prev = C_ref[h]

        D = jnp.where(causal, bc - br + il_r, neg)
        m_t = jnp.maximum(bc + m_prev, jnp.max(D, axis=-1, keepdims=True))
        S = _dot_nt(q_h, k_h) * jnp.exp(D - m_t)
        inter = jnp.exp(bc + m_prev - m_t)
        num = inter * _dot(q_h, _bf(C_prev)) + _dot(_bf(S), v_ext)
        den = num[:, DV:DV + 1]
        hh = num[:, :DV] / jnp.maximum(jnp.abs(den), jnp.exp(-m_t))
        ms = jnp.mean(hh * hh, axis=-1, keepdims=True)
        hn = hh * lax.rsqrt(ms + NORM_EPS) * nw_ref[:, h * DV:(h + 1) * DV]
        y_ref[:, h * DV:(h + 1) * DV] = _bf(hn * _sigmoid(oo[:, h * DV:(h + 1) * DV]))

        g_end = bc[L - 1:L]
        wst = g_end - bc + il_c
        m_new = jnp.maximum(g_end + m_prev, jnp.max(wst, axis=0, keepdims=True))
        kt = _bf(k_f * jnp.exp(wst - m_new))
        C_ref[h] = jnp.exp(g_end + m_prev - m_new) * C_prev + _dot_tn(kt, v_ext)
        m_ref[h:h + 1, :] = jnp.broadcast_to(m_new, (1, LANES))


def _mlstm(u, g_row, offs, B, T, prm):
    L = ML_L
    nc = T // L

    def col(off, w):
        return pl.BlockSpec((L, w), lambda b, c: (b * nc + c, off // w))

    def par(r, w):
        return pl.BlockSpec((r, w), lambda b, c: (0, 0))

    in_specs = [col(offs["ml_qk"], 2 * ML_QK), col(offs["ml_v"], ML_W), col(offs["ml_o"], ML_W),
                col(offs["ml_if"], LANES),
                pl.BlockSpec((SUBLANES, L), lambda b, c: (0, b * nc + c)),
                par(CONV_K, 2 * ML_QK), par(1, 2 * ML_QK), par(1, LANES), par(SUBLANES, 1), par(1, ML_W)]
    return pl.pallas_call(
        _mlstm_kernel,
        grid=(B, nc),
        in_specs=in_specs,
        out_specs=pl.BlockSpec((L, ML_W), lambda b, c: (b * nc + c, 0)),
        out_shape=jax.ShapeDtypeStruct((B * T, ML_W), BF16),
        scratch_shapes=[pltpu.VMEM((ML_H, ML_DK, ML_DV + LANES), F32),
                        pltpu.VMEM((SUBLANES, LANES), F32),
                        pltpu.VMEM((SUBLANES, 2 * ML_QK), F32)],
        compiler_params=_cparams(("parallel", "arbitrary")),
        name="mlstm",
    )(u, u, u, u, g_row, *prm)


def _hgrn_level_matrices():
    L = HG_L
    t = jnp.arange(L)[:, None]
    r = jnp.arange(L)[None, :]
    mats = [(r <= t)]
    for l in range(int(math.log2(L))):
        half = 1 << l
        base = (t // (2 * half)) * (2 * half)
        bnd = base + half - 1
        upper = (t - base) >= half
        m_up = upper & (r > bnd) & (r <= t)
        m_lo = (~upper) & (r > t) & (r <= bnd)
        mats.append(m_up | m_lo)
    return jnp.concatenate(mats, axis=0).astype(BF16)


def _hgrn_masks():
    L = HG_L
    t = jnp.arange(L)[:, None]
    s = jnp.arange(L)[None, :]
    ms = [(t == s)]
    for l in range(int(math.log2(L))):
        half = 1 << l
        same = (t // (2 * half)) == (s // (2 * half))
        ms.append(same & ((t % (2 * half)) >= half) & ((s % (2 * half)) < half))
    return jnp.stack(ms).astype(F32)


def _hgrn_kernel(q_ref, f_ref, i_ref, g_ref, lb_ref, lvl_ref, msk_ref, nw_ref, y_ref, S_ref):
    L = HG_L
    nl = int(math.log2(L))

    @pl.when(pl.program_id(1) == 0)
    def _():
        S_ref[...] = jnp.zeros_like(S_ref)

    lg = lb_ref[...]
    mx = jnp.max(lg, axis=0, keepdims=True)
    ex = jnp.exp(lg - mx)
    pr = ex / jnp.sum(ex, axis=0, keepdims=True)
    lb = (pr[0:1] + pr[1:2]) - pr[0:1]

    q = _silu(q_ref[...])
    fp = f_ref[...]
    iv = i_ref[...]
    a1 = jnp.log(lb)
    a2 = jnp.log1p(-lb) + _log_sigmoid(fp)
    log_f = jnp.maximum(a1, a2) + jnp.log1p(jnp.exp(-jnp.abs(a1 - a2)))
    k = (1.0 - lb) * _sigmoid(-fp)

    E = _dot_sel(lvl_ref[...], log_f)
    bcum = E[0:L]
    b_last = bcum[L - 1:L]
    qb = q * jnp.exp(bcum)
    kd = k * jnp.exp(b_last - bcum)
    e_last = jnp.exp(b_last)
    qs = [q]
    ks = [k]
    for l in range(nl):
        A = jnp.exp(E[(l + 1) * L:(l + 2) * L])
        qs.append(q * A)
        ks.append(k * A)
    gg = g_ref[...]

    sls = [slice(h * HG_HEAD, (h + 1) * HG_HEAD) for h in range(HG_H)]
    qsb = [_bf(x) for x in qs]
    ksb = [_bf(x) for x in ks]
    msk = [msk_ref[l] for l in range(nl + 1)]
    attn = [msk[0] * _dot_nt(qsb[0][:, sl], ksb[0][:, sl]) for sl in sls]
    for l in range(1, nl + 1):
        attn = [a + msk[l] * _dot_nt(qsb[l][:, sl], ksb[l][:, sl]) for a, sl in zip(attn, sls)]
    ivb = _bf(iv)
    qbb = _bf(qb)
    kdb = _bf(kd)
    St = [S_ref[h] for h in range(HG_H)]
    o = [_dot(_bf(a), ivb[:, sl]) + _dot_nt(qbb[:, sl], _bf(s)) for a, sl, s in zip(attn, sls, St)]
    Sn = [s * e_last[:, sl] + _dot_tn(ivb[:, sl], kdb[:, sl]) for s, sl in zip(St, sls)]
    for h in range(HG_H):
        S_ref[h] = Sn[h]
    for h, sl in enumerate(sls):
        ms = jnp.mean(o[h] * o[h], axis=-1, keepdims=True)
        on = o[h] * lax.rsqrt(ms + NORM_EPS) * nw_ref[:, sl]
        y_ref[:, sl] = _bf(on * _silu(gg[:, sl]))


def _hgrn(u, offs, B, T, prm):
    L = HG_L
    nc = T // L
    nl = int(math.log2(L))

    def col(off, w):
        return pl.BlockSpec((L, w), lambda b, c: (b * nc + c, off // w))

    in_specs = [col(offs["hg_q"], HG_W), col(offs["hg_f"], HG_W), col(offs["hg_i"], HG_W), col(offs["hg_g"], HG_W),
                pl.BlockSpec((2, HG_W), lambda b, c: (0, 0)),
                pl.BlockSpec(((nl + 1) * L, L), lambda b, c: (0, 0)),
                pl.BlockSpec((nl + 1, L, L), lambda b, c: (0, 0, 0)),
                pl.BlockSpec((1, HG_W), lambda b, c: (0, 0))]
    return pl.pallas_call(
        _hgrn_kernel,
        grid=(B, nc),
        in_specs=in_specs,
        out_specs=pl.BlockSpec((L, HG_W), lambda b, c: (b * nc + c, 0)),
        out_shape=jax.ShapeDtypeStruct((B * T, HG_W), BF16),
        scratch_shapes=[pltpu.VMEM((HG_H, HG_HEAD, HG_HEAD), F32)],
        compiler_params=_cparams(("parallel", "arbitrary")),
        name="hgrn2",
    )(u, u, u, u, *prm)


def _mamba_kernel(z_ref, x_ref, b_ref, c_ref, dtc_ref, dtr_ref,
                  cwx_ref, cbx_ref, cwb_ref, cbb_ref, cwc_ref, cbc_ref,
                  dbr_ref, dbc_ref, alr_ref, alc_ref, dsk_ref, nw_ref, exp_ref,
                  y_ref, S_ref, tx_ref, tb_ref, tc_ref):
    L = MB_L
    GW = MB_W // MB_G
    E = MB_H // MB_G

    @pl.when(pl.program_id(1) == 0)
    def _():
        S_ref[...] = jnp.zeros_like(S_ref)
        tx_ref[...] = jnp.zeros_like(tx_ref)
        tb_ref[...] = jnp.zeros_like(tb_ref)
        tc_ref[...] = jnp.zeros_like(tc_ref)

    def conv(ref, tail_ref, w_ref, bias_ref):
        x = ref[...]
        tail = tail_ref[...]
        acc = x * w_ref[CONV_K - 1:CONV_K] + bias_ref[...]
        for j in range(1, CONV_K):
            acc = acc + _shift_rows(tail, x, j) * w_ref[CONV_K - 1 - j:CONV_K - j]
        tail_ref[...] = x[L - SUBLANES:]
        return _silu(acc)

    xs = conv(x_ref, tx_ref, cwx_ref, cbx_ref)
    Bm = conv(b_ref, tb_ref, cwb_ref, cbb_ref)
    Cm = conv(c_ref, tc_ref, cwc_ref, cbc_ref)

    dt_c = _softplus(dtc_ref[...] + dbr_ref[...])
    dt_r = _softplus(dtr_ref[...] + dbc_ref[...])
    adt_c = dt_c * (-jnp.exp(alr_ref[...]))
    adt_r = dt_r * (-jnp.exp(alc_ref[...]))
    ri = _iota((L, L), 0)
    ci = _iota((L, L), 1)
    causal = ri >= ci
    tril = jnp.where(causal, 1.0, 0.0).astype(BF16)
    triu = jnp.where(ri <= ci, 1.0, 0.0).astype(BF16)
    ac_c = _dot_sel(tril, adt_c)
    ac_r = _dot_sel_r(adt_r, triu)
    ex = exp_ref[...]
    dt_full = _dot_sel_r(dt_c, ex)
    ac_full = _dot_sel_r(ac_c, ex)
    X = xs * dt_full
    a_last = ac_full[L - 1:L]
    dec_out = jnp.exp(ac_full)
    Xd = X * jnp.exp(a_last - ac_full)
    e_last = jnp.exp(a_last)
    neg = jnp.float32(-jnp.inf)
    lane_g = _idiv(_iota((1, GW), 1), MB_HEAD)
    zz = z_ref[...]

    for g in range(MB_G):
        gs = slice(g * GW, (g + 1) * GW)
        Bg = _bf(Bm[:, g * MB_N:(g + 1) * MB_N])
        Cg = _bf(Cm[:, g * MB_N:(g + 1) * MB_N])
        CB = _dot_nt(Cg, Bg)
        Xg = _bf(X[:, gs])
        Sg = S_ref[g]
        y = _dot(Cg, _bf(Sg)) * dec_out[:, gs]
        for e in range(E):
            h = g * E + e
            Lm = jnp.exp(jnp.where(causal, ac_c[:, h:h + 1] - ac_r[h:h + 1, :], neg))
            yd = _dot(_bf(CB * Lm), Xg)
            y = y + jnp.where(lane_g == e, yd, 0.0)
        S_ref[g] = Sg * e_last[:, gs] + _dot_tn(Bg, _bf(Xd[:, gs]))
        y = y + xs[:, gs] * dsk_ref[:, gs]
        y = y * _silu(zz[:, gs])
        ms = jnp.mean(y * y, axis=-1, keepdims=True)
        y_ref[:, gs] = _bf(y * lax.rsqrt(ms + NORM_EPS) * nw_ref[:, gs])


def _mamba(u, dt_row, offs, B, T, prm):
    L = MB_L
    nc = T // L
    GN = MB_G * MB_N

    def col(off, w):
        return pl.BlockSpec((L, w), lambda b, c: (b * nc + c, off // w))

    def par(r, w):
        return pl.BlockSpec((r, w), lambda b, c: (0, 0))

    in_specs = [col(offs["mb_z"], MB_W), col(offs["mb_x"], MB_W), col(offs["mb_b"], GN), col(offs["mb_c"], GN),
                col(offs["mb_dt"], LANES),
                pl.BlockSpec((MB_H, L), lambda b, c: (0, b * nc + c)),
                par(CONV_K, MB_W), par(1, MB_W), par(CONV_K, GN), par(1, GN), par(CONV_K, GN), par(1, GN),
                par(1, LANES), par(MB_H, 1), par(1, LANES), par(MB_H, 1), par(1, MB_W), par(1, MB_W),
                par(LANES, MB_W)]
    return pl.pallas_call(
        _mamba_kernel,
        grid=(B, nc),
        in_specs=in_specs,
        out_specs=pl.BlockSpec((L, MB_W), lambda b, c: (b * nc + c, 0)),
        out_shape=jax.ShapeDtypeStruct((B * T, MB_W), BF16),
        scratch_shapes=[pltpu.VMEM((MB_G, MB_N, MB_W // MB_G), F32),
                        pltpu.VMEM((SUBLANES, MB_W), F32),
                        pltpu.VMEM((SUBLANES, GN), F32),
                        pltpu.VMEM((SUBLANES, GN), F32)],
        compiler_params=_cparams(("parallel", "arbitrary")),
        name="mamba2",
    )(u, u, u, u, u, dt_row, *prm)


def _layout(segs, n_total):
    offs, cur = {}, 0
    for name, _, _, pw in segs:
        offs[name] = cur
        cur += pw
    assert cur <= n_total

    def pack(w, dtype=BF16):
        cols = []
        for _, s, wd, pw in segs:
            cols.append(w[:, s:s + wd])
            if pw > wd:
                cols.append(jnp.zeros((w.shape[0], pw - wd), w.dtype))
        if n_total > cur:
            cols.append(jnp.zeros((w.shape[0], n_total - cur), w.dtype))
        return jnp.concatenate(cols, axis=1).astype(dtype)

    return offs, pack


_EV_SEGS = [("rw_r", 0, RW_W, RW_W), ("rw_k", RW_W, RW_W, RW_W), ("rw_v", 2 * RW_W, RW_W, RW_W),
            ("ml_qk", RW_IN, 2 * ML_QK, 2 * ML_QK), ("ml_v", RW_IN + 2 * ML_QK, ML_W, ML_W),
            ("ml_o", RW_IN + 2 * ML_QK + ML_W, ML_W, ML_W),
            ("rw_dg", 3 * RW_W + RW_DECAY_LORA + RW_ICLR_LORA, RW_GATE_LORA, RW_GATE_LORA),
            ("rw_dw", 3 * RW_W, RW_DECAY_LORA, LANES),
            ("rw_da", 3 * RW_W + RW_DECAY_LORA, RW_ICLR_LORA, LANES),
            ("ml_if", RW_IN + 2 * ML_QK + 2 * ML_W, 2 * ML_H, LANES)]
_EV_N = 7168
_OD_SEGS = [("hg_q", 0, HG_W, HG_W), ("hg_f", HG_W, HG_W, HG_W), ("hg_i", 2 * HG_W, HG_W, HG_W),
            ("hg_g", 3 * HG_W, HG_W, HG_W),
            ("mb_z", HG_IN, MB_W, MB_W), ("mb_x", HG_IN + MB_W, MB_W, MB_W),
            ("mb_b", HG_IN + 2 * MB_W, MB_G * MB_N, MB_G * MB_N),
            ("mb_c", HG_IN + 2 * MB_W + MB_G * MB_N, MB_G * MB_N, MB_G * MB_N),
            ("mb_dt", HG_IN + MB_W + MB_CONV_W, MB_H, LANES)]
_OD_N = 7680


def _row(v, width=None):
    v = v.reshape(1, -1).astype(F32)
    if width is not None and v.shape[1] < width:
        v = jnp.pad(v, ((0, 0), (0, width - v.shape[1])))
    return v


def _pad_rows(w, rows):
    return jnp.pad(w, ((0, rows - w.shape[0]), (0, 0)))


def _even_layer(x, B, T, p):
    offs, pack = _layout(_EV_SEGS, _EV_N)
    s_if = RW_IN + 2 * ML_QK + 2 * ML_W
    u, g_row = _norm_matmul(x, _row(p["norm1"]), pack(p["w_in"]), _bf(p["w_in"][:, s_if:s_if + 2 * ML_H].T))
    mu = p["rw_mu"]
    o_dw, o_da, o_dg = 3 * RW_W, 3 * RW_W + RW_DECAY_LORA, 3 * RW_W + RW_DECAY_LORA + RW_ICLR_LORA
    rw_prm = [_row(mu[0:RW_W]), _row(mu[RW_W:2 * RW_W]), _row(mu[2 * RW_W:3 * RW_W]),
              _row(mu[o_dw:o_da], LANES), _row(mu[o_da:o_dg], LANES), _row(mu[o_dg:]),
              _row(p["rw_w0"]), _bf(_pad_rows(p["rw_w2"], LANES)),
              _row(p["rw_a0"]), _bf(_pad_rows(p["rw_a2"], LANES)), _bf(p["rw_g2"]),
              _row(p["rw_k_k"]), _row(p["rw_k_a"]), _row(p["rw_r_k"]), _row(p["rw_ln_w"]), _row(p["rw_ln_b"])]
    y_a = _rwkv(u, offs, B, T, rw_prm)
    gb =jnp.concatenate([p["ml_i_b"], p["ml_f_b"]]).astype(F32)
    ml_prm = [p["ml_conv_w"].astype(F32), _row(p["ml_conv_b"]), _row(gb, LANES), gb.reshape(-1, 1),
              _row(p["ml_norm_w"])]
    y_b = _mlstm(u, g_row, offs, B, T, ml_prm)
    x = _matmul_res(y_a, y_b, _bf(p["w_out"]), x)
    return _ffn(x, _row(p["norm2"]), _bf(p["ffn_w_gate"]), _bf(p["ffn_w_up"]), _bf(p["ffn_w_down"]))


def _odd_layer(x, B, T, p, final_w):
    offs, pack = _layout(_OD_SEGS, _OD_N)
    s_dt = HG_IN + MB_W + MB_CONV_W
    u, dt_row = _norm_matmul(x, _row(p["norm1"]), pack(p["w_in"]), _bf(p["w_in"][:, s_dt:s_dt + MB_H].T))
    hg_prm = [p["hg_lb_logits"].astype(F32), _hgrn_level_matrices(), _hgrn_masks(), _row(p["hg_norm_w"])]
    y_c = _hgrn(u, offs, B, T, hg_prm)
    cw, cb = p["mb_conv_w"].astype(F32), p["mb_conv_b"].astype(F32)
    GN = MB_G * MB_N
    expand = (jnp.arange(LANES)[:, None] == (jnp.arange(MB_W)[None, :] // MB_HEAD)).astype(BF16)
    mb_prm = [cw[:, :MB_W], _row(cb[:MB_W]), cw[:, MB_W:MB_W + GN], _row(cb[MB_W:MB_W + GN]),
              cw[:, MB_W + GN:], _row(cb[MB_W + GN:]),
              _row(p["mb_dt_bias"], LANES), p["mb_dt_bias"].astype(F32).reshape(-1, 1),
              _row(p["mb_A_log"], LANES), p["mb_A_log"].astype(F32).reshape(-1, 1),
              _row(jnp.repeat(p["mb_D"], MB_HEAD)), _row(p["mb_norm_w"]), expand]
    y_d = _mamba(u, dt_row, offs, B, T, mb_prm)
    x = _matmul_res(y_c, y_d, _bf(p["w_out"]), x)
    wr = jnp.pad(p["moe_router"].astype(F32), ((0, 0), (0, LANES - N_EXPERTS)))
    idx, prob, h = _router(x, _row(p["norm2"]), wr, 2 * x.shape[0] + N_EXPERTS * TM_MOE)
    pos, src_tok, tile_expert, tile_active = _route_plan(idx[:, :2], TM_MOE)
    xg = jnp.take(h, src_tok, axis=0, mode="clip")
    act = _moe_up(xg, tile_expert, tile_active, p["moe_w_gate"].astype(F32), p["moe_w_up"].astype(F32))
    yg = _moe_down(act, tile_expert, tile_active, p["moe_w_down"].astype(F32))
    y0 = jnp.take(yg, pos[:, 0], axis=0, mode="clip")
    y1 = jnp.take(yg, pos[:, 1], axis=0, mode="clip")
    return _combine(x, y0, y1, prob, _row(final_w))


def kernel(x, final_norm_w, hg_lb_logits, ev_norm1_w, ev_w_in, ev_w_out, rw_mu, rw_w0, rw_w2, rw_a0, rw_a2, rw_g2, rw_k_k, rw_k_a, rw_r_k, rw_ln_w, rw_ln_b, ml_conv_w, ml_conv_b, ml_i_b, ml_f_b, ml_norm_w, ev_norm2_w, ffn_w_gate, ffn_w_up, ffn_w_down, od_norm1_w, od_w_in, od_w_out, hg_norm_w, mb_conv_w, mb_conv_b, mb_dt_bias, mb_A_log, mb_D, mb_norm_w, od_norm2_w, moe_router, moe_w_gate, moe_w_up, moe_w_down):
    B, T, D = x.shape
    xf = x.reshape(B * T, D)
    ev = dict(norm1=ev_norm1_w[0], w_in=ev_w_in[0], w_out=ev_w_out[0], rw_mu=rw_mu[0], rw_w0=rw_w0[0],
              rw_w2=rw_w2[0], rw_a0=rw_a0[0], rw_a2=rw_a2[0], rw_g2=rw_g2[0], rw_k_k=rw_k_k[0],
              rw_k_a=rw_k_a[0], rw_r_k=rw_r_k[0], rw_ln_w=rw_ln_w[0], rw_ln_b=rw_ln_b[0],
              ml_conv_w=ml_conv_w[0], ml_conv_b=ml_conv_b[0], ml_i_b=ml_i_b[0], ml_f_b=ml_f_b[0],
              ml_norm_w=ml_norm_w[0], norm2=ev_norm2_w[0], ffn_w_gate=ffn_w_gate[0], ffn_w_up=ffn_w_up[0],
              ffn_w_down=ffn_w_down[0])
    od = dict(norm1=od_norm1_w[0], w_in=od_w_in[0], w_out=od_w_out[0], hg_lb_logits=hg_lb_logits,
              hg_norm_w=hg_norm_w[0], mb_conv_w=mb_conv_w[0], mb_conv_b=mb_conv_b[0], mb_dt_bias=mb_dt_bias[0],
              mb_A_log=mb_A_log[0], mb_D=mb_D[0], mb_norm_w=mb_norm_w[0], norm2=od_norm2_w[0],
              moe_router=moe_router[0], moe_w_gate=moe_w_gate[0], moe_w_up=moe_w_up[0],
              moe_w_down=moe_w_down[0])
    xf = _even_layer(xf, B, T, ev)
    xf = _odd_layer(xf, B, T, od, final_norm_w)
    return xf.reshape(B, T, D)
```

```python
import functools
import math

import jax
import jax.numpy as jnp
from jax import lax
from jax.experimental import pallas as pl
from jax.experimental.pallas import tpu as pltpu

F32 = jnp.float32
BF16 = jnp.bfloat16

D_MODEL = 2048
NORM_EPS = 1e-6
RW_HEAD = 64
RW_W = 1024
RW_H = RW_W // RW_HEAD
RW_DECAY_LORA = 96
RW_ICLR_LORA = 96
RW_GATE_LORA = 256
RW_IN = 3 * RW_W + RW_DECAY_LORA + RW_ICLR_LORA + RW_GATE_LORA
RW_LN_EPS = 64e-5
RW_DECAY_SCALE = math.exp(-0.5)
ML_W = 1024
ML_H = 4
ML_DV = ML_W // ML_H
ML_DK = ML_DV // 2
ML_QK = ML_H * ML_DK
ML_IN = 2 * ML_QK + 2 * ML_W + 2 * ML_H
HG_W = 1024
HG_HEAD = 128
HG_H = HG_W // HG_HEAD
HG_IN = 4 * HG_W
MB_W = 1024
MB_HEAD = 64
MB_H = MB_W // MB_HEAD
MB_G = 4
MB_N = 128
MB_CONV_W = MB_W + 2 * MB_G * MB_N
N_EXPERTS = 8

LANES = 128
SUBLANES = 8
VMEM_LIMIT = 56 * 1024 * 1024

TM_PROJ = 1024
TN_PROJ = (1024, 768, 512)
TM_DENSE = 1024
TM_FFN = 512
TF_FFN = 512
TM_MOE = 256
TF_MOE = 1408
TN_MOE = 1024
RW_L = 64
RW_BLK = 16
ML_L = 128
HG_L = 64
MB_L = 128
CONV_K = 4


def _cparams(sem):
    return pltpu.CompilerParams(dimension_semantics=sem, vmem_limit_bytes=VMEM_LIMIT)


def _proj_tn(n):
    return next(t for t in TN_PROJ if n % t == 0)


def _dot(a, b):
    return jnp.dot(a, b, preferred_element_type=F32)


def _dot_nt(a, b):
    return lax.dot_general(a, b, (((1,), (1,)), ((), ())), preferred_element_type=F32)


def _dot_tn(a, b):
    return lax.dot_general(a, b, (((0,), (0,)), ((), ())), preferred_element_type=F32)


def _bf(x):
    return x.astype(BF16)


def _split2(x):
    h = x.astype(BF16)
    l = (x - h.astype(F32)).astype(BF16)
    return h, l


def _split3(x):
    h = x.astype(BF16)
    r = x - h.astype(F32)
    m = r.astype(BF16)
    l = (r - m.astype(F32)).astype(BF16)
    return h, m, l


def _dot_sel(sel_bf16, x):
    h, m, l = _split3(x)
    return _dot(sel_bf16, h) + _dot(sel_bf16, m) + _dot(sel_bf16, l)


def _dot_sel_r(x, sel_bf16):
    h, m, l = _split3(x)
    return _dot(h, sel_bf16) + _dot(m, sel_bf16) + _dot(l, sel_bf16)


def _dot3(a, b):
    ah, al = _split2(a)
    bh, bl = _split2(b)
    return _dot(ah, bh) + _dot(ah, bl) + _dot(al, bh)


def _sigmoid(x):
    return 1.0 / (1.0 + jnp.exp(-x))


def _silu(x):
    return x * _sigmoid(x)


def _log_sigmoid(x):
    return -(jnp.maximum(-x, 0.0) + jnp.log1p(jnp.exp(-jnp.abs(x))))


def _softplus(x):
    return jnp.maximum(x, 0.0) + jnp.log1p(jnp.exp(-jnp.abs(x)))


def _iota(shape, dim):
    return lax.broadcasted_iota(jnp.int32, shape, dim)


def _idiv(x, d):
    sh = d.bit_length() - 1
    assert d == 1 << sh
    return lax.shift_right_logical(x, jnp.int32(sh))


def _shift_rows(tail, x, j):
    xc = jnp.concatenate([tail, x], axis=0)
    return pltpu.roll(xc, j, 0)[SUBLANES:]


def _norm_matmul_kernel(x_ref, nw_ref, w_ref, wt_ref, o_ref, ot_ref, h_ref):
    @pl.when(pl.program_id(1) == 0)
    def _():
        x = x_ref[...]
        ms = jnp.mean(x * x, axis=-1, keepdims=True)
        h_ref[...] = _bf(x * lax.rsqrt(ms + NORM_EPS) * nw_ref[...])
        ot_ref[...] = _dot_nt(wt_ref[...], h_ref[...])

    o_ref[...] = _dot(h_ref[...], w_ref[...])


def _norm_matmul(x, nw, w, wt):
    m, d = x.shape
    n = w.shape[1]
    r = wt.shape[0]
    tn = _proj_tn(n)
    return pl.pallas_call(
        _norm_matmul_kernel,
        grid=(m // TM_PROJ, n // tn),
        in_specs=[pl.BlockSpec((TM_PROJ, d), lambda i, j: (i, 0)),
                  pl.BlockSpec((1, d), lambda i, j: (0, 0)),
                  pl.BlockSpec((d, tn), lambda i, j: (0, j)),
                  pl.BlockSpec((r, d), lambda i, j: (0, 0))],
        out_specs=[pl.BlockSpec((TM_PROJ, tn), lambda i, j: (i, j)),
                   pl.BlockSpec((r, TM_PROJ), lambda i, j: (0, i))],
        out_shape=[jax.ShapeDtypeStruct((m, n), F32),
                   jax.ShapeDtypeStruct((r, m), F32)],
        scratch_shapes=[pltpu.VMEM((TM_PROJ, d), BF16)],
        compiler_params=_cparams(("parallel", "arbitrary")),
        name="norm_matmul",
    )(x, nw, w, wt)


def _matmul_res_kernel(ya_ref, yb_ref, wa_ref, wb_ref, r_ref, o_ref):
    o_ref[...] = r_ref[...] + _dot(ya_ref[...], wa_ref[...]) + _dot(yb_ref[...], wb_ref[...])


def _matmul_res(ya, yb, w, res):
    m, k = ya.shape
    n = w.shape[1]
    tn = _proj_tn(n)
    return pl.pallas_call(
        _matmul_res_kernel,
        grid=(m // TM_PROJ, n // tn),
        in_specs=[pl.BlockSpec((TM_PROJ, k), lambda i, j: (i, 0)),
                  pl.BlockSpec((TM_PROJ, k), lambda i, j: (i, 0)),
                  pl.BlockSpec((k, tn), lambda i, j: (0, j)),
                  pl.BlockSpec((k, tn), lambda i, j: (1, j)),
                  pl.BlockSpec((TM_PROJ, tn), lambda i, j: (i, j))],
        out_specs=pl.BlockSpec((TM_PROJ, tn), lambda i, j: (i, j)),
        out_shape=jax.ShapeDtypeStruct((m, n), F32),
        compiler_params=_cparams(("parallel", "arbitrary")),
        name="matmul_res",
    )(ya, yb, w, w, res)


def _ffn_kernel(x_ref, nw_ref, wg_ref, wu_ref, wd_ref, o_ref, h_ref):
    f = pl.program_id(1)

    @pl.when(f == 0)
    def _():
        x = x_ref[...]
        ms = jnp.mean(x * x, axis=-1, keepdims=True)
        h_ref[...] = _bf(x * lax.rsqrt(ms + NORM_EPS) * nw_ref[...])
        o_ref[...] = x

    h = h_ref[...]
    act = _silu(_dot(h, wg_ref[...])) * _dot(h, wu_ref[...])
    o_ref[...] += _dot(_bf(act), wd_ref[...])


def _ffn(x, nw, wg, wu, wd):
    m, d = x.shape
    f = wg.shape[1]
    return pl.pallas_call(
        _ffn_kernel,
        grid=(m // TM_DENSE, f // TF_FFN),
        in_specs=[pl.BlockSpec((TM_DENSE, d), lambda i, j: (i, 0)),
                  pl.BlockSpec((1, d), lambda i, j: (0, 0)),
                  pl.BlockSpec((d, TF_FFN), lambda i, j: (0, j)),
                  pl.BlockSpec((d, TF_FFN), lambda i, j: (0, j)),
                  pl.BlockSpec((TF_FFN, d), lambda i, j: (j, 0))],
        out_specs=pl.BlockSpec((TM_DENSE, d), lambda i, j: (i, 0)),
        out_shape=jax.ShapeDtypeStruct((m, d), F32),
        scratch_shapes=[pltpu.VMEM((TM_DENSE, d), BF16)],
        compiler_params=_cparams(("parallel", "arbitrary")),
        name="ffn_swiglu",
    )(x, nw, wg, wu, wd)


def _router_kernel(x_ref, nw_ref, wr_ref, i_ref, p_ref, h_ref, *, n_blocks):
    step = pl.program_id(0)

    @pl.when(step < n_blocks)
    def _():
        _router_block(x_ref, nw_ref, wr_ref, i_ref, p_ref, h_ref)

    @pl.when(step >= n_blocks)
    def _():
        h_ref[...] = jnp.zeros_like(h_ref)


def _router_block(x_ref, nw_ref, wr_ref, i_ref, p_ref, h_ref):
    x = x_ref[...]
    ms = jnp.mean(x * x, axis=-1, keepdims=True)
    h = x * lax.rsqrt(ms + NORM_EPS) * nw_ref[...]
    h_ref[...] = _bf(h)
    wr = wr_ref[...]
    hh, hl = _split2(h)
    wh, wl = _split2(wr)
    logits = _dot(hh, wh) + _dot(hh, wl) + _dot(hl, wh)
    lane = _iota(logits.shape, 1)
    neg = jnp.float32(-jnp.inf)
    logits = jnp.where(lane < N_EXPERTS, logits, neg)
    v1 = jnp.max(logits, axis=-1, keepdims=True)
    i1 = jnp.min(jnp.where(logits == v1, lane, LANES), axis=-1, keepdims=True)
    rest = jnp.where(lane == i1, neg, logits)
    v2 = jnp.max(rest, axis=-1, keepdims=True)
    i2 = jnp.min(jnp.where(rest == v2, lane, LANES), axis=-1, keepdims=True)
    e2 = jnp.exp(v2 - v1)
    p1 = 1.0 / (1.0 + e2)
    p2 = e2 / (1.0 + e2)
    i_ref[...] = jnp.where(lane == 0, i1, jnp.where(lane == 1, i2, 0))
    p_ref[...] = jnp.where(lane == 0, p1, jnp.where(lane == 1, p2, 0.0))


def _router(x, nw, wr, table_rows):
    m, d = x.shape
    nb = m // TM_FFN
    tok = lambda i: (jnp.minimum(i, nb - 1), 0)
    return pl.pallas_call(
        functools.partial(_router_kernel, n_blocks=nb),
        grid=(table_rows // TM_FFN,),
        in_specs=[pl.BlockSpec((TM_FFN, d), tok),
                  pl.BlockSpec((1, d), lambda i: (0, 0)),
                  pl.BlockSpec((d, LANES), lambda i: (0, 0))],
        out_specs=[pl.BlockSpec((TM_FFN, LANES), tok),
                   pl.BlockSpec((TM_FFN, LANES), tok),
                   pl.BlockSpec((TM_FFN, d), lambda i: (i, 0))],
        out_shape=[jax.ShapeDtypeStruct((m, LANES), jnp.int32),
                   jax.ShapeDtypeStruct((m, LANES), F32),
                   jax.ShapeDtypeStruct((table_rows, d), BF16)],
        compiler_params=_cparams(("arbitrary",)),
        name="moe_router",
    )(x, nw, wr)


def _route_plan(top_idx, tm):
    m = top_idx.shape[0]
    n_rows = 2 * m + N_EXPERTS * tm
    e_flat = top_idx.reshape(-1)
    onehot = (e_flat[:, None] == jnp.arange(N_EXPERTS, dtype=jnp.int32)[None, :]).astype(jnp.int32)
    rank = jnp.cumsum(onehot, axis=0) - onehot
    counts = jnp.sum(onehot, axis=0)
    padded = ((counts + tm - 1) // tm) * tm
    ends = jnp.cumsum(padded)
    off = ends - padded
    pos = jnp.sum(onehot * (off[None, :] + rank), axis=1)
    src_tok = jnp.zeros((n_rows,), jnp.int32).at[pos].set(jnp.arange(2 * m, dtype=jnp.int32) // 2,
                                                          unique_indices=True)
    tile_start = jnp.arange(n_rows // tm, dtype=jnp.int32) * tm
    tile_expert = jnp.minimum(jnp.sum((tile_start[:, None] >= ends[None, :]).astype(jnp.int32), axis=1),
                              N_EXPERTS - 1)
    tile_active = (tile_start < ends[-1]).astype(jnp.int32)
    return pos.reshape(m, 2), src_tok, tile_expert, tile_active


def _new_weight_block(te_ref, i):
    return (i == 0) | (te_ref[i] != te_ref[jnp.maximum(i - 1, 0)])


def _moe_up_kernel(te_ref, ta_ref, x_ref, wg_ref, wu_ref, a_ref, wgb_ref, wub_ref):
    i = pl.program_id(1)

    @pl.when(_new_weight_block(te_ref, i))
    def _():
        wgb_ref[...] = _bf(wg_ref[0])
        wub_ref[...] = _bf(wu_ref[0])

    @pl.when(ta_ref[i] == 1)
    def _():
        x = x_ref[...]
        act = _silu(_dot(x, wgb_ref[...])) * _dot(x, wub_ref[...])
        a_ref[...] = _bf(act)

    @pl.when(ta_ref[i] == 0)
    def _():
        a_ref[...] = jnp.zeros_like(a_ref)


def _moe_up(xg, tile_expert, tile_active, wg, wu):
    n_rows, d = xg.shape
    fe = wg.shape[2]
    wmap = lambda f, i, te, ta: (te[i], 0, f)
    grid_spec = pltpu.PrefetchScalarGridSpec(
        num_scalar_prefetch=2,
        grid=(fe // TF_MOE, n_rows // TM_MOE),
        in_specs=[pl.BlockSpec((TM_MOE, d), lambda f, i, te, ta: (i, 0)),
                  pl.BlockSpec((1, d, TF_MOE), wmap),
                  pl.BlockSpec((1, d, TF_MOE), wmap, pipeline_mode=pl.Buffered(1))],
        out_specs=pl.BlockSpec((TM_MOE, TF_MOE), lambda f, i, te, ta: (i, f)),
        scratch_shapes=[pltpu.VMEM((d, TF_MOE), BF16), pltpu.VMEM((d, TF_MOE), BF16)],
    )
    return pl.pallas_call(
        _moe_up_kernel,
        grid_spec=grid_spec,
        out_shape=jax.ShapeDtypeStruct((n_rows, fe), BF16),
        compiler_params=_cparams(("arbitrary", "arbitrary")),
        name="moe_up",
    )(tile_expert, tile_active, xg, wg, wu)


def _moe_down_kernel(te_ref, ta_ref, a_ref, wd_ref, y_ref, wdb_ref):
    i = pl.program_id(1)

    @pl.when(_new_weight_block(te_ref, i))
    def _():
        wdb_ref[...] = _bf(wd_ref[0])

    @pl.when(ta_ref[i] == 1)
    def _():
        y_ref[...] = _bf(_dot(a_ref[...], wdb_ref[...]))

    @pl.when(ta_ref[i] == 0)
    def _():
        y_ref[...] = jnp.zeros_like(y_ref)


def _moe_down(act, tile_expert, tile_active, wd):
    n_rows, fe = act.shape
    d = wd.shape[2]
    grid_spec = pltpu.PrefetchScalarGridSpec(
        num_scalar_prefetch=2,
        grid=(d // TN_MOE, n_rows // TM_MOE),
        in_specs=[pl.BlockSpec((TM_MOE, fe), lambda n, i, te, ta: (i, 0)),
                  pl.BlockSpec((1, fe, TN_MOE), lambda n, i, te, ta: (te[i], 0, n))],
        out_specs=pl.BlockSpec((TM_MOE, TN_MOE), lambda n, i, te, ta: (i, n)),
        scratch_shapes=[pltpu.VMEM((fe, TN_MOE), BF16)],
    )
    return pl.pallas_call(
        _moe_down_kernel,
        grid_spec=grid_spec,
        out_shape=jax.ShapeDtypeStruct((n_rows, d), BF16),
        compiler_params=_cparams(("arbitrary", "arbitrary")),
        name="moe_down",
    )(tile_expert, tile_active, act, wd)


def _combine_kernel(x_ref, y0_ref, y1_ref, p_ref, fw_ref, o_ref):
    p = p_ref[...]
    y = x_ref[...] + p[:, 0:1] * y0_ref[...].astype(F32) + p[:, 1:2] * y1_ref[...].astype(F32)
    ms = jnp.mean(y * y, axis=-1, keepdims=True)
    o_ref[...] = y * lax.rsqrt(ms + NORM_EPS) * fw_ref[...]


def _combine(x, y0, y1, prob, final_w):
    m, d = x.shape
    row = pl.BlockSpec((TM_FFN, d), lambda i: (i, 0))
    return pl.pallas_call(
        _combine_kernel,
        grid=(m // TM_FFN,),
        in_specs=[row, row, row, pl.BlockSpec((TM_FFN, LANES), lambda i: (i, 0)),
                  pl.BlockSpec((1, d), lambda i: (0, 0))],
        out_specs=row,
        out_shape=jax.ShapeDtypeStruct((m, d), F32),
        compiler_params=_cparams(("parallel",)),
        name="moe_combine",
    )(x, y0, y1, prob, final_w)


def _rwkv_kernel(r_ref, k_ref, v_ref, dw_ref, da_ref, dg_ref,
                 mur_ref, muk_ref, muv_ref, mudw_ref, muda_ref, mudg_ref,
                 w0_ref, w2_ref, a0_ref, a2_ref, g2_ref, kk_ref, ka_ref, rk_ref, lnw_ref, lnb_ref,
                 o_ref,
                 H_ref, tr_ref, tk_ref, tv_ref, tdw_ref, tda_ref, tdg_ref):
    L = RW_L
    L2 = 2 * L

    @pl.when(pl.program_id(1) == 0)
    def _():
        H_ref[...] = jnp.zeros_like(H_ref)
        for t in (tr_ref, tk_ref, tv_ref, tdw_ref, tda_ref, tdg_ref):
            t[...] = jnp.zeros_like(t)

    def shift_mix(ref, tail, mu):
        x = ref[...]
        prev = _shift_rows(tail[...], x, 1)
        tail[...] = x[L - SUBLANES:]
        return x + (prev - x) * mu[...]

    r = shift_mix(r_ref, tr_ref, mur_ref)
    k = shift_mix(k_ref, tk_ref, muk_ref)
    v = shift_mix(v_ref, tv_ref, muv_ref)
    dw = shift_mix(dw_ref, tdw_ref, mudw_ref)
    da = shift_mix(da_ref, tda_ref, muda_ref)
    dg = shift_mix(dg_ref, tdg_ref, mudg_ref)

    log_w = -RW_DECAY_SCALE * _sigmoid(w0_ref[...] + _dot(_bf(jnp.tanh(dw)), w2_ref[...]))
    a = _sigmoid(a0_ref[...] + _dot(_bf(da), a2_ref[...]))
    g = _dot(_bf(_sigmoid(dg)), g2_ref[...])

    ri = _iota((LANES, LANES), 0)
    ci = _iota((LANES, LANES), 1)
    hsum = jnp.where(_idiv(ri, RW_HEAD) == _idiv(ci, RW_HEAD), 1.0, 0.0).astype(BF16)
    same_head = _idiv(ri, L) == _idiv(ci, L)
    tril_s = jnp.where(same_head & (ri > ci), 1.0, 0.0)
    tril_i = jnp.where(same_head & (ri >= ci), 1.0, 0.0)
    blk_d = jnp.where(_idiv(ri, RW_BLK) == _idiv(ci, RW_BLK), 1.0, 0.0)
    eye = jnp.where(ri == ci, 1.0, 0.0)
    cum = jnp.where(_iota((L, L), 0) >= _iota((L, L), 1), 1.0, 0.0).astype(BF16)
    lane = _iota((1, LANES), 1)
    m0 = lane < RW_HEAD

    def head_sum(x):
        h, l = _split2(x)
        return _dot(h, hsum) + _dot(l, hsum)

    kk = k * kk_ref[...]
    k2 = k * (1.0 + (a - 1.0) * ka_ref[...])
    cw = _dot_sel(cum, log_w)
    e_prev = jnp.exp(cw - log_w)
    e_inv = jnp.exp(-cw)
    e_cw = jnp.exp(cw)
    wl = cw[L - 1:L]
    e_end = jnp.exp(wl - cw)
    e_wl = jnp.exp(wl)
    rk2 = r * k2 * rk_ref[...]

    def stack(x):
        x0 = jnp.where(m0, x, 0.0)
        return jnp.concatenate([x0, x - x0], axis=0)

    def mm3(xs, ys):
        return _dot(xs[0], ys[0]) + _dot(xs[0], ys[1]) + _dot(xs[1], ys[0])

    NP = RW_H // 2
    sls = [slice(p * LANES, (p + 1) * LANES) for p in range(NP)]

    def each(fn, *lists):
        return [fn(*args) for args in zip(*lists)]

    kk_l = [kk[:, sl] for sl in sls]
    kk_l = each(lambda x: x * lax.rsqrt(jnp.maximum(head_sum(x * x), 1e-12)), kk_l)
    kka_l = [x * a[:, sl] for x, sl in zip(kk_l, sls)]
    At = [_bf(stack(-x * e_prev[:, sl])) for x, sl in zip(kk_l, sls)]
    Bt = [_bf(stack(x * e_inv[:, sl])) for x, sl in zip(kka_l, sls)]
    Kt = [_bf(stack(k2[:, sl] * e_inv[:, sl])) for sl in sls]
    Rt_f = [stack(r[:, sl] * e_cw[:, sl]) for sl in sls]
    Rt = each(_bf, Rt_f)
    Bh = [_bf(stack(x * e_end[:, sl])) for x, sl in zip(kka_l, sls)]
    Kh = [_bf(stack(k2[:, sl] * e_end[:, sl])) for sl in sls]
    Vs = [_bf(stack(v[:, sl])) for sl in sls]

    Mab = each(lambda x, y: _dot_nt(x, y) * tril_s, At, Bt)
    Mak = each(lambda x, y: _bf(_dot_nt(x, y) * tril_s), At, Kt)
    Arb = each(lambda x, y: _bf(_dot_nt(x, y) * tril_i), Rt, Bt)
    Ark = each(lambda x, y: _bf(_dot_nt(x, y) * tril_i), Rt, Kt)

    MD = each(lambda m: m * blk_d, Mab)
    Nn = each(lambda m, d: _split2(m - d), Mab, MD)
    P = each(lambda d: eye + d, MD)
    MDs = each(_split2, MD)
    S = each(mm3, MDs, MDs)
    for it in range(3):
        Ss = each(_split2, S)
        P = each(lambda p_, s_: p_ + mm3(_split2(p_), s_), P, Ss)
        if it < 2:
            S = each(mm3, Ss, Ss)
    Ps = each(_split2, P)
    X = each(mm3, Ps, Nn)
    Xs = each(_split2, X)
    X2 = each(lambda x: _split2(mm3(x, x)), Xs)
    Y = each(lambda x: eye + x, X)
    Y = each(lambda y_, x2: y_ + mm3(_split2(y_), x2), Y, X2)
    T = each(lambda y_, p_: _split2(mm3(_split2(y_), p_)), Y, Ps)

    P1 = each(lambda t, x: _bf(_dot(t[0], x) + _dot(t[1], x)), T, At)
    MV = each(lambda m, x: _split2(_dot(m, x)), Mak, Vs)
    P2 = each(lambda t, x: _bf(mm3(t, x)), T, MV)
    G = [eye * e_wl[:, sl] + _dot_tn(b, p1) for sl, b, p1 in zip(sls, Bh, P1)]
    J = each(lambda b, p2, kh, vs: _dot_tn(b, p2) + _dot_tn(kh, vs), Bh, P2, Kh, Vs)
    Q = each(lambda rf, ar, p1: rf + _dot(ar, p1), Rt_f, Arb, P1)
    Z = each(lambda ar, p2, ak, vs: _dot(ar, p2) + _dot(ak, vs), Arb, P2, Ark, Vs)

    Hs = [_split2(H_ref[p]) for p in range(NP)]
    Yst = each(lambda q, h, z: mm3(_split2(q), h) + z, Q, Hs, Z)
    Hn = each(lambda g_, h, j: mm3(_split2(g_), h) + j, G, Hs, J)
    for p in range(NP):
        H_ref[p] = Hn[p]
    y_l = each(lambda y_: y_[:L] + y_[L:], Yst)

    mean = each(lambda y_: head_sum(y_) * (1.0 / RW_HEAD), y_l)
    yc = each(lambda y_, m: y_ - m, y_l, mean)
    var = each(lambda c: head_sum(c * c) * (1.0 / RW_HEAD), yc)
    yn = [c * lax.rsqrt(vr + RW_LN_EPS) * lnw_ref[:, sl] + lnb_ref[:, sl] for c, vr, sl in zip(yc, var, sls)]
    bonus = [head_sum(rk2[:, sl]) * v[:, sl] for sl in sls]
    for p in range(NP):
        o_ref[:, sls[p]] = _bf((yn[p] + bonus[p]) * g[:, sls[p]])


def _rwkv(u, offs, B, T, prm):
    L = RW_L
    nc = T // L

    def col(off, w):
        return pl.BlockSpec((L, w), lambda b, c: (b * nc + c, off // w))

    def par(w, rows=1):
        return pl.BlockSpec((rows, w), lambda b, c: (0, 0))

    in_specs = [col(offs["rw_r"], RW_W), col(offs["rw_k"], RW_W), col(offs["rw_v"], RW_W),
                col(offs["rw_dw"], LANES), col(offs["rw_da"], LANES), col(offs["rw_dg"], RW_GATE_LORA),
                par(RW_W), par(RW_W), par(RW_W), par(LANES), par(LANES), par(RW_GATE_LORA),
                par(RW_W), par(RW_W, LANES), par(RW_W), par(RW_W, LANES), par(RW_W, RW_GATE_LORA),
                par(RW_W), par(RW_W), par(RW_W), par(RW_W), par(RW_W)]
    return pl.pallas_call(
        _rwkv_kernel,
        grid=(B, nc),
        in_specs=in_specs,
        out_specs=pl.BlockSpec((L, RW_W), lambda b, c: (b * nc + c, 0)),
        out_shape=jax.ShapeDtypeStruct((B * T, RW_W), BF16),
        scratch_shapes=[pltpu.VMEM((RW_H // 2, LANES, LANES), F32),
                        pltpu.VMEM((SUBLANES, RW_W), F32), pltpu.VMEM((SUBLANES, RW_W), F32),
                        pltpu.VMEM((SUBLANES, RW_W), F32), pltpu.VMEM((SUBLANES, LANES), F32),
                        pltpu.VMEM((SUBLANES, LANES), F32), pltpu.VMEM((SUBLANES, RW_GATE_LORA), F32)],
        compiler_params=_cparams(("parallel", "arbitrary")),
        name="rwkv7",
    )(u, u, u, u, u, u, *prm)


def _mlstm_kernel(qk_ref, v_ref, o_ref, gc_ref, gr_ref, cw_ref, cb_ref, gbr_ref, gbc_ref, nw_ref,
                  y_ref, C_ref, m_ref, tail_ref):
    L = ML_L
    DK, DV = ML_DK, ML_DV

    @pl.when(pl.program_id(1) == 0)
    def _():
        C_ref[...] = jnp.zeros_like(C_ref)
        m_ref[...] = jnp.zeros_like(m_ref)
        tail_ref[...] = jnp.zeros_like(tail_ref)

    x = qk_ref[...]
    tail = tail_ref[...]
    acc = x * cw_ref[CONV_K - 1:CONV_K] + cb_ref[...]
    for j in range(1, CONV_K):
        acc = acc + _shift_rows(tail, x, j) * cw_ref[CONV_K - 1 - j:CONV_K - j]
    tail_ref[...] = x[L - SUBLANES:]
    qk = _silu(acc)

    gc = gc_ref[...] + gbr_ref[...]
    gr = gr_ref[...] + gbc_ref[...]
    fl_c = _log_sigmoid(gc)
    fl_r = _log_sigmoid(gr)
    ri = _iota((L, L), 0)
    ci = _iota((L, L), 1)
    causal = ri >= ci
    tril = jnp.where(causal, 1.0, 0.0).astype(BF16)
    triu = jnp.where(ri <= ci, 1.0, 0.0).astype(BF16)
    b_c = _dot_sel(tril, fl_c)
    b_r = _dot_sel_r(fl_r, triu)
    one_col = jnp.where(_iota((L, LANES), 1) == 0, 1.0, 0.0)
    neg = jnp.float32(-jnp.inf)
    vv = v_ref[...]
    oo = o_ref[...]

    for h in range(ML_H):
        q_h = _bf(qk[:, h * DK:(h + 1) * DK] * (DK ** -0.5))
        k_f = qk[:, ML_QK + h * DK:ML_QK + (h + 1) * DK]
        k_h = _bf(k_f)
        v_ext = _bf(jnp.concatenate([vv[:, h * DV:(h + 1) * DV], one_col], axis=1))
        bc = b_c[:, ML_H + h:ML_H + h + 1]
        br = b_r[ML_H + h:ML_H + h + 1, :]
        il_c = gc[:, h:h + 1]
        il_r = gr[h:h + 1, :]
        m_prev = m_ref[h:h + 1, 0:1]
        C_prev = C_ref[h]

        D = jnp.where(causal, bc - br + il_r, neg)
        m_t = jnp.maximum(bc + m_prev, jnp.max(D, axis=-1, keepdims=True))
        S = _dot_nt(q_h, k_h) * jnp.exp(D - m_t)
        inter = jnp.exp(bc + m_prev - m_t)
        num = inter * _dot(q_h, _bf(C_prev)) + _dot(_bf(S), v_ext)
        den = num[:, DV:DV + 1]
        hh = num[:, :DV] / jnp.maximum(jnp.abs(den), jnp.exp(-m_t))
        ms = jnp.mean(hh * hh, axis=-1, keepdims=True)
        hn = hh * lax.rsqrt(ms + NORM_EPS) * nw_ref[:, h * DV:(h + 1) * DV]
        y_ref[:, h * DV:(h + 1) * DV] = _bf(hn * _sigmoid(oo[:, h * DV:(h + 1) * DV]))

        g_end = bc[L - 1:L]
        wst = g_end - bc + il_c
        m_new = jnp.maximum(g_end + m_prev, jnp.max(wst, axis=0, keepdims=True))
        kt = _bf(k_f * jnp.exp(wst - m_new))
        C_ref[h] = jnp.exp(g_end + m_prev - m_new) * C_prev + _dot_tn(kt, v_ext)
        m_ref[h:h + 1, :] = jnp.broadcast_to(m_new, (1, LANES))


def _mlstm(u, g_row, offs, B, T, prm):
    L = ML_L
    nc = T // L

    def col(off, w):
        return pl.BlockSpec((L, w), lambda b, c: (b * nc + c, off // w))

    def par(r, w):
        return pl.BlockSpec((r, w), lambda b, c: (0, 0))

    in_specs = [col(offs["ml_qk"], 2 * ML_QK), col(offs["ml_v"], ML_W), col(offs["ml_o"], ML_W),
                col(offs["ml_if"], LANES),
                pl.BlockSpec((SUBLANES, L), lambda b, c: (0, b * nc + c)),
                par(CONV_K, 2 * ML_QK), par(1, 2 * ML_QK), par(1, LANES), par(SUBLANES, 1), par(1, ML_W)]
    return pl.pallas_call(
        _mlstm_kernel,
        grid=(B, nc),
        in_specs=in_specs,
        out_specs=pl.BlockSpec((L, ML_W), lambda b, c: (b * nc + c, 0)),
        out_shape=jax.ShapeDtypeStruct((B * T, ML_W), BF16),
        scratch_shapes=[pltpu.VMEM((ML_H, ML_DK, ML_DV + LANES), F32),
                        pltpu.VMEM((SUBLANES, LANES), F32),
                        pltpu.VMEM((SUBLANES, 2 * ML_QK), F32)],
        compiler_params=_cparams(("parallel", "arbitrary")),
        name="mlstm",
    )(u, u, u, u, g_row, *prm)


def _hgrn_level_matrices():
    L = HG_L
    t = jnp.arange(L)[:, None]
    r = jnp.arange(L)[None, :]
    mats = [(r <= t)]
    for l in range(int(math.log2(L))):
        half = 1 << l
        base = (t // (2 * half)) * (2 * half)
        bnd = base + half - 1
        upper = (t - base) >= half
        m_up = upper & (r > bnd) & (r <= t)
        m_lo = (~upper) & (r > t) & (r <= bnd)
        mats.append(m_up | m_lo)
    return jnp.concatenate(mats, axis=0).astype(BF16)


def _hgrn_masks():
    L = HG_L
    t = jnp.arange(L)[:, None]
    s = jnp.arange(L)[None, :]
    ms = [(t == s)]
    for l in range(int(math.log2(L))):
        half = 1 << l
        same = (t // (2 * half)) == (s // (2 * half))
        ms.append(same & ((t % (2 * half)) >= half) & ((s % (2 * half)) < half))
    return jnp.stack(ms).astype(F32)


def _hgrn_kernel(q_ref, f_ref, i_ref, g_ref, lb_ref, lvl_ref, msk_ref, nw_ref, y_ref, S_ref):
    L = HG_L
    nl = int(math.log2(L))

    @pl.when(pl.program_id(1) == 0)
    def _():
        S_ref[...] = jnp.zeros_like(S_ref)

    lg = lb_ref[...]
    mx = jnp.max(lg, axis=0, keepdims=True)
    ex = jnp.exp(lg - mx)
    pr = ex / jnp.sum(ex, axis=0, keepdims=True)
    lb = (pr[0:1] + pr[1:2]) - pr[0:1]

    q = _silu(q_ref[...])
    fp = f_ref[...]
    iv = i_ref[...]
    a1 = jnp.log(lb)
    a2 = jnp.log1p(-lb) + _log_sigmoid(fp)
    log_f = jnp.maximum(a1, a2) + jnp.log1p(jnp.exp(-jnp.abs(a1 - a2)))
    k = (1.0 - lb) * _sigmoid(-fp)

    E = _dot_sel(lvl_ref[...], log_f)
    bcum = E[0:L]
    b_last = bcum[L - 1:L]
    qb = q * jnp.exp(bcum)
    kd = k * jnp.exp(b_last - bcum)
    e_last = jnp.exp(b_last)
    qs = [q]
    ks = [k]
    for l in range(nl):
        A = jnp.exp(E[(l + 1) * L:(l + 2) * L])
        qs.append(q * A)
        ks.append(k * A)
    gg = g_ref[...]

    sls = [slice(h * HG_HEAD, (h + 1) * HG_HEAD) for h in range(HG_H)]
    qsb = [_bf(x) for x in qs]
    ksb = [_bf(x) for x in ks]
    msk = [msk_ref[l] for l in range(nl + 1)]
    attn = [msk[0] * _dot_nt(qsb[0][:, sl], ksb[0][:, sl]) for sl in sls]
    for l in range(1, nl + 1):
        attn = [a + msk[l] * _dot_nt(qsb[l][:, sl], ksb[l][:, sl]) for a, sl in zip(attn, sls)]
    ivb = _bf(iv)
    qbb = _bf(qb)
    kdb = _bf(kd)
    St = [S_ref[h] for h in range(HG_H)]
    o = [_dot(_bf(a), ivb[:, sl]) + _dot_nt(qbb[:, sl], _bf(s)) for a, sl, s in zip(attn, sls, St)]
    Sn = [s * e_last[:, sl] + _dot_tn(ivb[:, sl], kdb[:, sl]) for s, sl in zip(St, sls)]
    for h in range(HG_H):
        S_ref[h] = Sn[h]
    for h, sl in enumerate(sls):
        ms = jnp.mean(o[h] * o[h], axis=-1, keepdims=True)
        on = o[h] * lax.rsqrt(ms + NORM_EPS) * nw_ref[:, sl]
        y_ref[:, sl] = _bf(on * _silu(gg[:, sl]))


def _hgrn(u, offs, B, T, prm):
    L = HG_L
    nc = T // L
    nl = int(math.log2(L))

    def col(off, w):
        return pl.BlockSpec((L, w), lambda b, c: (b * nc + c, off // w))

    in_specs = [col(offs["hg_q"], HG_W), col(offs["hg_f"], HG_W), col(offs["hg_i"], HG_W), col(offs["hg_g"], HG_W),
                pl.BlockSpec((2, HG_W), lambda b, c: (0, 0)),
                pl.BlockSpec(((nl + 1) * L, L), lambda b, c: (0, 0)),
                pl.BlockSpec((nl + 1, L, L), lambda b, c: (0, 0, 0)),
                pl.BlockSpec((1, HG_W), lambda b, c: (0, 0))]
    return pl.pallas_call(
        _hgrn_kernel,
        grid=(B, nc),
        in_specs=in_specs,
        out_specs=pl.BlockSpec((L, HG_W), lambda b, c: (b * nc + c, 0)),
        out_shape=jax.ShapeDtypeStruct((B * T, HG_W), BF16),
        scratch_shapes=[pltpu.VMEM((HG_H, HG_HEAD, HG_HEAD), F32)],
        compiler_params=_cparams(("parallel", "arbitrary")),
        name="hgrn2",
    )(u, u, u, u, *prm)


def _mamba_kernel(z_ref, x_ref, b_ref, c_ref, dtc_ref, dtr_ref,
                  cwx_ref, cbx_ref, cwb_ref, cbb_ref, cwc_ref, cbc_ref,
                  dbr_ref, dbc_ref, alr_ref, alc_ref, dsk_ref, nw_ref, exp_ref,
                  y_ref, S_ref, tx_ref, tb_ref, tc_ref):
    L = MB_L
    GW = MB_W // MB_G
    E = MB_H // MB_G

    @pl.when(pl.program_id(1) == 0)
    def _():
        S_ref[...] = jnp.zeros_like(S_ref)
        tx_ref[...] = jnp.zeros_like(tx_ref)
        tb_ref[...] = jnp.zeros_like(tb_ref)
        tc_ref[...] = jnp.zeros_like(tc_ref)

    def conv(ref, tail_ref, w_ref, bias_ref):
        x = ref[...]
        tail = tail_ref[...]
        acc = x * w_ref[CONV_K - 1:CONV_K] + bias_ref[...]
        for j in range(1, CONV_K):
            acc = acc + _shift_rows(tail, x, j) * w_ref[CONV_K - 1 - j:CONV_K - j]
        tail_ref[...] = x[L - SUBLANES:]
        return _silu(acc)

    xs = conv(x_ref, tx_ref, cwx_ref, cbx_ref)
    Bm = conv(b_ref, tb_ref, cwb_ref, cbb_ref)
    Cm = conv(c_ref, tc_ref, cwc_ref, cbc_ref)

    dt_c = _softplus(dtc_ref[...] + dbr_ref[...])
    dt_r = _softplus(dtr_ref[...] + dbc_ref[...])
    adt_c = dt_c * (-jnp.exp(alr_ref[...]))
    adt_r = dt_r * (-jnp.exp(alc_ref[...]))
    ri = _iota((L, L), 0)
    ci = _iota((L, L), 1)
    causal = ri >= ci
    tril = jnp.where(causal, 1.0, 0.0).astype(BF16)
    triu = jnp.where(ri <= ci, 1.0, 0.0).astype(BF16)
    ac_c = _dot_sel(tril, adt_c)
    ac_r = _dot_sel_r(adt_r, triu)
    ex = exp_ref[...]
    dt_full = _dot_sel_r(dt_c, ex)
    ac_full = _dot_sel_r(ac_c, ex)
    X = xs * dt_full
    a_last = ac_full[L - 1:L]
    dec_out = jnp.exp(ac_full)
    Xd = X * jnp.exp(a_last - ac_full)
    e_last = jnp.exp(a_last)
    neg = jnp.float32(-jnp.inf)
    lane_g = _idiv(_iota((1, GW), 1), MB_HEAD)
    zz = z_ref[...]

    for g in range(MB_G):
        gs = slice(g * GW, (g + 1) * GW)
        Bg = _bf(Bm[:, g * MB_N:(g + 1) * MB_N])
        Cg = _bf(Cm[:, g * MB_N:(g + 1) * MB_N])
        CB = _dot_nt(Cg, Bg)
        Xg = _bf(X[:, gs])
        Sg = S_ref[g]
        y = _dot(Cg, _bf(Sg)) * dec_out[:, gs]
        for e in range(E):
            h = g * E + e
            Lm = jnp.exp(jnp.where(causal, ac_c[:, h:h + 1] - ac_r[h:h + 1, :], neg))
            yd = _dot(_bf(CB * Lm), Xg)
            y = y + jnp.where(lane_g == e, yd, 0.0)
        S_ref[g] = Sg * e_last[:, gs] + _dot_tn(Bg, _bf(Xd[:, gs]))
        y = y + xs[:, gs] * dsk_ref[:, gs]
        y = y * _silu(zz[:, gs])
        ms = jnp.mean(y * y, axis=-1, keepdims=True)
        y_ref[:, gs] = _bf(y * lax.rsqrt(ms + NORM_EPS) * nw_ref[:, gs])


def _mamba(u, dt_row, offs, B, T, prm):
    L = MB_L
    nc = T // L
    GN = MB_G * MB_N

    def col(off, w):
        return pl.BlockSpec((L, w), lambda b, c: (b * nc + c, off // w))

    def par(r, w):
        return pl.BlockSpec((r, w), lambda b, c: (0, 0))

    in_specs = [col(offs["mb_z"], MB_W), col(offs["mb_x"], MB_W), col(offs["mb_b"], GN), col(offs["mb_c"], GN),
                col(offs["mb_dt"], LANES),
                pl.BlockSpec((MB_H, L), lambda b, c: (0, b * nc + c)),
                par(CONV_K, MB_W), par(1, MB_W), par(CONV_K, GN), par(1, GN), par(CONV_K, GN), par(1, GN),
                par(1, LANES), par(MB_H, 1), par(1, LANES), par(MB_H, 1), par(1, MB_W), par(1, MB_W),
                par(LANES, MB_W)]
    return pl.pallas_call(
        _mamba_kernel,
        grid=(B, nc),
        in_specs=in_specs,
        out_specs=pl.BlockSpec((L, MB_W), lambda b, c: (b * nc + c, 0)),
        out_shape=jax.ShapeDtypeStruct((B * T, MB_W), BF16),
        scratch_shapes=[pltpu.VMEM((MB_G, MB_N, MB_W // MB_G), F32),
                        pltpu.VMEM((SUBLANES, MB_W), F32),
                        pltpu.VMEM((SUBLANES, GN), F32),
                        pltpu.VMEM((SUBLANES, GN), F32)],
        compiler_params=_cparams(("parallel", "arbitrary")),
        name="mamba2",
    )(u, u, u, u, u, dt_row, *prm)


def _layout(segs, n_total):
    offs, cur = {}, 0
    for name, _, _, pw in segs:
        offs[name] = cur
        cur += pw
    assert cur <= n_total

    def pack(w, dtype=BF16):
        cols = []
        for _, s, wd, pw in segs:
            cols.append(w[:, s:s + wd])
            if pw > wd:
                cols.append(jnp.zeros((w.shape[0], pw - wd), w.dtype))
        if n_total > cur:
            cols.append(jnp.zeros((w.shape[0], n_total - cur), w.dtype))
        return jnp.concatenate(cols, axis=1).astype(dtype)

    return offs, pack


_EV_SEGS = [("rw_r", 0, RW_W, RW_W), ("rw_k", RW_W, RW_W, RW_W), ("rw_v", 2 * RW_W, RW_W, RW_W),
            ("ml_qk", RW_IN, 2 * ML_QK, 2 * ML_QK), ("ml_v", RW_IN + 2 * ML_QK, ML_W, ML_W),
            ("ml_o", RW_IN + 2 * ML_QK + ML_W, ML_W, ML_W),
            ("rw_dg", 3 * RW_W + RW_DECAY_LORA + RW_ICLR_LORA, RW_GATE_LORA, RW_GATE_LORA),
            ("rw_dw", 3 * RW_W, RW_DECAY_LORA, LANES),
            ("rw_da", 3 * RW_W + RW_DECAY_LORA, RW_ICLR_LORA, LANES),
            ("ml_if", RW_IN + 2 * ML_QK + 2 * ML_W, 2 * ML_H, LANES)]
_EV_N = 7168
_OD_SEGS = [("hg_q", 0, HG_W, HG_W), ("hg_f", HG_W, HG_W, HG_W), ("hg_i", 2 * HG_W, HG_W, HG_W),
            ("hg_g", 3 * HG_W, HG_W, HG_W),
            ("mb_z", HG_IN, MB_W, MB_W), ("mb_x", HG_IN + MB_W, MB_W, MB_W),
            ("mb_b", HG_IN + 2 * MB_W, MB_G * MB_N, MB_G * MB_N),
            ("mb_c", HG_IN + 2 * MB_W + MB_G * MB_N, MB_G * MB_N, MB_G * MB_N),
            ("mb_dt", HG_IN + MB_W + MB_CONV_W, MB_H, LANES)]
_OD_N = 7680


def _row(v, width=None):
    v = v.reshape(1, -1).astype(F32)
    if width is not None and v.shape[1] < width:
        v = jnp.pad(v, ((0, 0), (0, width - v.shape[1])))
    return v


def _pad_rows(w, rows):
    return jnp.pad(w, ((0, rows - w.shape[0]), (0, 0)))


def _even_layer(x, B, T, p):
    offs, pack = _layout(_EV_SEGS, _EV_N)
    s_if = RW_IN + 2 * ML_QK + 2 * ML_W
    u, g_row = _norm_matmul(x, _row(p["norm1"]), pack(p["w_in"]), _bf(p["w_in"][:, s_if:s_if + 2 * ML_H].T))
    mu = p["rw_mu"]
    o_dw, o_da, o_dg = 3 * RW_W, 3 * RW_W + RW_DECAY_LORA, 3 * RW_W + RW_DECAY_LORA + RW_ICLR_LORA
    rw_prm = [_row(mu[0:RW_W]), _row(mu[RW_W:2 * RW_W]), _row(mu[2 * RW_W:3 * RW_W]),
              _row(mu[o_dw:o_da], LANES), _row(mu[o_da:o_dg], LANES), _row(mu[o_dg:]),
              _row(p["rw_w0"]), _bf(_pad_rows(p["rw_w2"], LANES)),
              _row(p["rw_a0"]), _bf(_pad_rows(p["rw_a2"], LANES)), _bf(p["rw_g2"]),
              _row(p["rw_k_k"]), _row(p["rw_k_a"]), _row(p["rw_r_k"]), _row(p["rw_ln_w"]), _row(p["rw_ln_b"])]
    y_a = _rwkv(u, offs, B, T, rw_prm)
    gb =jnp.concatenate([p["ml_i_b"], p["ml_f_b"]]).astype(F32)
    ml_prm = [p["ml_conv_w"].astype(F32), _row(p["ml_conv_b"]), _row(gb, LANES), gb.reshape(-1, 1),
              _row(p["ml_norm_w"])]
    y_b = _mlstm(u, g_row, offs, B, T, ml_prm)
    x = _matmul_res(y_a, y_b, _bf(p["w_out"]), x)
    return _ffn(x, _row(p["norm2"]), _bf(p["ffn_w_gate"]), _bf(p["ffn_w_up"]), _bf(p["ffn_w_down"]))


def _odd_layer(x, B, T, p, final_w):
    offs, pack = _layout(_OD_SEGS, _OD_N)
    s_dt = HG_IN + MB_W + MB_CONV_W
    u, dt_row = _norm_matmul(x, _row(p["norm1"]), pack(p["w_in"]), _bf(p["w_in"][:, s_dt:s_dt + MB_H].T))
    hg_prm = [p["hg_lb_logits"].astype(F32), _hgrn_level_matrices(), _hgrn_masks(), _row(p["hg_norm_w"])]
    y_c = _hgrn(u, offs, B, T, hg_prm)
    cw, cb = p["mb_conv_w"].astype(F32), p["mb_conv_b"].astype(F32)
    GN = MB_G * MB_N
    expand = (jnp.arange(LANES)[:, None] == (jnp.arange(MB_W)[None, :] // MB_HEAD)).astype(BF16)
    mb_prm = [cw[:, :MB_W], _row(cb[:MB_W]), cw[:, MB_W:MB_W + GN], _row(cb[MB_W:MB_W + GN]),
              cw[:, MB_W + GN:], _row(cb[MB_W + GN:]),
              _row(p["mb_dt_bias"], LANES), p["mb_dt_bias"].astype(F32).reshape(-1, 1),
              _row(p["mb_A_log"], LANES), p["mb_A_log"].astype(F32).reshape(-1, 1),
              _row(jnp.repeat(p["mb_D"], MB_HEAD)), _row(p["mb_norm_w"]), expand]
    y_d = _mamba(u, dt_row, offs, B, T, mb_prm)
    x = _matmul_res(y_c, y_d, _bf(p["w_out"]), x)
    wr = jnp.pad(p["moe_router"].astype(F32), ((0, 0), (0, LANES - N_EXPERTS)))
    idx, prob, h = _router(x, _row(p["norm2"]), wr, 2 * x.shape[0] + N_EXPERTS * TM_MOE)
    pos, src_tok, tile_expert, tile_active = _route_plan(idx[:, :2], TM_MOE)
    xg = jnp.take(h, src_tok, axis=0, mode="clip")
    act = _moe_up(xg, tile_expert, tile_active, p["moe_w_gate"].astype(F32), p["moe_w_up"].astype(F32))
    yg = _moe_down(act, tile_expert, tile_active, p["moe_w_down"].astype(F32))
    y0 = jnp.take(yg, pos[:, 0], axis=0, mode="clip")
    y1 = jnp.take(yg, pos[:, 1], axis=0, mode="clip")
    return _combine(x, y0, y1, prob, _row(final_w))


def kernel(x, final_norm_w, hg_lb_logits, ev_norm1_w, ev_w_in, ev_w_out, rw_mu, rw_w0, rw_w2, rw_a0, rw_a2, rw_g2, rw_k_k, rw_k_a, rw_r_k, rw_ln_w, rw_ln_b, ml_conv_w, ml_conv_b, ml_i_b, ml_f_b, ml_norm_w, ev_norm2_w, ffn_w_gate, ffn_w_up, ffn_w_down, od_norm1_w, od_w_in, od_w_out, hg_norm_w, mb_conv_w, mb_conv_b, mb_dt_bias, mb_A_log, mb_D, mb_norm_w, od_norm2_w, moe_router, moe_w_gate, moe_w_up, moe_w_down):
    B, T, D = x.shape
    xf = x.reshape(B * T, D)
    ev = dict(norm1=ev_norm1_w[0], w_in=ev_w_in[0], w_out=ev_w_out[0], rw_mu=rw_mu[0], rw_w0=rw_w0[0],
              rw_w2=rw_w2[0], rw_a0=rw_a0[0], rw_a2=rw_a2[0], rw_g2=rw_g2[0], rw_k_k=rw_k_k[0],
              rw_k_a=rw_k_a[0], rw_r_k=rw_r_k[0], rw_ln_w=rw_ln_w[0], rw_ln_b=rw_ln_b[0],
              ml_conv_w=ml_conv_w[0], ml_conv_b=ml_conv_b[0], ml_i_b=ml_i_b[0], ml_f_b=ml_f_b[0],
              ml_norm_w=ml_norm_w[0], norm2=ev_norm2_w[0], ffn_w_gate=ffn_w_gate[0], ffn_w_up=ffn_w_up[0],
              ffn_w_down=ffn_w_down[0])
    od = dict(norm1=od_norm1_w[0], w_in=od_w_in[0], w_out=od_w_out[0], hg_lb_logits=hg_lb_logits,
              hg_norm_w=hg_norm_w[0], mb_conv_w=mb_conv_w[0], mb_conv_b=mb_conv_b[0], mb_dt_bias=mb_dt_bias[0],
              mb_A_log=mb_A_log[0], mb_D=mb_D[0], mb_norm_w=mb_norm_w[0], norm2=od_norm2_w[0],
              moe_router=moe_router[0], moe_w_gate=moe_w_gate[0], moe_w_up=moe_w_up[0],
              moe_w_down=moe_w_down[0])
    xf = _even_layer(xf, B, T, ev)
    xf = _odd_layer(xf, B, T, od, final_norm_w)
    return xf.reshape(B, T, D)
```

```python
import functools
import math

import jax
import jax.numpy as jnp
from jax import lax
from jax.experimental import pallas as pl
from jax.experimental.pallas import tpu as pltpu

F32 = jnp.float32
BF16 = jnp.bfloat16

D_MODEL = 2048
NORM_EPS = 1e-6
RW_HEAD = 64
RW_W = 1024
RW_H = RW_W // RW_HEAD
RW_DECAY_LORA = 96
RW_ICLR_LORA = 96
RW_GATE_LORA = 256
RW_IN = 3 * RW_W + RW_DECAY_LORA + RW_ICLR_LORA + RW_GATE_LORA
RW_LN_EPS = 64e-5
RW_DECAY_SCALE = math.exp(-0.5)
ML_W = 1024
ML_H = 4
ML_DV = ML_W // ML_H
ML_DK = ML_DV // 2
ML_QK = ML_H * ML_DK
ML_IN = 2 * ML_QK + 2 * ML_W + 2 * ML_H
HG_W = 1024
HG_HEAD = 128
HG_H = HG_W // HG_HEAD
HG_IN = 4 * HG_W
MB_W = 1024
MB_HEAD = 64
MB_H = MB_W // MB_HEAD
MB_G = 4
MB_N = 128
MB_CONV_W = MB_W + 2 * MB_G * MB_N
N_EXPERTS = 8

LANES = 128
SUBLANES = 8
VMEM_LIMIT = 62 * 1024 * 1024

TM_PROJ = 1024
TN_PROJ = (1024, 768, 512)
TM_DENSE = 1024
TM_FFN = 512
TF_FFN = 512
TM_MOE = 256
TF_MOE = 1408
TN_MOE = 1024
RW_L = 64
RW_BLK = 16
ML_L = 128
HG_L = 64
MB_L = 128
CONV_K = 4


def _cparams(sem):
    return pltpu.CompilerParams(dimension_semantics=sem, vmem_limit_bytes=VMEM_LIMIT)


def _proj_tn(n):
    return next(t for t in TN_PROJ if n % t == 0)


def _dot(a, b):
    return jnp.dot(a, b, preferred_element_type=F32)


def _dot_nt(a, b):
    return lax.dot_general(a, b, (((1,), (1,)), ((), ())), preferred_element_type=F32)


def _dot_tn(a, b):
    return lax.dot_general(a, b, (((0,), (0,)), ((), ())), preferred_element_type=F32)


def _bf(x):
    return x.astype(BF16)


def _split2(x):
    h = x.astype(BF16)
    l = (x - h.astype(F32)).astype(BF16)
    return h, l


def _split3(x):
    h = x.astype(BF16)
    r = x - h.astype(F32)
    m = r.astype(BF16)
    l = (r - m.astype(F32)).astype(BF16)
    return h, m, l


def _dot_sel(sel_bf16, x):
    h, m, l = _split3(x)
    return _dot(sel_bf16, h) + _dot(sel_bf16, m) + _dot(sel_bf16, l)


def _dot_sel_r(x, sel_bf16):
    h, m, l = _split3(x)
    return _dot(h, sel_bf16) + _dot(m, sel_bf16) + _dot(l, sel_bf16)


def _sigmoid(x):
    return 1.0 / (1.0 + jnp.exp(-x))


def _silu(x):
    return x * _sigmoid(x)


def _log_sigmoid(x):
    return -(jnp.maximum(-x, 0.0) + jnp.log1p(jnp.exp(-jnp.abs(x))))


def _softplus(x):
    return jnp.maximum(x, 0.0) + jnp.log1p(jnp.exp(-jnp.abs(x)))


def _iota(shape, dim):
    return lax.broadcasted_iota(jnp.int32, shape, dim)


def _idiv(x, d):
    sh = d.bit_length() - 1
    assert d == 1 << sh
    return lax.shift_right_logical(x, jnp.int32(sh))


def _shift_rows(tail, x, j):
    xc = jnp.concatenate([tail, x], axis=0)
    return pltpu.roll(xc, j, 0)[SUBLANES:]


def _norm_matmul_kernel(x_ref, nw_ref, w_ref, wt_ref, o_ref, ot_ref, h_ref):
    @pl.when(pl.program_id(1) == 0)
    def _():
        x = x_ref[...]
        ms = jnp.mean(x * x, axis=-1, keepdims=True)
        h_ref[...] = _bf(x * lax.rsqrt(ms + NORM_EPS) * nw_ref[...])
        ot_ref[...] = _dot_nt(wt_ref[...], h_ref[...])

    o_ref[...] = _dot(h_ref[...], w_ref[...])


def _norm_matmul(x, nw, w, wt):
    m, d = x.shape
    n = w.shape[1]
    r = wt.shape[0]
    tn = _proj_tn(n)
    return pl.pallas_call(
        _norm_matmul_kernel,
        grid=(m // TM_PROJ, n // tn),
        in_specs=[pl.BlockSpec((TM_PROJ, d), lambda i, j: (i, 0)),
                  pl.BlockSpec((1, d), lambda i, j: (0, 0)),
                  pl.BlockSpec((d, tn), lambda i, j: (0, j)),
                  pl.BlockSpec((r, d), lambda i, j: (0, 0))],
        out_specs=[pl.BlockSpec((TM_PROJ, tn), lambda i, j: (i, j)),
                   pl.BlockSpec((r, TM_PROJ), lambda i, j: (0, i))],
        out_shape=[jax.ShapeDtypeStruct((m, n), F32),
                   jax.ShapeDtypeStruct((r, m), F32)],
        scratch_shapes=[pltpu.VMEM((TM_PROJ, d), BF16)],
        compiler_params=_cparams(("parallel", "arbitrary")),
        name="norm_matmul",
    )(x, nw, w, wt)


def _matmul_res_kernel(ya_ref, yb_ref, wa_ref, wb_ref, r_ref, o_ref):
    o_ref[...] = r_ref[...] + _dot(ya_ref[...], wa_ref[...]) + _dot(yb_ref[...], wb_ref[...])


def _matmul_res(ya, yb, w, res):
    m, k = ya.shape
    n = w.shape[1]
    tn = _proj_tn(n)
    return pl.pallas_call(
        _matmul_res_kernel,
        grid=(m // TM_PROJ, n // tn),
        in_specs=[pl.BlockSpec((TM_PROJ, k), lambda i, j: (i, 0)),
                  pl.BlockSpec((TM_PROJ, k), lambda i, j: (i, 0)),
                  pl.BlockSpec((k, tn), lambda i, j: (0, j)),
                  pl.BlockSpec((k, tn), lambda i, j: (1, j)),
                  pl.BlockSpec((TM_PROJ, tn), lambda i, j: (i, j))],
        out_specs=pl.BlockSpec((TM_PROJ, tn), lambda i, j: (i, j)),
        out_shape=jax.ShapeDtypeStruct((m, n), F32),
        compiler_params=_cparams(("parallel", "arbitrary")),
        name="matmul_res",
    )(ya, yb, w, w, res)


def _ffn_kernel(x_ref, nw_ref, wg_ref, wu_ref, wd_ref, o_ref, h_ref):
    f = pl.program_id(1)

    @pl.when(f == 0)
    def _():
        x = x_ref[...]
        ms = jnp.mean(x * x, axis=-1, keepdims=True)
        h_ref[...] = _bf(x * lax.rsqrt(ms + NORM_EPS) * nw_ref[...])
        o_ref[...] = x

    h = h_ref[...]
    act = _silu(_dot(h, wg_ref[...])) * _dot(h, wu_ref[...])
    o_ref[...] += _dot(_bf(act), wd_ref[...])


def _ffn(x, nw, wg, wu, wd):
    m, d = x.shape
    f = wg.shape[1]
    return pl.pallas_call(
        _ffn_kernel,
        grid=(m // TM_DENSE, f // TF_FFN),
        in_specs=[pl.BlockSpec((TM_DENSE, d), lambda i, j: (i, 0)),
                  pl.BlockSpec((1, d), lambda i, j: (0, 0)),
                  pl.BlockSpec((d, TF_FFN), lambda i, j: (0, j)),
                  pl.BlockSpec((d, TF_FFN), lambda i, j: (0, j)),
                  pl.BlockSpec((TF_FFN, d), lambda i, j: (j, 0))],
        out_specs=pl.BlockSpec((TM_DENSE, d), lambda i, j: (i, 0)),
        out_shape=jax.ShapeDtypeStruct((m, d), F32),
        scratch_shapes=[pltpu.VMEM((TM_DENSE, d), BF16)],
        compiler_params=_cparams(("parallel", "arbitrary")),
        name="ffn_swiglu",
    )(x, nw, wg, wu, wd)


def _router_kernel(x_ref, nw_ref, wr_ref, i_ref, p_ref, h_ref, *, n_blocks):
    step = pl.program_id(0)

    @pl.when(step < n_blocks)
    def _():
        _router_block(x_ref, nw_ref, wr_ref, i_ref, p_ref, h_ref)

    @pl.when(step >= n_blocks)
    def _():
        h_ref[...] = jnp.zeros_like(h_ref)


def _router_block(x_ref, nw_ref, wr_ref, i_ref, p_ref, h_ref):
    x = x_ref[...]
    ms = jnp.mean(x * x, axis=-1, keepdims=True)
    h = x * lax.rsqrt(ms + NORM_EPS) * nw_ref[...]
    h_ref[...] = _bf(h)
    wr = wr_ref[...]
    hh, hl = _split2(h)
    wh, wl = _split2(wr)
    logits = _dot(hh, wh) + _dot(hh, wl) + _dot(hl, wh)
    lane = _iota(logits.shape, 1)
    neg = jnp.float32(-jnp.inf)
    logits = jnp.where(lane < N_EXPERTS, logits, neg)
    v1 = jnp.max(logits, axis=-1, keepdims=True)
    i1 = jnp.min(jnp.where(logits == v1, lane, LANES), axis=-1, keepdims=True)
    rest = jnp.where(lane == i1, neg, logits)
    v2 = jnp.max(rest, axis=-1, keepdims=True)
    i2 = jnp.min(jnp.where(rest == v2, lane, LANES), axis=-1, keepdims=True)
    e2 = jnp.exp(v2 - v1)
    p1 = 1.0 / (1.0 + e2)
    p2 = e2 / (1.0 + e2)
    i_ref[...] = jnp.where(lane == 0, i1, jnp.where(lane == 1, i2, 0))
    p_ref[...] = jnp.where(lane == 0, p1, jnp.where(lane == 1, p2, 0.0))


def _router(x, nw, wr, table_rows):
    m, d = x.shape
    nb = m // TM_FFN
    tok = lambda i: (jnp.minimum(i, nb - 1), 0)
    return pl.pallas_call(
        functools.partial(_router_kernel, n_blocks=nb),
        grid=(table_rows // TM_FFN,),
        in_specs=[pl.BlockSpec((TM_FFN, d), tok),
                  pl.BlockSpec((1, d), lambda i: (0, 0)),
                  pl.BlockSpec((d, LANES), lambda i: (0, 0))],
        out_specs=[pl.BlockSpec((TM_FFN, LANES), tok),
                   pl.BlockSpec((TM_FFN, LANES), tok),
                   pl.BlockSpec((TM_FFN, d), lambda i: (i, 0))],
        out_shape=[jax.ShapeDtypeStruct((m, LANES), jnp.int32),
                   jax.ShapeDtypeStruct((m, LANES), F32),
                   jax.ShapeDtypeStruct((table_rows, d), BF16)],
        compiler_params=_cparams(("arbitrary",)),
        name="moe_router",
    )(x, nw, wr)


def _route_plan(top_idx, tm):
    m = top_idx.shape[0]
    n_rows = 2 * m + N_EXPERTS * tm
    e_flat = top_idx.reshape(-1)
    onehot = (e_flat[:, None] == jnp.arange(N_EXPERTS, dtype=jnp.int32)[None, :]).astype(jnp.int32)
    rank = jnp.cumsum(onehot, axis=0) - onehot
    counts = jnp.sum(onehot, axis=0)
    padded = ((counts + tm - 1) // tm) * tm
    ends = jnp.cumsum(padded)
    off = ends - padded
    pos = jnp.sum(onehot * (off[None, :] + rank), axis=1)
    src_tok = jnp.zeros((n_rows,), jnp.int32).at[pos].set(jnp.arange(2 * m, dtype=jnp.int32) // 2,
                                                          unique_indices=True)
    tile_start = jnp.arange(n_rows // tm, dtype=jnp.int32) * tm
    tile_expert = jnp.minimum(jnp.sum((tile_start[:, None] >= ends[None, :]).astype(jnp.int32), axis=1),
                              N_EXPERTS - 1)
    tile_active = (tile_start < ends[-1]).astype(jnp.int32)
    return pos.reshape(m, 2), src_tok, tile_expert, tile_active


def _new_weight_block(te_ref, i):
    return (i == 0) | (te_ref[i] != te_ref[jnp.maximum(i - 1, 0)])


def _moe_up_kernel(te_ref, ta_ref, x_ref, wg_ref, wu_ref, a_ref, wgb_ref, wub_ref):
    i = pl.program_id(1)

    @pl.when(_new_weight_block(te_ref, i))
    def _():
        wgb_ref[...] = _bf(wg_ref[0])
        wub_ref[...] = _bf(wu_ref[0])

    @pl.when(ta_ref[i] == 1)
    def _():
        x = x_ref[...]
        act = _silu(_dot(x, wgb_ref[...])) * _dot(x, wub_ref[...])
        a_ref[...] = _bf(act)

    @pl.when(ta_ref[i] == 0)
    def _():
        a_ref[...] = jnp.zeros_like(a_ref)


def _moe_up(xg, tile_expert, tile_active, wg, wu):
    n_rows, d = xg.shape
    fe = wg.shape[2]
    wmap = lambda f, i, te, ta: (te[i], 0, f)
    grid_spec = pltpu.PrefetchScalarGridSpec(
        num_scalar_prefetch=2,
        grid=(fe // TF_MOE, n_rows // TM_MOE),
        in_specs=[pl.BlockSpec((TM_MOE, d), lambda f, i, te, ta: (i, 0)),
                  pl.BlockSpec((1, d, TF_MOE), wmap),
                  pl.BlockSpec((1, d, TF_MOE), wmap)],
        out_specs=pl.BlockSpec((TM_MOE, TF_MOE), lambda f, i, te, ta: (i, f)),
        scratch_shapes=[pltpu.VMEM((d, TF_MOE), BF16), pltpu.VMEM((d, TF_MOE), BF16)],
    )
    return pl.pallas_call(
        _moe_up_kernel,
        grid_spec=grid_spec,
        out_shape=jax.ShapeDtypeStruct((n_rows, fe), BF16),
        compiler_params=_cparams(("arbitrary", "arbitrary")),
        name="moe_up",
    )(tile_expert, tile_active, xg, wg, wu)


def _moe_down_kernel(te_ref, ta_ref, a_ref, wd_ref, y_ref, wdb_ref):
    i = pl.program_id(1)

    @pl.when(_new_weight_block(te_ref, i))
    def _():
        wdb_ref[...] = _bf(wd_ref[0])

    @pl.when(ta_ref[i] == 1)
    def _():
        y_ref[...] = _bf(_dot(a_ref[...], wdb_ref[...]))

    @pl.when(ta_ref[i] == 0)
    def _():
        y_ref[...] = jnp.zeros_like(y_ref)


def _moe_down(act, tile_expert, tile_active, wd):
    n_rows, fe = act.shape
    d = wd.shape[2]
    grid_spec = pltpu.PrefetchScalarGridSpec(
        num_scalar_prefetch=2,
        grid=(d // TN_MOE, n_rows // TM_MOE),
        in_specs=[pl.BlockSpec((TM_MOE, fe), lambda n, i, te, ta: (i, 0)),
                  pl.BlockSpec((1, fe, TN_MOE), lambda n, i, te, ta: (te[i], 0, n))],
        out_specs=pl.BlockSpec((TM_MOE, TN_MOE), lambda n, i, te, ta: (i, n)),
        scratch_shapes=[pltpu.VMEM((fe, TN_MOE), BF16)],
    )
    return pl.pallas_call(
        _moe_down_kernel,
        grid_spec=grid_spec,
        out_shape=jax.ShapeDtypeStruct((n_rows, d), BF16),
        compiler_params=_cparams(("arbitrary", "arbitrary")),
        name="moe_down",
    )(tile_expert, tile_active, act, wd)


def _combine_kernel(x_ref, y0_ref, y1_ref, p_ref, fw_ref, o_ref):
    p = p_ref[...]
    y = x_ref[...] + p[:, 0:1] * y0_ref[...].astype(F32) + p[:, 1:2] * y1_ref[...].astype(F32)
    ms = jnp.mean(y * y, axis=-1, keepdims=True)
    o_ref[...] = y * lax.rsqrt(ms + NORM_EPS) * fw_ref[...]


def _combine(x, y0, y1, prob, final_w):
    m, d = x.shape
    row = pl.BlockSpec((TM_FFN, d), lambda i: (i, 0))
    return pl.pallas_call(
        _combine_kernel,
        grid=(m // TM_FFN,),
        in_specs=[row, row, row, pl.BlockSpec((TM_FFN, LANES), lambda i: (i, 0)),
                  pl.BlockSpec((1, d), lambda i: (0, 0))],
        out_specs=row,
        out_shape=jax.ShapeDtypeStruct((m, d), F32),
        compiler_params=_cparams(("parallel",)),
        name="moe_combine",
    )(x, y0, y1, prob, final_w)


def _rwkv_kernel(r_ref, k_ref, v_ref, dw_ref, da_ref, dg_ref,
                 mur_ref, muk_ref, muv_ref, mudw_ref, muda_ref, mudg_ref,
                 w0_ref, w2_ref, a0_ref, a2_ref, g2_ref, kk_ref, ka_ref, rk_ref, lnw_ref, lnb_ref,
                 o_ref,
                 H_ref, tr_ref, tk_ref, tv_ref, tdw_ref, tda_ref, tdg_ref):
    L = RW_L

    @pl.when(pl.program_id(1) == 0)
    def _():
        H_ref[...] = jnp.zeros_like(H_ref)
        for t in (tr_ref, tk_ref, tv_ref, tdw_ref, tda_ref, tdg_ref):
            t[...] = jnp.zeros_like(t)

    def shift_mix(ref, tail, mu):
        x = ref[...]
        prev = _shift_rows(tail[...], x, 1)
        tail[...] = x[L - SUBLANES:]
        return x + (prev - x) * mu[...]

    r = shift_mix(r_ref, tr_ref, mur_ref)
    k = shift_mix(k_ref, tk_ref, muk_ref)
    v = shift_mix(v_ref, tv_ref, muv_ref)
    dw = shift_mix(dw_ref, tdw_ref, mudw_ref)
    da = shift_mix(da_ref, tda_ref, muda_ref)
    dg = shift_mix(dg_ref, tdg_ref, mudg_ref)

    log_w = -RW_DECAY_SCALE * _sigmoid(w0_ref[...] + _dot(_bf(jnp.tanh(dw)), w2_ref[...]))
    a = _sigmoid(a0_ref[...] + _dot(_bf(da), a2_ref[...]))
    g = _dot(_bf(_sigmoid(dg)), g2_ref[...])

    ri = _iota((LANES, LANES), 0)
    ci = _iota((LANES, LANES), 1)
    hsum = jnp.where(_idiv(ri, RW_HEAD) == _idiv(ci, RW_HEAD), 1.0, 0.0).astype(BF16)
    same_head = _idiv(ri, L) == _idiv(ci, L)
    tril_s = jnp.where(same_head & (ri > ci), 1.0, 0.0)
    tril_i = jnp.where(same_head & (ri >= ci), 1.0, 0.0)
    blk_d = jnp.where(_idiv(ri, RW_BLK) == _idiv(ci, RW_BLK), 1.0, 0.0)
    eye = jnp.where(ri == ci, 1.0, 0.0)
    cum = jnp.where(_iota((L, L), 0) >= _iota((L, L), 1), 1.0, 0.0).astype(BF16)
    lane = _iota((1, LANES), 1)
    m0 = lane < RW_HEAD

    def head_sum(x):
        h, l = _split2(x)
        return _dot(h, hsum) + _dot(l, hsum)

    kk = k * kk_ref[...]
    k2 = k * (1.0 + (a - 1.0) * ka_ref[...])
    cw = _dot_sel(cum, log_w)
    e_prev = jnp.exp(cw - log_w)
    e_inv = jnp.exp(-cw)
    e_cw = jnp.exp(cw)
    wl = cw[L - 1:L]
    e_end = jnp.exp(wl - cw)
    e_wl = jnp.exp(wl)
    rk2 = r * k2 * rk_ref[...]

    def stack(x):
        x0 = jnp.where(m0, x, 0.0)
        return jnp.concatenate([x0, x - x0], axis=0)

    def mm3(xs, ys):
        return _dot(xs[0], ys[0]) + _dot(xs[0], ys[1]) + _dot(xs[1], ys[0])

    NP = RW_H // 2
    sls = [slice(p * LANES, (p + 1) * LANES) for p in range(NP)]

    def each(fn, *lists):
        return [fn(*args) for args in zip(*lists)]

    kk_l = [kk[:, sl] for sl in sls]
    kk_l = each(lambda x: x * lax.rsqrt(jnp.maximum(head_sum(x * x), 1e-12)), kk_l)
    kka_l = [x * a[:, sl] for x, sl in zip(kk_l, sls)]
    At = [_bf(stack(-x * e_prev[:, sl])) for x, sl in zip(kk_l, sls)]
    Bt = [_bf(stack(x * e_inv[:, sl])) for x, sl in zip(kka_l, sls)]
    Kt = [_bf(stack(k2[:, sl] * e_inv[:, sl])) for sl in sls]
    Rt_f = [stack(r[:, sl] * e_cw[:, sl]) for sl in sls]
    Rt = each(_bf, Rt_f)
    Bh = [_bf(stack(x * e_end[:, sl])) for x, sl in zip(kka_l, sls)]
    Kh = [_bf(stack(k2[:, sl] * e_end[:, sl])) for sl in sls]
    Vs = [_bf(stack(v[:, sl])) for sl in sls]

    Mab = each(lambda x, y: _dot_nt(x, y) * tril_s, At, Bt)
    Mak = each(lambda x, y: _bf(_dot_nt(x, y) * tril_s), At, Kt)
    Arb = each(lambda x, y: _bf(_dot_nt(x, y) * tril_i), Rt, Bt)
    Ark = each(lambda x, y: _bf(_dot_nt(x, y) * tril_i), Rt, Kt)

    MD = each(lambda m: m * blk_d, Mab)
    Nn = each(lambda m, d: _split2(m - d), Mab, MD)
    P = each(lambda d: eye + d, MD)
    MDs = each(_split2, MD)
    S = each(mm3, MDs, MDs)
    for it in range(3):
        Ss = each(_split2, S)
        P = each(lambda p_, s_: p_ + mm3(_split2(p_), s_), P, Ss)
        if it < 2:
            S = each(mm3, Ss, Ss)
    Ps = each(_split2, P)
    X = each(mm3, Ps, Nn)
    Xs = each(_split2, X)
    X2 = each(lambda x: _split2(mm3(x, x)), Xs)
    Y = each(lambda x: eye + x, X)
    Y = each(lambda y_, x2: y_ + mm3(_split2(y_), x2), Y, X2)
    T = each(lambda y_, p_: _split2(mm3(_split2(y_), p_)), Y, Ps)

    P1 = each(lambda t, x: _bf(_dot(t[0], x) + _dot(t[1], x)), T, At)
    MV = each(lambda m, x: _split2(_dot(m, x)), Mak, Vs)
    P2 = each(lambda t, x: _bf(mm3(t, x)), T, MV)
    G = [eye * e_wl[:, sl] + _dot_tn(b, p1) for sl, b, p1 in zip(sls, Bh, P1)]
    J = each(lambda b, p2, kh, vs: _dot_tn(b, p2) + _dot_tn(kh, vs), Bh, P2, Kh, Vs)
    Q = each(lambda rf, ar, p1: rf + _dot(ar, p1), Rt_f, Arb, P1)
    Z = each(lambda ar, p2, ak, vs: _dot(ar, p2) + _dot(ak, vs), Arb, P2, Ark, Vs)

    Hs = [_split2(H_ref[p]) for p in range(NP)]
    Yst = each(lambda q, h, z: mm3(_split2(q), h) + z, Q, Hs, Z)
    Hn = each(lambda g_, h, j: mm3(_split2(g_), h) + j, G, Hs, J)
    for p in range(NP):
        H_ref[p] = Hn[p]
    y_l = each(lambda y_: y_[:L] + y_[L:], Yst)

    mean = each(lambda y_: head_sum(y_) * (1.0 / RW_HEAD), y_l)
    yc = each(lambda y_, m: y_ - m, y_l, mean)
    var = each(lambda c: head_sum(c * c) * (1.0 / RW_HEAD), yc)
    yn = [c * lax.rsqrt(vr + RW_LN_EPS) * lnw_ref[:, sl] + lnb_ref[:, sl] for c, vr, sl in zip(yc, var, sls)]
    bonus = [head_sum(rk2[:, sl]) * v[:, sl] for sl in sls]
    for p in range(NP):
        o_ref[:, sls[p]] = _bf((yn[p] + bonus[p]) * g[:, sls[p]])


def _rwkv(u, offs, B, T, prm):
    L = RW_L
    nc = T // L

    def col(off, w):
        return pl.BlockSpec((L, w), lambda b, c: (b * nc + c, off // w))

    def par(w, rows=1):
        return pl.BlockSpec((rows, w), lambda b, c: (0, 0))

    in_specs = [col(offs["rw_r"], RW_W), col(offs["rw_k"], RW_W), col(offs["rw_v"], RW_W),
                col(offs["rw_dw"], LANES), col(offs["rw_da"], LANES), col(offs["rw_dg"], RW_GATE_LORA),
                par(RW_W), par(RW_W), par(RW_W), par(LANES), par(LANES), par(RW_GATE_LORA),
                par(RW_W), par(RW_W, LANES), par(RW_W), par(RW_W, LANES), par(RW_W, RW_GATE_LORA),
                par(RW_W), par(RW_W), par(RW_W), par(RW_W), par(RW_W)]
    return pl.pallas_call(
        _rwkv_kernel,
        grid=(B, nc),
        in_specs=in_specs,
        out_specs=pl.BlockSpec((L, RW_W), lambda b, c: (b * nc + c, 0)),
        out_shape=jax.ShapeDtypeStruct((B * T, RW_W), BF16),
        scratch_shapes=[pltpu.VMEM((RW_H // 2, LANES, LANES), F32),
                        pltpu.VMEM((SUBLANES, RW_W), F32), pltpu.VMEM((SUBLANES, RW_W), F32),
                        pltpu.VMEM((SUBLANES, RW_W), F32), pltpu.VMEM((SUBLANES, LANES), F32),
                        pltpu.VMEM((SUBLANES, LANES), F32), pltpu.VMEM((SUBLANES, RW_GATE_LORA), F32)],
        compiler_params=_cparams(("parallel", "arbitrary")),
        name="rwkv7",
    )(u, u, u, u, u, u, *prm)


def _mlstm_kernel(qk_ref, v_ref, o_ref, gc_ref, gr_ref, cw_ref, cb_ref, gbr_ref, gbc_ref, nw_ref,
                  y_ref, C_ref, m_ref, tail_ref):
    L = ML_L
    DK, DV = ML_DK, ML_DV

    @pl.when(pl.program_id(1) == 0)
    def _():
        C_ref[...] = jnp.zeros_like(C_ref)
        m_ref[...] = jnp.zeros_like(m_ref)
        tail_ref[...] = jnp.zeros_like(tail_ref)

    x = qk_ref[...]
    tail = tail_ref[...]
    acc = x * cw_ref[CONV_K - 1:CONV_K] + cb_ref[...]
    for j in range(1, CONV_K):
        acc = acc + _shift_rows(tail, x, j) * cw_ref[CONV_K - 1 - j:CONV_K - j]
    tail_ref[...] = x[L - SUBLANES:]
    qk = _silu(acc)

    gc = gc_ref[...] + gbr_ref[...]
    gr = gr_ref[...] + gbc_ref[...]
    fl_c = _log_sigmoid(gc)
    fl_r = _log_sigmoid(gr)
    ri = _iota((L, L), 0)
    ci = _iota((L, L), 1)
    causal = ri >= ci
    tril = jnp.where(causal, 1.0, 0.0).astype(BF16)
    triu = jnp.where(ri <= ci, 1.0, 0.0).astype(BF16)
    b_c = _dot_sel(tril, fl_c)
    b_r = _dot_sel_r(fl_r, triu)
    one_col = jnp.where(_iota((L, LANES), 1) == 0, 1.0, 0.0)
    neg = jnp.float32(-jnp.inf)
    vv = v_ref[...]
    oo = o_ref[...]

    for h in range(ML_H):
        q_h = _bf(qk[:, h * DK:(h + 1) * DK] * (DK ** -0.5))
        k_f = qk[:, ML_QK + h * DK:ML_QK + (h + 1) * DK]
        k_h = _bf(k_f)
        v_ext = _bf(jnp.concatenate([vv[:, h * DV:(h + 1) * DV], one_col], axis=1))
        bc = b_c[:, ML_H + h:ML_H + h + 1]
        br = b_r[ML_H + h:ML_H + h + 1, :]
        il_c = gc[:, h:h + 1]
        il_r = gr[h:h + 1, :]
        m_prev = m_ref[h:h + 1, 0:1]
        C_prev = C_ref[h]

        D = jnp.where(causal, bc - br + il_r, neg)
        m_t = jnp.maximum(bc + m_prev, jnp.max(D, axis=-1, keepdims=True))
        S = _dot_nt(q_h, k_h) * jnp.exp(D - m_t)
        inter = jnp.exp(bc + m_prev - m_t)
        num = inter * _dot(q_h, _bf(C_prev)) + _dot(_bf(S), v_ext)
        den = num[:, DV:DV + 1]
        hh = num[:, :DV] / jnp.maximum(jnp.abs(den), jnp.exp(-m_t))
        ms = jnp.mean(hh * hh, axis=-1, keepdims=True)
        hn = hh * lax.rsqrt(ms + NORM_EPS) * nw_ref[:, h * DV:(h + 1) * DV]
        y_ref[:, h * DV:(h + 1) * DV] = _bf(hn * _sigmoid(oo[:, h * DV:(h + 1) * DV]))

        g_end = bc[L - 1:L]
        wst = g_end - bc + il_c
        m_new = jnp.maximum(g_end + m_prev, jnp.max(wst, axis=0, keepdims=True))
        kt = _bf(k_f * jnp.exp(wst - m_new))
        C_ref[h] = jnp.exp(g_end + m_prev - m_new) * C_prev + _dot_tn(kt, v_ext)
        m_ref[h:h + 1, :] = jnp.broadcast_to(m_new, (1, LANES))


def _mlstm(u, g_row, offs, B, T, prm):
    L = ML_L
    nc = T // L

    def col(off, w):
        return pl.BlockSpec((L, w), lambda b, c: (b * nc + c, off // w))

    def par(r, w):
        return pl.BlockSpec((r, w), lambda b, c: (0, 0))

    in_specs = [col(offs["ml_qk"], 2 * ML_QK), col(offs["ml_v"], ML_W), col(offs["ml_o"], ML_W),
                col(offs["ml_if"], LANES),
                pl.BlockSpec((SUBLANES, L), lambda b, c: (0, b * nc + c)),
                par(CONV_K, 2 * ML_QK), par(1, 2 * ML_QK), par(1, LANES), par(SUBLANES, 1), par(1, ML_W)]
    return pl.pallas_call(
        _mlstm_kernel,
        grid=(B, nc),
        in_specs=in_specs,
        out_specs=pl.BlockSpec((L, ML_W), lambda b, c: (b * nc + c, 0)),
        out_shape=jax.ShapeDtypeStruct((B * T, ML_W), BF16),
        scratch_shapes=[pltpu.VMEM((ML_H, ML_DK, ML_DV + LANES), F32),
                        pltpu.VMEM((SUBLANES, LANES), F32),
                        pltpu.VMEM((SUBLANES, 2 * ML_QK), F32)],
        compiler_params=_cparams(("parallel", "arbitrary")),
        name="mlstm",
    )(u, u, u, u, g_row, *prm)


def _hgrn_level_matrices():
    L = HG_L
    t = jnp.arange(L)[:, None]
    r = jnp.arange(L)[None, :]
    mats = [(r <= t)]
    for l in range(int(math.log2(L))):
        half = 1 << l
        base = (t // (2 * half)) * (2 * half)
        bnd = base + half - 1
        upper = (t - base) >= half
        m_up = upper & (r > bnd) & (r <= t)
        m_lo = (~upper) & (r > t) & (r <= bnd)
        mats.append(m_up | m_lo)
    return jnp.concatenate(mats, axis=0).astype(BF16)


def _hgrn_masks():
    L = HG_L
    t = jnp.arange(L)[:, None]
    s = jnp.arange(L)[None, :]
    ms = [(t == s)]
    for l in range(int(math.log2(L))):
        half = 1 << l
        same = (t // (2 * half)) == (s // (2 * half))
        ms.append(same & ((t % (2 * half)) >= half) & ((s % (2 * half)) < half))
    return jnp.stack(ms).astype(F32)


def _hgrn_kernel(q_ref, f_ref, i_ref, g_ref, lb_ref, lvl_ref, msk_ref, nw_ref, y_ref, S_ref):
    L = HG_L
    nl = int(math.log2(L))

    @pl.when(pl.program_id(1) == 0)
    def _():
        S_ref[...] = jnp.zeros_like(S_ref)

    lg = lb_ref[...]
    mx = jnp.max(lg, axis=0, keepdims=True)
    ex = jnp.exp(lg - mx)
    pr = ex / jnp.sum(ex, axis=0, keepdims=True)
    lb = (pr[0:1] + pr[1:2]) - pr[0:1]

    q = _silu(q_ref[...])
    fp = f_ref[...]
    iv = i_ref[...]
    a1 = jnp.log(lb)
    a2 = jnp.log1p(-lb) + _log_sigmoid(fp)
    log_f = jnp.maximum(a1, a2) + jnp.log1p(jnp.exp(-jnp.abs(a1 - a2)))
    k = (1.0 - lb) * _sigmoid(-fp)

    E = _dot_sel(lvl_ref[...], log_f)
    bcum = E[0:L]
    b_last = bcum[L - 1:L]
    qb = q * jnp.exp(bcum)
    kd = k * jnp.exp(b_last - bcum)
    e_last = jnp.exp(b_last)
    qs = [q]
    ks = [k]
    for l in range(nl):
        A = jnp.exp(E[(l + 1) * L:(l + 2) * L])
        qs.append(q * A)
        ks.append(k * A)
    gg = g_ref[...]

    sls = [slice(h * HG_HEAD, (h + 1) * HG_HEAD) for h in range(HG_H)]
    qsb = [_bf(x) for x in qs]
    ksb = [_bf(x) for x in ks]
    msk = [msk_ref[l] for l in range(nl + 1)]
    attn = [msk[0] * _dot_nt(qsb[0][:, sl], ksb[0][:, sl]) for sl in sls]
    for l in range(1, nl + 1):
        attn = [a + msk[l] * _dot_nt(qsb[l][:, sl], ksb[l][:, sl]) for a, sl in zip(attn, sls)]
    ivb = _bf(iv)
    qbb = _bf(qb)
    kdb = _bf(kd)
    St = [S_ref[h] for h in range(HG_H)]
    o = [_dot(_bf(a), ivb[:, sl]) + _dot_nt(qbb[:, sl], _bf(s)) for a, sl, s in zip(attn, sls, St)]
    Sn = [s * e_last[:, sl] + _dot_tn(ivb[:, sl], kdb[:, sl]) for s, sl in zip(St, sls)]
    for h in range(HG_H):
        S_ref[h] = Sn[h]
    for h, sl in enumerate(sls):
        ms = jnp.mean(o[h] * o[h], axis=-1, keepdims=True)
        on = o[h] * lax.rsqrt(ms + NORM_EPS) * nw_ref[:, sl]
        y_ref[:, sl] = _bf(on * _silu(gg[:, sl]))


def _hgrn(u, offs, B, T, prm):
    L = HG_L
    nc = T // L
    nl = int(math.log2(L))

    def col(off, w):
        return pl.BlockSpec((L, w), lambda b, c: (b * nc + c, off // w))

    in_specs = [col(offs["hg_q"], HG_W), col(offs["hg_f"], HG_W), col(offs["hg_i"], HG_W), col(offs["hg_g"], HG_W),
                pl.BlockSpec((2, HG_W), lambda b, c: (0, 0)),
                pl.BlockSpec(((nl + 1) * L, L), lambda b, c: (0, 0)),
                pl.BlockSpec((nl + 1, L, L), lambda b, c: (0, 0, 0)),
                pl.BlockSpec((1, HG_W), lambda b, c: (0, 0))]
    return pl.pallas_call(
        _hgrn_kernel,
        grid=(B, nc),
        in_specs=in_specs,
        out_specs=pl.BlockSpec((L, HG_W), lambda b, c: (b * nc + c, 0)),
        out_shape=jax.ShapeDtypeStruct((B * T, HG_W), BF16),
        scratch_shapes=[pltpu.VMEM((HG_H, HG_HEAD, HG_HEAD), F32)],
        compiler_params=_cparams(("parallel", "arbitrary")),
        name="hgrn2",
    )(u, u, u, u, *prm)


def _mamba_kernel(z_ref, x_ref, b_ref, c_ref, dtc_ref, dtr_ref,
                  cwx_ref, cbx_ref, cwb_ref, cbb_ref, cwc_ref, cbc_ref,
                  dbr_ref, dbc_ref, alr_ref, alc_ref, dsk_ref, nw_ref, exp_ref,
                  y_ref, S_ref, tx_ref, tb_ref, tc_ref):
    L = MB_L
    GW = MB_W // MB_G
    E = MB_H // MB_G

    @pl.when(pl.program_id(1) == 0)
    def _():
        S_ref[...] = jnp.zeros_like(S_ref)
        tx_ref[...] = jnp.zeros_like(tx_ref)
        tb_ref[...] = jnp.zeros_like(tb_ref)
        tc_ref[...] = jnp.zeros_like(tc_ref)

    def conv(ref, tail_ref, w_ref, bias_ref):
        x = ref[...]
        tail = tail_ref[...]
        acc = x * w_ref[CONV_K - 1:CONV_K] + bias_ref[...]
        for j in range(1, CONV_K):
            acc = acc + _shift_rows(tail, x, j) * w_ref[CONV_K - 1 - j:CONV_K - j]
        tail_ref[...] = x[L - SUBLANES:]
        return _silu(acc)

    xs = conv(x_ref, tx_ref, cwx_ref, cbx_ref)
    Bm = conv(b_ref, tb_ref, cwb_ref, cbb_ref)
    Cm = conv(c_ref, tc_ref, cwc_ref, cbc_ref)

    dt_c = _softplus(dtc_ref[...] + dbr_ref[...])
    dt_r = _softplus(dtr_ref[...] + dbc_ref[...])
    adt_c = dt_c * (-jnp.exp(alr_ref[...]))
    adt_r = dt_r * (-jnp.exp(alc_ref[...]))
    ri = _iota((L, L), 0)
    ci = _iota((L, L), 1)
    causal = ri >= ci
    tril = jnp.where(causal, 1.0, 0.0).astype(BF16)
    triu = jnp.where(ri <= ci, 1.0, 0.0).astype(BF16)
    ac_c = _dot_sel(tril, adt_c)
    ac_r = _dot_sel_r(adt_r, triu)
    ex = exp_ref[...]
    dt_full = _dot_sel_r(dt_c, ex)
    ac_full = _dot_sel_r(ac_c, ex)
    X = xs * dt_full
    a_last = ac_full[L - 1:L]
    dec_out = jnp.exp(ac_full)
    Xd = X * jnp.exp(a_last - ac_full)
    e_last = jnp.exp(a_last)
    neg = jnp.float32(-jnp.inf)
    lane_g = _idiv(_iota((1, GW), 1), MB_HEAD)
    zz = z_ref[...]

    for g in range(MB_G):
        gs = slice(g * GW, (g + 1) * GW)
        Bg = _bf(Bm[:, g * MB_N:(g + 1) * MB_N])
        Cg = _bf(Cm[:, g * MB_N:(g + 1) * MB_N])
        CB = _dot_nt(Cg, Bg)
        Xg = _bf(X[:, gs])
        Sg = S_ref[g]
        y = _dot(Cg, _bf(Sg)) * dec_out[:, gs]
        for e in range(E):
            h = g * E + e
            Lm = jnp.exp(jnp.where(causal, ac_c[:, h:h + 1] - ac_r[h:h + 1, :], neg))
            yd = _dot(_bf(CB * Lm), Xg)
            y = y + jnp.where(lane_g == e, yd, 0.0)
        S_ref[g] = Sg * e_last[:, gs] + _dot_tn(Bg, _bf(Xd[:, gs]))
        y = y + xs[:, gs] * dsk_ref[:, gs]
        y = y * _silu(zz[:, gs])
        ms = jnp.mean(y * y, axis=-1, keepdims=True)
        y_ref[:, gs] = _bf(y * lax.rsqrt(ms + NORM_EPS) * nw_ref[:, gs])


def _mamba(u, dt_row, offs, B, T, prm):
    L = MB_L
    nc = T // L
    GN = MB_G * MB_N

    def col(off, w):
        return pl.BlockSpec((L, w), lambda b, c: (b * nc + c, off // w))

    def par(r, w):
        return pl.BlockSpec((r, w), lambda b, c: (0, 0))

    in_specs = [col(offs["mb_z"], MB_W), col(offs["mb_x"], MB_W), col(offs["mb_b"], GN), col(offs["mb_c"], GN),
                col(offs["mb_dt"], LANES),
                pl.BlockSpec((MB_H, L), lambda b, c: (0, b * nc + c)),
                par(CONV_K, MB_W), par(1, MB_W), par(CONV_K, GN), par(1, GN), par(CONV_K, GN), par(1, GN),
                par(1, LANES), par(MB_H, 1), par(1, LANES), par(MB_H, 1), par(1, MB_W), par(1, MB_W),
                par(LANES, MB_W)]
    return pl.pallas_call(
        _mamba_kernel,
        grid=(B, nc),
        in_specs=in_specs,
        out_specs=pl.BlockSpec((L, MB_W), lambda b, c: (b * nc + c, 0)),
        out_shape=jax.ShapeDtypeStruct((B * T, MB_W), BF16),
        scratch_shapes=[pltpu.VMEM((MB_G, MB_N, MB_W // MB_G), F32),
                        pltpu.VMEM((SUBLANES, MB_W), F32),
                        pltpu.VMEM((SUBLANES, GN), F32),
                        pltpu.VMEM((SUBLANES, GN), F32)],
        compiler_params=_cparams(("parallel", "arbitrary")),
        name="mamba2",
    )(u, u, u, u, u, dt_row, *prm)


def _layout(segs, n_total):
    offs, cur = {}, 0
    for name, _, _, pw in segs:
        offs[name] = cur
        cur += pw
    assert cur <= n_total

    def pack(w, dtype=BF16):
        cols = []
        for _, s, wd, pw in segs:
            cols.append(w[:, s:s + wd])
            if pw > wd:
                cols.append(jnp.zeros((w.shape[0], pw - wd), w.dtype))
        if n_total > cur:
            cols.append(jnp.zeros((w.shape[0], n_total - cur), w.dtype))
        return jnp.concatenate(cols, axis=1).astype(dtype)

    return offs, pack


_EV_SEGS = [("rw_r", 0, RW_W, RW_W), ("rw_k", RW_W, RW_W, RW_W), ("rw_v", 2 * RW_W, RW_W, RW_W),
            ("ml_qk", RW_IN, 2 * ML_QK, 2 * ML_QK), ("ml_v", RW_IN + 2 * ML_QK, ML_W, ML_W),
            ("ml_o", RW_IN + 2 * ML_QK + ML_W, ML_W, ML_W),
            ("rw_dg", 3 * RW_W + RW_DECAY_LORA + RW_ICLR_LORA, RW_GATE_LORA, RW_GATE_LORA),
            ("rw_dw", 3 * RW_W, RW_DECAY_LORA, LANES),
            ("rw_da", 3 * RW_W + RW_DECAY_LORA, RW_ICLR_LORA, LANES),
            ("ml_if", RW_IN + 2 * ML_QK + 2 * ML_W, 2 * ML_H, LANES)]
_EV_N = 7168
_OD_SEGS = [("hg_q", 0, HG_W, HG_W), ("hg_f", HG_W, HG_W, HG_W), ("hg_i", 2 * HG_W, HG_W, HG_W),
            ("hg_g", 3 * HG_W, HG_W, HG_W),
            ("mb_z", HG_IN, MB_W, MB_W), ("mb_x", HG_IN + MB_W, MB_W, MB_W),
            ("mb_b", HG_IN + 2 * MB_W, MB_G * MB_N, MB_G * MB_N),
            ("mb_c", HG_IN + 2 * MB_W + MB_G * MB_N, MB_G * MB_N, MB_G * MB_N),
            ("mb_dt", HG_IN + MB_W + MB_CONV_W, MB_H, LANES)]
_OD_N = 7680


def _row(v, width=None):
    v = v.reshape(1, -1).astype(F32)
    if width is not None and v.shape[1] < width:
        v = jnp.pad(v, ((0, 0), (0, width - v.shape[1])))
    return v


def _pad_rows(w, rows):
    return jnp.pad(w, ((0, rows - w.shape[0]), (0, 0)))


def _even_layer(x, B, T, p):
    offs, pack = _layout(_EV_SEGS, _EV_N)
    s_if = RW_IN + 2 * ML_QK + 2 * ML_W
    u, g_row = _norm_matmul(x, _row(p["norm1"]), pack(p["w_in"]), _bf(p["w_in"][:, s_if:s_if + 2 * ML_H].T))
    mu = p["rw_mu"]
    o_dw, o_da, o_dg = 3 * RW_W, 3 * RW_W + RW_DECAY_LORA, 3 * RW_W + RW_DECAY_LORA + RW_ICLR_LORA
    rw_prm = [_row(mu[0:RW_W]), _row(mu[RW_W:2 * RW_W]), _row(mu[2 * RW_W:3 * RW_W]),
              _row(mu[o_dw:o_da], LANES), _row(mu[o_da:o_dg], LANES), _row(mu[o_dg:]),
              _row(p["rw_w0"]), _bf(_pad_rows(p["rw_w2"], LANES)),
              _row(p["rw_a0"]), _bf(_pad_rows(p["rw_a2"], LANES)), _bf(p["rw_g2"]),
              _row(p["rw_k_k"]), _row(p["rw_k_a"]), _row(p["rw_r_k"]), _row(p["rw_ln_w"]), _row(p["rw_ln_b"])]
    y_a = _rwkv(u, offs, B, T, rw_prm)
    gb =jnp.concatenate([p["ml_i_b"], p["ml_f_b"]]).astype(F32)
    ml_prm = [p["ml_conv_w"].astype(F32), _row(p["ml_conv_b"]), _row(gb, LANES), gb.reshape(-1, 1),
              _row(p["ml_norm_w"])]
    y_b = _mlstm(u, g_row, offs, B, T, ml_prm)
    x = _matmul_res(y_a, y_b, _bf(p["w_out"]), x)
    return _ffn(x, _row(p["norm2"]), _bf(p["ffn_w_gate"]), _bf(p["ffn_w_up"]), _bf(p["ffn_w_down"]))


def _odd_layer(x, B, T, p, final_w):
    offs, pack = _layout(_OD_SEGS, _OD_N)
    s_dt = HG_IN + MB_W + MB_CONV_W
    u, dt_row = _norm_matmul(x, _row(p["norm1"]), pack(p["w_in"]), _bf(p["w_in"][:, s_dt:s_dt + MB_H].T))
    hg_prm = [p["hg_lb_logits"].astype(F32), _hgrn_level_matrices(), _hgrn_masks(), _row(p["hg_norm_w"])]
    y_c = _hgrn(u, offs, B, T, hg_prm)
    cw, cb = p["mb_conv_w"].astype(F32), p["mb_conv_b"].astype(F32)
    GN = MB_G * MB_N
    expand = (jnp.arange(LANES)[:, None] == (jnp.arange(MB_W)[None, :] // MB_HEAD)).astype(BF16)
    mb_prm = [cw[:, :MB_W], _row(cb[:MB_W]), cw[:, MB_W:MB_W + GN], _row(cb[MB_W:MB_W + GN]),
              cw[:, MB_W + GN:], _row(cb[MB_W + GN:]),
              _row(p["mb_dt_bias"], LANES), p["mb_dt_bias"].astype(F32).reshape(-1, 1),
              _row(p["mb_A_log"], LANES), p["mb_A_log"].astype(F32).reshape(-1, 1),
              _row(jnp.repeat(p["mb_D"], MB_HEAD)), _row(p["mb_norm_w"]), expand]
    y_d = _mamba(u, dt_row, offs, B, T, mb_prm)
    x = _matmul_res(y_c, y_d, _bf(p["w_out"]), x)
    wr = jnp.pad(p["moe_router"].astype(F32), ((0, 0), (0, LANES - N_EXPERTS)))
    idx, prob, h = _router(x, _row(p["norm2"]), wr, 2 * x.shape[0] + N_EXPERTS * TM_MOE)
    pos, src_tok, tile_expert, tile_active = _route_plan(idx[:, :2], TM_MOE)
    xg = jnp.take(h, src_tok, axis=0, mode="clip")
    act = _moe_up(xg, tile_expert, tile_active, p["moe_w_gate"].astype(F32), p["moe_w_up"].astype(F32))
    yg = _moe_down(act, tile_expert, tile_active, p["moe_w_down"].astype(F32))
    y0 = jnp.take(yg, pos[:, 0], axis=0, mode="clip")
    y1 = jnp.take(yg, pos[:, 1], axis=0, mode="clip")
    return _combine(x, y0, y1, prob, _row(final_w))


def kernel(x, final_norm_w, hg_lb_logits, ev_norm1_w, ev_w_in, ev_w_out, rw_mu, rw_w0, rw_w2, rw_a0, rw_a2, rw_g2, rw_k_k, rw_k_a, rw_r_k, rw_ln_w, rw_ln_b, ml_conv_w, ml_conv_b, ml_i_b, ml_f_b, ml_norm_w, ev_norm2_w, ffn_w_gate, ffn_w_up, ffn_w_down, od_norm1_w, od_w_in, od_w_out, hg_norm_w, mb_conv_w, mb_conv_b, mb_dt_bias, mb_A_log, mb_D, mb_norm_w, od_norm2_w, moe_router, moe_w_gate, moe_w_up, moe_w_down):
    B, T, D = x.shape
    xf = x.reshape(B * T, D)
    ev = dict(norm1=ev_norm1_w[0], w_in=ev_w_in[0], w_out=ev_w_out[0], rw_mu=rw_mu[0], rw_w0=rw_w0[0],
              rw_w2=rw_w2[0], rw_a0=rw_a0[0], rw_a2=rw_a2[0], rw_g2=rw_g2[0], rw_k_k=rw_k_k[0],
              rw_k_a=rw_k_a[0], rw_r_k=rw_r_k[0], rw_ln_w=rw_ln_w[0], rw_ln_b=rw_ln_b[0],
              ml_conv_w=ml_conv_w[0], ml_conv_b=ml_conv_b[0], ml_i_b=ml_i_b[0], ml_f_b=ml_f_b[0],
              ml_norm_w=ml_norm_w[0], norm2=ev_norm2_w[0], ffn_w_gate=ffn_w_gate[0], ffn_w_up=ffn_w_up[0],
              ffn_w_down=ffn_w_down[0])
    od = dict(norm1=od_norm1_w[0], w_in=od_w_in[0], w_out=od_w_out[0], hg_lb_logits=hg_lb_logits,
              hg_norm_w=hg_norm_w[0], mb_conv_w=mb_conv_w[0], mb_conv_b=mb_conv_b[0], mb_dt_bias=mb_dt_bias[0],
              mb_A_log=mb_A_log[0], mb_D=mb_D[0], mb_norm_w=mb_norm_w[0], norm2=od_norm2_w[0],
              moe_router=moe_router[0], moe_w_gate=moe_w_gate[0], moe_w_up=moe_w_up[0],
              moe_w_down=moe_w_down[0])
    xf = _even_layer(xf, B, T, ev)
    xf = _odd_layer(xf, B, T, od, final_norm_w)
    return xf.reshape(B, T, D)
```

```python
import functools
import math

import jax
import jax.numpy as jnp
from jax import lax
from jax.experimental import pallas as pl
from jax.experimental.pallas import tpu as pltpu

F32 = jnp.float32
BF16 = jnp.bfloat16

D_MODEL = 2048
NORM_EPS = 1e-6
RW_HEAD = 64
RW_W = 1024
RW_H = RW_W // RW_HEAD
RW_DECAY_LORA = 96
RW_ICLR_LORA = 96
RW_GATE_LORA = 256
RW_IN = 3 * RW_W + RW_DECAY_LORA + RW_ICLR_LORA + RW_GATE_LORA
RW_LN_EPS = 64e-5
RW_DECAY_SCALE = math.exp(-0.5)
ML_W = 1024
ML_H = 4
ML_DV = ML_W // ML_H
ML_DK = ML_DV // 2
ML_QK = ML_H * ML_DK
ML_IN = 2 * ML_QK + 2 * ML_W + 2 * ML_H
HG_W = 1024
HG_HEAD = 128
HG_H = HG_W // HG_HEAD
HG_IN = 4 * HG_W
MB_W = 1024
MB_HEAD = 64
MB_H = MB_W // MB_HEAD
MB_G = 4
MB_N = 128
MB_CONV_W = MB_W + 2 * MB_G * MB_N
N_EXPERTS = 8

LANES = 128
SUBLANES = 8
VMEM_LIMIT = 56 * 1024 * 1024

TM_PROJ = 1024
TN_PROJ = (1024, 768, 512)
TM_DENSE = 1024
TM_FFN = 512
TF_FFN = 512
TM_MOE = 256
TF_MOE = 1408
TN_MOE = 1024
RW_L = 64
RW_BLK = 16
ML_L = 128
HG_L = 64
MB_L = 128
CONV_K = 4


def _cparams(sem):
    return pltpu.CompilerParams(dimension_semantics=sem, vmem_limit_bytes=VMEM_LIMIT)


def _proj_tn(n):
    return next(t for t in TN_PROJ if n % t == 0)


def _dot(a, b):
    return jnp.dot(a, b, preferred_element_type=F32)


def _dot_nt(a, b):
    return lax.dot_general(a, b, (((1,), (1,)), ((), ())), preferred_element_type=F32)


def _dot_tn(a, b):
    return lax.dot_general(a, b, (((0,), (0,)), ((), ())), preferred_element_type=F32)


def _bf(x):
    return x.astype(BF16)


def _split2(x):
    h = x.astype(BF16)
    l = (x - h.astype(F32)).astype(BF16)
    return h, l


def _split3(x):
    h = x.astype(BF16)
    r = x - h.astype(F32)
    m = r.astype(BF16)
    l = (r - m.astype(F32)).astype(BF16)
    return h, m, l


def _dot_sel(sel_bf16, x):
    h, m, l = _split3(x)
    return _dot(sel_bf16, h) + _dot(sel_bf16, m) + _dot(sel_bf16, l)


def _dot_sel_r(x, sel_bf16):
    h, m, l = _split3(x)
    return _dot(h, sel_bf16) + _dot(m, sel_bf16) + _dot(l, sel_bf16)


def _dot3(a, b):
    ah, al = _split2(a)
    bh, bl = _split2(b)
    return _dot(ah, bh) + _dot(ah, bl) + _dot(al, bh)


def _sigmoid(x):
    return 1.0 / (1.0 + jnp.exp(-x))


def _silu(x):
    return x * _sigmoid(x)


def _log_sigmoid(x):
    return -(jnp.maximum(-x, 0.0) + jnp.log1p(jnp.exp(-jnp.abs(x))))


def _softplus(x):
    return jnp.maximum(x, 0.0) + jnp.log1p(jnp.exp(-jnp.abs(x)))


def _iota(shape, dim):
    return lax.broadcasted_iota(jnp.int32, shape, dim)


def _idiv(x, d):
    sh = d.bit_length() - 1
    assert d == 1 << sh
    return lax.shift_right_logical(x, jnp.int32(sh))


def _shift_rows(tail, x, j):
    xc = jnp.concatenate([tail, x], axis=0)
    return pltpu.roll(xc, j, 0)[SUBLANES:]


def _norm_matmul_kernel(x_ref, nw_ref, w_ref, wt_ref, o_ref, ot_ref, h_ref):
    @pl.when(pl.program_id(1) == 0)
    def _():
        x = x_ref[...]
        ms = jnp.mean(x * x, axis=-1, keepdims=True)
        h_ref[...] = _bf(x * lax.rsqrt(ms + NORM_EPS) * nw_ref[...])
        ot_ref[...] = _dot_nt(wt_ref[...], h_ref[...])

    o_ref[...] = _dot(h_ref[...], w_ref[...])


def _norm_matmul(x, nw, w, wt):
    m, d = x.shape
    n = w.shape[1]
    r = wt.shape[0]
    tn = _proj_tn(n)
    return pl.pallas_call(
        _norm_matmul_kernel,
        grid=(m // TM_PROJ, n // tn),
        in_specs=[pl.BlockSpec((TM_PROJ, d), lambda i, j: (i, 0)),
                  pl.BlockSpec((1, d), lambda i, j: (0, 0)),
                  pl.BlockSpec((d, tn), lambda i, j: (0, j)),
                  pl.BlockSpec((r, d), lambda i, j: (0, 0))],
        out_specs=[pl.BlockSpec((TM_PROJ, tn), lambda i, j: (i, j)),
                   pl.BlockSpec((r, TM_PROJ), lambda i, j: (0, i))],
        out_shape=[jax.ShapeDtypeStruct((m, n), F32),
                   jax.ShapeDtypeStruct((r, m), F32)],
        scratch_shapes=[pltpu.VMEM((TM_PROJ, d), BF16)],
        compiler_params=_cparams(("parallel", "arbitrary")),
        name="norm_matmul",
    )(x, nw, w, wt)


def _matmul_res_kernel(ya_ref, yb_ref, wa_ref, wb_ref, r_ref, o_ref):
    o_ref[...] = r_ref[...] + _dot(ya_ref[...], wa_ref[...]) + _dot(yb_ref[...], wb_ref[...])


def _matmul_res(ya, yb, w, res):
    m, k = ya.shape
    n = w.shape[1]
    tn = _proj_tn(n)
    return pl.pallas_call(
        _matmul_res_kernel,
        grid=(m // TM_PROJ, n // tn),
        in_specs=[pl.BlockSpec((TM_PROJ, k), lambda i, j: (i, 0)),
                  pl.BlockSpec((TM_PROJ, k), lambda i, j: (i, 0)),
                  pl.BlockSpec((k, tn), lambda i, j: (0, j)),
                  pl.BlockSpec((k, tn), lambda i, j: (1, j)),
                  pl.BlockSpec((TM_PROJ, tn), lambda i, j: (i, j))],
        out_specs=pl.BlockSpec((TM_PROJ, tn), lambda i, j: (i, j)),
        out_shape=jax.ShapeDtypeStruct((m, n), F32),
        compiler_params=_cparams(("parallel", "arbitrary")),
        name="matmul_res",
    )(ya, yb, w, w, res)


def _ffn_kernel(x_ref, nw_ref, wg_ref, wu_ref, wd_ref, o_ref, h_ref):
    f = pl.program_id(1)

    @pl.when(f == 0)
    def _():
        x = x_ref[...]
        ms = jnp.mean(x * x, axis=-1, keepdims=True)
        h_ref[...] = _bf(x * lax.rsqrt(ms + NORM_EPS) * nw_ref[...])
        o_ref[...] = x

    h = h_ref[...]
    act = _silu(_dot(h, wg_ref[...])) * _dot(h, wu_ref[...])
    o_ref[...] += _dot(_bf(act), wd_ref[...])


def _ffn(x, nw, wg, wu, wd):
    m, d = x.shape
    f = wg.shape[1]
    return pl.pallas_call(
        _ffn_kernel,
        grid=(m // TM_DENSE, f // TF_FFN),
        in_specs=[pl.BlockSpec((TM_DENSE, d), lambda i, j: (i, 0)),
                  pl.BlockSpec((1, d), lambda i, j: (0, 0)),
                  pl.BlockSpec((d, TF_FFN), lambda i, j: (0, j)),
                  pl.BlockSpec((d, TF_FFN), lambda i, j: (0, j)),
                  pl.BlockSpec((TF_FFN, d), lambda i, j: (j, 0))],
        out_specs=pl.BlockSpec((TM_DENSE, d), lambda i, j: (i, 0)),
        out_shape=jax.ShapeDtypeStruct((m, d), F32),
        scratch_shapes=[pltpu.VMEM((TM_DENSE, d), BF16)],
        compiler_params=_cparams(("parallel", "arbitrary")),
        name="ffn_swiglu",
    )(x, nw, wg, wu, wd)


def _router_kernel(x_ref, nw_ref, wr_ref, i_ref, p_ref, h_ref, *, n_blocks):
    step = pl.program_id(0)

    @pl.when(step < n_blocks)
    def _():
        _router_block(x_ref, nw_ref, wr_ref, i_ref, p_ref, h_ref)

    @pl.when(step >= n_blocks)
    def _():
        h_ref[...] = jnp.zeros_like(h_ref)


def _router_block(x_ref, nw_ref, wr_ref, i_ref, p_ref, h_ref):
    x = x_ref[...]
    ms = jnp.mean(x * x, axis=-1, keepdims=True)
    h = x * lax.rsqrt(ms + NORM_EPS) * nw_ref[...]
    h_ref[...] = _bf(h)
    wr = wr_ref[...]
    hh, hl = _split2(h)
    wh, wl = _split2(wr)
    logits = _dot(hh, wh) + _dot(hh, wl) + _dot(hl, wh)
    lane = _iota(logits.shape, 1)
    neg = jnp.float32(-jnp.inf)
    logits = jnp.where(lane < N_EXPERTS, logits, neg)
    v1 = jnp.max(logits, axis=-1, keepdims=True)
    i1 = jnp.min(jnp.where(logits == v1, lane, LANES), axis=-1, keepdims=True)
    rest = jnp.where(lane == i1, neg, logits)
    v2 = jnp.max(rest, axis=-1, keepdims=True)
    i2 = jnp.min(jnp.where(rest == v2, lane, LANES), axis=-1, keepdims=True)
    e2 = jnp.exp(v2 - v1)
    p1 = 1.0 / (1.0 + e2)
    p2 = e2 / (1.0 + e2)
    i_ref[...] = jnp.where(lane == 0, i1, jnp.where(lane == 1, i2, 0))
    p_ref[...] = jnp.where(lane == 0, p1, jnp.where(lane == 1, p2, 0.0))


def _router(x, nw, wr, table_rows):
    m, d = x.shape
    nb = m // TM_FFN
    tok = lambda i: (jnp.minimum(i, nb - 1), 0)
    return pl.pallas_call(
        functools.partial(_router_kernel, n_blocks=nb),
        grid=(table_rows // TM_FFN,),
        in_specs=[pl.BlockSpec((TM_FFN, d), tok),
                  pl.BlockSpec((1, d), lambda i: (0, 0)),
                  pl.BlockSpec((d, LANES), lambda i: (0, 0))],
        out_specs=[pl.BlockSpec((TM_FFN, LANES), tok),
                   pl.BlockSpec((TM_FFN, LANES), tok),
                   pl.BlockSpec((TM_FFN, d), lambda i: (i, 0))],
        out_shape=[jax.ShapeDtypeStruct((m, LANES), jnp.int32),
                   jax.ShapeDtypeStruct((m, LANES), F32),
                   jax.ShapeDtypeStruct((table_rows, d), BF16)],
        compiler_params=_cparams(("arbitrary",)),
        name="moe_router",
    )(x, nw, wr)


def _route_plan(top_idx, tm):
    m = top_idx.shape[0]
    n_rows = 2 * m + N_EXPERTS * tm
    e_flat = top_idx.reshape(-1)
    onehot = (e_flat[:, None] == jnp.arange(N_EXPERTS, dtype=jnp.int32)[None, :]).astype(jnp.int32)
    rank = jnp.cumsum(onehot, axis=0) - onehot
    counts = jnp.sum(onehot, axis=0)
    padded = ((counts + tm - 1) // tm) * tm
    ends = jnp.cumsum(padded)
    off = ends - padded
    pos = jnp.sum(onehot * (off[None, :] + rank), axis=1)
    src_tok = jnp.zeros((n_rows,), jnp.int32).at[pos].set(jnp.arange(2 * m, dtype=jnp.int32) // 2,
                                                          unique_indices=True)
    tile_start = jnp.arange(n_rows // tm, dtype=jnp.int32) * tm
    tile_expert = jnp.minimum(jnp.sum((tile_start[:, None] >= ends[None, :]).astype(jnp.int32), axis=1),
                              N_EXPERTS - 1)
    tile_active = (tile_start < ends[-1]).astype(jnp.int32)
    return pos.reshape(m, 2), src_tok, tile_expert, tile_active


def _new_weight_block(te_ref, i):
    return (i == 0) | (te_ref[i] != te_ref[jnp.maximum(i - 1, 0)])


def _moe_up_kernel(te_ref, ta_ref, x_ref, wg_ref, wu_ref, a_ref, wgb_ref, wub_ref):
    i = pl.program_id(1)

    @pl.when(_new_weight_block(te_ref, i))
    def _():
        wgb_ref[...] = _bf(wg_ref[0])
        wub_ref[...] = _bf(wu_ref[0])

    @pl.when(ta_ref[i] == 1)
    def _():
        x = x_ref[...]
        act = _silu(_dot(x, wgb_ref[...])) * _dot(x, wub_ref[...])
        a_ref[...] = _bf(act)

    @pl.when(ta_ref[i] == 0)
    def _():
        a_ref[...] = jnp.zeros_like(a_ref)


def _moe_up(xg, tile_expert, tile_active, wg, wu):
    n_rows, d = xg.shape
    fe = wg.shape[2]
    wmap = lambda f, i, te, ta: (te[i], 0, f)
    grid_spec = pltpu.PrefetchScalarGridSpec(
        num_scalar_prefetch=2,
        grid=(fe // TF_MOE, n_rows // TM_MOE),
        in_specs=[pl.BlockSpec((TM_MOE, d), lambda f, i, te, ta: (i, 0)),
                  pl.BlockSpec((1, d, TF_MOE), wmap),
                  pl.BlockSpec((1, d, TF_MOE), wmap, pipeline_mode=pl.Buffered(1))],
        out_specs=pl.BlockSpec((TM_MOE, TF_MOE), lambda f, i, te, ta: (i, f)),
        scratch_shapes=[pltpu.VMEM((d, TF_MOE), BF16), pltpu.VMEM((d, TF_MOE), BF16)],
    )
    return pl.pallas_call(
        _moe_up_kernel,
        grid_spec=grid_spec,
        out_shape=jax.ShapeDtypeStruct((n_rows, fe), BF16),
        compiler_params=_cparams(("arbitrary", "arbitrary")),
        name="moe_up",
    )(tile_expert, tile_active, xg, wg, wu)


def _moe_down_kernel(te_ref, ta_ref, a0_ref, a1_ref, wd_ref, y_ref, wdb_ref, *, half):
    i = pl.program_id(1)

    @pl.when(_new_weight_block(te_ref, i))
    def _():
        wdb_ref[...] = _bf(wd_ref[0])

    @pl.when((ta_ref[i] == 1) & (i < half))
    def _():
        y_ref[...] = _bf(_dot(a0_ref[...], wdb_ref[...]))

    @pl.when((ta_ref[i] == 1) & (i >= half))
    def _():
        y_ref[...] = _bf(_dot(a1_ref[...], wdb_ref[...]))

    @pl.when(ta_ref[i] == 0)
    def _():
        y_ref[...] = jnp.zeros_like(y_ref)


def _moe_down(act0, act1, tile_expert, tile_active, wd):
    half_rows, fe = act0.shape
    half = half_rows // TM_MOE
    n_rows = 2 * half_rows
    d = wd.shape[2]
    grid_spec = pltpu.PrefetchScalarGridSpec(
        num_scalar_prefetch=2,
        grid=(d // TN_MOE, n_rows // TM_MOE),
        in_specs=[pl.BlockSpec((TM_MOE, fe), lambda n, i, te, ta: (jnp.minimum(i, half - 1), 0)),
                  pl.BlockSpec((TM_MOE, fe), lambda n, i, te, ta: (jnp.maximum(i - half, 0), 0)),
                  pl.BlockSpec((1, fe, TN_MOE), lambda n, i, te, ta: (te[i], 0, n))],
        out_specs=pl.BlockSpec((TM_MOE, TN_MOE), lambda n, i, te, ta: (i, n)),
        scratch_shapes=[pltpu.VMEM((fe, TN_MOE), BF16)],
    )
    return pl.pallas_call(
        functools.partial(_moe_down_kernel, half=half),
        grid_spec=grid_spec,
        out_shape=jax.ShapeDtypeStruct((n_rows, d), BF16),
        compiler_params=_cparams(("arbitrary", "arbitrary")),
        name="moe_down",
    )(tile_expert, tile_active, act0, act1, wd)


def _combine_kernel(x_ref, y0_ref, y1_ref, p_ref, fw_ref, o_ref):
    p = p_ref[...]
    y = x_ref[...] + p[:, 0:1] * y0_ref[...].astype(F32) + p[:, 1:2] * y1_ref[...].astype(F32)
    ms = jnp.mean(y * y, axis=-1, keepdims=True)
    o_ref[...] = y * lax.rsqrt(ms + NORM_EPS) * fw_ref[...]


def _combine(x, y0, y1, prob, final_w):
    m, d = x.shape
    row = pl.BlockSpec((TM_FFN, d), lambda i: (i, 0))
    return pl.pallas_call(
        _combine_kernel,
        grid=(m // TM_FFN,),
        in_specs=[row, row, row, pl.BlockSpec((TM_FFN, LANES), lambda i: (i, 0)),
                  pl.BlockSpec((1, d), lambda i: (0, 0))],
        out_specs=row,
        out_shape=jax.ShapeDtypeStruct((m, d), F32),
        compiler_params=_cparams(("parallel",)),
        name="moe_combine",
    )(x, y0, y1, prob, final_w)


def _rwkv_kernel(r_ref, k_ref, v_ref, dw_ref, da_ref, dg_ref,
                 mur_ref, muk_ref, muv_ref, mudw_ref, muda_ref, mudg_ref,
                 w0_ref, w2_ref, a0_ref, a2_ref, g2_ref, kk_ref, ka_ref, rk_ref, lnw_ref, lnb_ref,
                 o_ref,
                 H_ref, tr_ref, tk_ref, tv_ref, tdw_ref, tda_ref, tdg_ref):
    L = RW_L
    L2 = 2 * L

    @pl.when(pl.program_id(1) == 0)
    def _():
        H_ref[...] = jnp.zeros_like(H_ref)
        for t in (tr_ref, tk_ref, tv_ref, tdw_ref, tda_ref, tdg_ref):
            t[...] = jnp.zeros_like(t)

    def shift_mix(ref, tail, mu):
        x = ref[...]
        prev = _shift_rows(tail[...], x, 1)
        tail[...] = x[L - SUBLANES:]
        return x + (prev - x) * mu[...]

    r = shift_mix(r_ref, tr_ref, mur_ref)
    k = shift_mix(k_ref, tk_ref, muk_ref)
    v = shift_mix(v_ref, tv_ref, muv_ref)
    dw = shift_mix(dw_ref, tdw_ref, mudw_ref)
    da = shift_mix(da_ref, tda_ref, muda_ref)
    dg = shift_mix(dg_ref, tdg_ref, mudg_ref)

    log_w = -RW_DECAY_SCALE * _sigmoid(w0_ref[...] + _dot(_bf(jnp.tanh(dw)), w2_ref[...]))
    a = _sigmoid(a0_ref[...] + _dot(_bf(da), a2_ref[...]))
    g = _dot(_bf(_sigmoid(dg)), g2_ref[...])

    ri = _iota((LANES, LANES), 0)
    ci = _iota((LANES, LANES), 1)
    hsum = jnp.where(_idiv(ri, RW_HEAD) == _idiv(ci, RW_HEAD), 1.0, 0.0).astype(BF16)
    same_head = _idiv(ri, L) == _idiv(ci, L)
    tril_s = jnp.where(same_head & (ri > ci), 1.0, 0.0)
    tril_i = jnp.where(same_head & (ri >= ci), 1.0, 0.0)
    blk_d = jnp.where(_idiv(ri, RW_BLK) == _idiv(ci, RW_BLK), 1.0, 0.0)
    eye = jnp.where(ri == ci, 1.0, 0.0)
    cum = jnp.where(_iota((L, L), 0) >= _iota((L, L), 1), 1.0, 0.0).astype(BF16)
    lane = _iota((1, LANES), 1)
    m0 = lane < RW_HEAD

    def head_sum(x):
        h, l = _split2(x)
        return _dot(h, hsum) + _dot(l, hsum)

    kk = k * kk_ref[...]
    k2 = k * (1.0 + (a - 1.0) * ka_ref[...])
    cw = _dot_sel(cum, log_w)
    e_prev = jnp.exp(cw - log_w)
    e_inv = jnp.exp(-cw)
    e_cw = jnp.exp(cw)
    wl = cw[L - 1:L]
    e_end = jnp.exp(wl - cw)
    e_wl = jnp.exp(wl)
    rk2 = r * k2 * rk_ref[...]

    def stack(x):
        x0 = jnp.where(m0, x, 0.0)
        return jnp.concatenate([x0, x - x0], axis=0)

    def mm3(xs, ys):
        return _dot(xs[0], ys[0]) + _dot(xs[0], ys[1]) + _dot(xs[1], ys[0])

    NP = RW_H // 2
    sls = [slice(p * LANES, (p + 1) * LANES) for p in range(NP)]

    def each(fn, *lists):
        return [fn(*args) for args in zip(*lists)]

    kk_l = [kk[:, sl] for sl in sls]
    kk_l = each(lambda x: x * lax.rsqrt(jnp.maximum(head_sum(x * x), 1e-12)), kk_l)
    kka_l = [x * a[:, sl] for x, sl in zip(kk_l, sls)]
    At = [_bf(stack(-x * e_prev[:, sl])) for x, sl in zip(kk_l, sls)]
    Bt = [_bf(stack(x * e_inv[:, sl])) for x, sl in zip(kka_l, sls)]
    Kt = [_bf(stack(k2[:, sl] * e_inv[:, sl])) for sl in sls]
    Rt_f = [stack(r[:, sl] * e_cw[:, sl]) for sl in sls]
    Rt = each(_bf, Rt_f)
    Bh = [_bf(stack(x * e_end[:, sl])) for x, sl in zip(kka_l, sls)]
    Kh = [_bf(stack(k2[:, sl] * e_end[:, sl])) for sl in sls]
    Vs = [_bf(stack(v[:, sl])) for sl in sls]

    Mab = each(lambda x, y: _dot_nt(x, y) * tril_s, At, Bt)
    Mak = each(lambda x, y: _bf(_dot_nt(x, y) * tril_s), At, Kt)
    Arb = each(lambda x, y: _bf(_dot_nt(x, y) * tril_i), Rt, Bt)
    Ark = each(lambda x, y: _bf(_dot_nt(x, y) * tril_i), Rt, Kt)

    MD = each(lambda m: m * blk_d, Mab)
    Nn = each(lambda m, d: _split2(m - d), Mab, MD)
    P = each(lambda d: eye + d, MD)
    MDs = each(_split2, MD)
    S = each(mm3, MDs, MDs)
    for it in range(3):
        Ss = each(_split2, S)
        P = each(lambda p_, s_: p_ + mm3(_split2(p_), s_), P, Ss)
        if it < 2:
            S = each(mm3, Ss, Ss)
    Ps = each(_split2, P)
    X = each(mm3, Ps, Nn)
    Xs = each(_split2, X)
    X2 = each(lambda x: _split2(mm3(x, x)), Xs)
    Y = each(lambda x: eye + x, X)
    Y = each(lambda y_, x2: y_ + mm3(_split2(y_), x2), Y, X2)
    T = each(lambda y_, p_: _split2(mm3(_split2(y_), p_)), Y, Ps)

    P1 = each(lambda t, x: _bf(_dot(t[0], x) + _dot(t[1], x)), T, At)
    MV = each(lambda m, x: _split2(_dot(m, x)), Mak, Vs)
    P2 = each(lambda t, x: _bf(mm3(t, x)), T, MV)
    G = [eye * e_wl[:, sl] + _dot_tn(b, p1) for sl, b, p1 in zip(sls, Bh, P1)]
    J = each(lambda b, p2, kh, vs: _dot_tn(b, p2) + _dot_tn(kh, vs), Bh, P2, Kh, Vs)
    Q = each(lambda rf, ar, p1: rf + _dot(ar, p1), Rt_f, Arb, P1)
    Z = each(lambda ar, p2, ak, vs: _dot(ar, p2) + _dot(ak, vs), Arb, P2, Ark, Vs)

    Hs = [_split2(H_ref[p]) for p in range(NP)]
    Yst = each(lambda q, h, z: mm3(_split2(q), h) + z, Q, Hs, Z)
    Hn = each(lambda g_, h, j: mm3(_split2(g_), h) + j, G, Hs, J)
    for p in range(NP):
        H_ref[p] = Hn[p]
    y_l = each(lambda y_: y_[:L] + y_[L:], Yst)

    mean = each(lambda y_: head_sum(y_) * (1.0 / RW_HEAD), y_l)
    yc = each(lambda y_, m: y_ - m, y_l, mean)
    var = each(lambda c: head_sum(c * c) * (1.0 / RW_HEAD), yc)
    yn = [c * lax.rsqrt(vr + RW_LN_EPS) * lnw_ref[:, sl] + lnb_ref[:, sl] for c, vr, sl in zip(yc, var, sls)]
    bonus = [head_sum(rk2[:, sl]) * v[:, sl] for sl in sls]
    for p in range(NP):
        o_ref[:, sls[p]] = _bf((yn[p] + bonus[p]) * g[:, sls[p]])


def _rwkv(u, offs, B, T, prm):
    L = RW_L
    nc = T // L

    def col(off, w):
        return pl.BlockSpec((L, w), lambda b, c: (b * nc + c, off // w))

    def par(w, rows=1):
        return pl.BlockSpec((rows, w), lambda b, c: (0, 0))

    in_specs = [col(offs["rw_r"], RW_W), col(offs["rw_k"], RW_W), col(offs["rw_v"], RW_W),
                col(offs["rw_dw"], LANES), col(offs["rw_da"], LANES), col(offs["rw_dg"], RW_GATE_LORA),
                par(RW_W), par(RW_W), par(RW_W), par(LANES), par(LANES), par(RW_GATE_LORA),
                par(RW_W), par(RW_W, LANES), par(RW_W), par(RW_W, LANES), par(RW_W, RW_GATE_LORA),
                par(RW_W), par(RW_W), par(RW_W), par(RW_W), par(RW_W)]
    return pl.pallas_call(
        _rwkv_kernel,
        grid=(B, nc),
        in_specs=in_specs,
        out_specs=pl.BlockSpec((L, RW_W), lambda b, c: (b * nc + c, 0)),
        out_shape=jax.ShapeDtypeStruct((B * T, RW_W), BF16),
        scratch_shapes=[pltpu.VMEM((RW_H // 2, LANES, LANES), F32),
                        pltpu.VMEM((SUBLANES, RW_W), F32), pltpu.VMEM((SUBLANES, RW_W), F32),
                        pltpu.VMEM((SUBLANES, RW_W), F32), pltpu.VMEM((SUBLANES, LANES), F32),
                        pltpu.VMEM((SUBLANES, LANES), F32), pltpu.VMEM((SUBLANES, RW_GATE_LORA), F32)],
        compiler_params=_cparams(("parallel", "arbitrary")),
        name="rwkv7",
    )(u, u, u, u, u, u, *prm)


def _mlstm_kernel(qk_ref, v_ref, o_ref, gc_ref, gr_ref, cw_ref, cb_ref, gbr_ref, gbc_ref, nw_ref,
                  y_ref, C_ref, m_ref, tail_ref):
    L = ML_L
    DK, DV = ML_DK, ML_DV

    @pl.when(pl.program_id(1) == 0)
    def _():
        C_ref[...] = jnp.zeros_like(C_ref)
        m_ref[...] = jnp.zeros_like(m_ref)
        tail_ref[...] = jnp.zeros_like(tail_ref)

    x = qk_ref[...]
    tail = tail_ref[...]
    acc = x * cw_ref[CONV_K - 1:CONV_K] + cb_ref[...]
    for j in range(1, CONV_K):
        acc = acc + _shift_rows(tail, x, j) * cw_ref[CONV_K - 1 - j:CONV_K - j]
    tail_ref[...] = x[L - SUBLANES:]
    qk = _silu(acc)

    gc = gc_ref[...] + gbr_ref[...]
    gr = gr_ref[...] + gbc_ref[...]
    fl_c = _log_sigmoid(gc)
    fl_r = _log_sigmoid(gr)
    ri = _iota((L, L), 0)
    ci = _iota((L, L), 1)
    causal = ri >= ci
    tril = jnp.where(causal, 1.0, 0.0).astype(BF16)
    triu = jnp.where(ri <= ci, 1.0, 0.0).astype(BF16)
    b_c = _dot_sel(tril, fl_c)
    b_r = _dot_sel_r(fl_r, triu)
    one_col = jnp.where(_iota((L, LANES), 1) == 0, 1.0, 0.0)
    neg = jnp.float32(-jnp.inf)
    vv = v_ref[...]
    oo = o_ref[...]

    for h in range(ML_H):
        q_h = _bf(qk[:, h * DK:(h + 1) * DK] * (DK ** -0.5))
        k_f = qk[:, ML_QK + h * DK:ML_QK + (h + 1) * DK]
        k_h = _bf(k_f)
        v_ext = _bf(jnp.concatenate([vv[:, h * DV:(h + 1) * DV], one_col], axis=1))
        bc = b_c[:, ML_H + h:ML_H + h + 1]
        br = b_r[ML_H + h:ML_H + h + 1, :]
        il_c = gc[:, h:h + 1]
        il_r = gr[h:h + 1, :]
        m_prev = m_ref[h:h + 1, 0:1]
        C_prev = C_ref[h]

        D = jnp.where(causal, bc - br + il_r, neg)
        m_t = jnp.maximum(bc + m_prev, jnp.max(D, axis=-1, keepdims=True))
        S = _dot_nt(q_h, k_h) * jnp.exp(D - m_t)
        inter = jnp.exp(bc + m_prev - m_t)
        num = inter * _dot(q_h, _bf(C_prev)) + _dot(_bf(S), v_ext)
        den = num[:, DV:DV + 1]
        hh = num[:, :DV] / jnp.maximum(jnp.abs(den), jnp.exp(-m_t))
        ms = jnp.mean(hh * hh, axis=-1, keepdims=True)
        hn = hh * lax.rsqrt(ms + NORM_EPS) * nw_ref[:, h * DV:(h + 1) * DV]
        y_ref[:, h * DV:(h + 1) * DV] = _bf(hn * _sigmoid(oo[:, h * DV:(h + 1) * DV]))

        g_end = bc[L - 1:L]
        wst = g_end - bc + il_c
        m_new = jnp.maximum(g_end + m_prev, jnp.max(wst, axis=0, keepdims=True))
        kt = _bf(k_f * jnp.exp(wst - m_new))
        C_ref[h] = jnp.exp(g_end + m_prev - m_new) * C_prev + _dot_tn(kt, v_ext)
        m_ref[h:h + 1, :] = jnp.broadcast_to(m_new, (1, LANES))


def _mlstm(u, g_row, offs, B, T, prm):
    L = ML_L
    nc = T // L

    def col(off, w):
        return pl.BlockSpec((L, w), lambda b, c: (b * nc + c, off // w))

    def par(r, w):
        return pl.BlockSpec((r, w), lambda b, c: (0, 0))

    in_specs = [col(offs["ml_qk"], 2 * ML_QK), col(offs["ml_v"], ML_W), col(offs["ml_o"], ML_W),
                col(offs["ml_if"], LANES),
                pl.BlockSpec((SUBLANES, L), lambda b, c: (0, b * nc + c)),
                par(CONV_K, 2 * ML_QK), par(1, 2 * ML_QK), par(1, LANES), par(SUBLANES, 1), par(1, ML_W)]
    return pl.pallas_call(
        _mlstm_kernel,
        grid=(B, nc),
        in_specs=in_specs,
        out_specs=pl.BlockSpec((L, ML_W), lambda b, c: (b * nc + c, 0)),
        out_shape=jax.ShapeDtypeStruct((B * T, ML_W), BF16),
        scratch_shapes=[pltpu.VMEM((ML_H, ML_DK, ML_DV + LANES), F32),
                        pltpu.VMEM((SUBLANES, LANES), F32),
                        pltpu.VMEM((SUBLANES, 2 * ML_QK), F32)],
        compiler_params=_cparams(("parallel", "arbitrary")),
        name="mlstm",
    )(u, u, u, u, g_row, *prm)


def _hgrn_level_matrices():
    L = HG_L
    t = jnp.arange(L)[:, None]
    r = jnp.arange(L)[None, :]
    mats = [(r <= t)]
    for l in range(int(math.log2(L))):
        half = 1 << l
        base = (t // (2 * half)) * (2 * half)
        bnd = base + half - 1
        upper = (t - base) >= half
        m_up = upper & (r > bnd) & (r <= t)
        m_lo = (~upper) & (r > t) & (r <= bnd)
        mats.append(m_up | m_lo)
    return jnp.concatenate(mats, axis=0).astype(BF16)


def _hgrn_masks():
    L = HG_L
    t = jnp.arange(L)[:, None]
    s = jnp.arange(L)[None, :]
    ms = [(t == s)]
    for l in range(int(math.log2(L))):
        half = 1 << l
        same = (t // (2 * half)) == (s // (2 * half))
        ms.append(same & ((t % (2 * half)) >= half) & ((s % (2 * half)) < half))
    return jnp.stack(ms).astype(F32)


def _hgrn_kernel(q_ref, f_ref, i_ref, g_ref, lb_ref, lvl_ref, msk_ref, nw_ref, y_ref, S_ref):
    L = HG_L
    nl = int(math.log2(L))

    @pl.when(pl.program_id(1) == 0)
    def _():
        S_ref[...] = jnp.zeros_like(S_ref)

    lg = lb_ref[...]
    mx = jnp.max(lg, axis=0, keepdims=True)
    ex = jnp.exp(lg - mx)
    pr = ex / jnp.sum(ex, axis=0, keepdims=True)
    lb = (pr[0:1] + pr[1:2]) - pr[0:1]

    q = _silu(q_ref[...])
    fp = f_ref[...]
    iv = i_ref[...]
    a1 = jnp.log(lb)
    a2 = jnp.log1p(-lb) + _log_sigmoid(fp)
    log_f = jnp.maximum(a1, a2) + jnp.log1p(jnp.exp(-jnp.abs(a1 - a2)))
    k = (1.0 - lb) * _sigmoid(-fp)

    E = _dot_sel(lvl_ref[...], log_f)
    bcum = E[0:L]
    b_last = bcum[L - 1:L]
    qb = q * jnp.exp(bcum)
    kd = k * jnp.exp(b_last - bcum)
    e_last = jnp.exp(b_last)
    qs = [q]
    ks = [k]
    for l in range(nl):
        A = jnp.exp(E[(l + 1) * L:(l + 2) * L])
        qs.append(q * A)
        ks.append(k * A)
    gg = g_ref[...]

    sls = [slice(h * HG_HEAD, (h + 1) * HG_HEAD) for h in range(HG_H)]
    qsb = [_bf(x) for x in qs]
    ksb = [_bf(x) for x in ks]
    msk = [msk_ref[l] for l in range(nl + 1)]
    attn = [msk[0] * _dot_nt(qsb[0][:, sl], ksb[0][:, sl]) for sl in sls]
    for l in range(1, nl + 1):
        attn = [a + msk[l] * _dot_nt(qsb[l][:, sl], ksb[l][:, sl]) for a, sl in zip(attn, sls)]
    ivb = _bf(iv)
    qbb = _bf(qb)
    kdb = _bf(kd)
    St = [S_ref[h] for h in range(HG_H)]
    o = [_dot(_bf(a), ivb[:, sl]) + _dot_nt(qbb[:, sl], _bf(s)) for a, sl, s in zip(attn, sls, St)]
    Sn = [s * e_last[:, sl] + _dot_tn(ivb[:, sl], kdb[:, sl]) for s, sl in zip(St, sls)]
    for h in range(HG_H):
        S_ref[h] = Sn[h]
    for h, sl in enumerate(sls):
        ms = jnp.mean(o[h] * o[h], axis=-1, keepdims=True)
        on = o[h] * lax.rsqrt(ms + NORM_EPS) * nw_ref[:, sl]
        y_ref[:, sl] = _bf(on * _silu(gg[:, sl]))


def _hgrn(u, offs, B, T, prm):
    L = HG_L
    nc = T // L
    nl = int(math.log2(L))

    def col(off, w):
        return pl.BlockSpec((L, w), lambda b, c: (b * nc + c, off // w))

    in_specs = [col(offs["hg_q"], HG_W), col(offs["hg_f"], HG_W), col(offs["hg_i"], HG_W), col(offs["hg_g"], HG_W),
                pl.BlockSpec((2, HG_W), lambda b, c: (0, 0)),
                pl.BlockSpec(((nl + 1) * L, L), lambda b, c: (0, 0)),
                pl.BlockSpec((nl + 1, L, L), lambda b, c: (0, 0, 0)),
                pl.BlockSpec((1, HG_W), lambda b, c: (0, 0))]
    return pl.pallas_call(
        _hgrn_kernel,
        grid=(B, nc),
        in_specs=in_specs,
        out_specs=pl.BlockSpec((L, HG_W), lambda b, c: (b * nc + c, 0)),
        out_shape=jax.ShapeDtypeStruct((B * T, HG_W), BF16),
        scratch_shapes=[pltpu.VMEM((HG_H, HG_HEAD, HG_HEAD), F32)],
        compiler_params=_cparams(("parallel", "arbitrary")),
        name="hgrn2",
    )(u, u, u, u, *prm)


def _mamba_kernel(z_ref, x_ref, b_ref, c_ref, dtc_ref, dtr_ref,
                  cwx_ref, cbx_ref, cwb_ref, cbb_ref, cwc_ref, cbc_ref,
                  dbr_ref, dbc_ref, alr_ref, alc_ref, dsk_ref, nw_ref, exp_ref,
                  y_ref, S_ref, tx_ref, tb_ref, tc_ref):
    L = MB_L
    GW = MB_W // MB_G
    E = MB_H // MB_G

    @pl.when(pl.program_id(1) == 0)
    def _():
        S_ref[...] = jnp.zeros_like(S_ref)
        tx_ref[...] = jnp.zeros_like(tx_ref)
        tb_ref[...] = jnp.zeros_like(tb_ref)
        tc_ref[...] = jnp.zeros_like(tc_ref)

    def conv(ref, tail_ref, w_ref, bias_ref):
        x = ref[...]
        tail = tail_ref[...]
        acc = x * w_ref[CONV_K - 1:CONV_K] + bias_ref[...]
        for j in range(1, CONV_K):
            acc = acc + _shift_rows(tail, x, j) * w_ref[CONV_K - 1 - j:CONV_K - j]
        tail_ref[...] = x[L - SUBLANES:]
        return _silu(acc)

    xs = conv(x_ref, tx_ref, cwx_ref, cbx_ref)
    Bm = conv(b_ref, tb_ref, cwb_ref, cbb_ref)
    Cm = conv(c_ref, tc_ref, cwc_ref, cbc_ref)

    dt_c = _softplus(dtc_ref[...] + dbr_ref[...])
    dt_r = _softplus(dtr_ref[...] + dbc_ref[...])
    adt_c = dt_c * (-jnp.exp(alr_ref[...]))
    adt_r = dt_r * (-jnp.exp(alc_ref[...]))
    ri = _iota((L, L), 0)
    ci = _iota((L, L), 1)
    causal = ri >= ci
    tril = jnp.where(causal, 1.0, 0.0).astype(BF16)
    triu = jnp.where(ri <= ci, 1.0, 0.0).astype(BF16)
    ac_c = _dot_sel(tril, adt_c)
    ac_r = _dot_sel_r(adt_r, triu)
    ex = exp_ref[...]
    dt_full = _dot_sel_r(dt_c, ex)
    ac_full = _dot_sel_r(ac_c, ex)
    X = xs * dt_full
    a_last = ac_full[L - 1:L]
    dec_out = jnp.exp(ac_full)
    Xd = X * jnp.exp(a_last - ac_full)
    e_last = jnp.exp(a_last)
    neg = jnp.float32(-jnp.inf)
    lane_g = _idiv(_iota((1, GW), 1), MB_HEAD)
    zz = z_ref[...]

    for g in range(MB_G):
        gs = slice(g * GW, (g + 1) * GW)
        Bg = _bf(Bm[:, g * MB_N:(g + 1) * MB_N])
        Cg = _bf(Cm[:, g * MB_N:(g + 1) * MB_N])
        CB = _dot_nt(Cg, Bg)
        Xg = _bf(X[:, gs])
        Sg = S_ref[g]
        y = _dot(Cg, _bf(Sg)) * dec_out[:, gs]
        for e in range(E):
            h = g * E + e
            Lm = jnp.exp(jnp.where(causal, ac_c[:, h:h + 1] - ac_r[h:h + 1, :], neg))
            yd = _dot(_bf(CB * Lm), Xg)
            y = y + jnp.where(lane_g == e, yd, 0.0)
        S_ref[g] = Sg * e_last[:, gs] + _dot_tn(Bg, _bf(Xd[:, gs]))
        y = y + xs[:, gs] * dsk_ref[:, gs]
        y = y * _silu(zz[:, gs])
        ms = jnp.mean(y * y, axis=-1, keepdims=True)
        y_ref[:, gs] = _bf(y * lax.rsqrt(ms + NORM_EPS) * nw_ref[:, gs])


def _mamba(u, dt_row, offs, B, T, prm):
    L = MB_L
    nc = T // L
    GN = MB_G * MB_N

    def col(off, w):
        return pl.BlockSpec((L, w), lambda b, c: (b * nc + c, off // w))

    def par(r, w):
        return pl.BlockSpec((r, w), lambda b, c: (0, 0))

    in_specs = [col(offs["mb_z"], MB_W), col(offs["mb_x"], MB_W), col(offs["mb_b"], GN), col(offs["mb_c"], GN),
                col(offs["mb_dt"], LANES),
                pl.BlockSpec((MB_H, L), lambda b, c: (0, b * nc + c)),
                par(CONV_K, MB_W), par(1, MB_W), par(CONV_K, GN), par(1, GN), par(CONV_K, GN), par(1, GN),
                par(1, LANES), par(MB_H, 1), par(1, LANES), par(MB_H, 1), par(1, MB_W), par(1, MB_W),
                par(LANES, MB_W)]
    return pl.pallas_call(
        _mamba_kernel,
        grid=(B, nc),
        in_specs=in_specs,
        out_specs=pl.BlockSpec((L, MB_W), lambda b, c: (b * nc + c, 0)),
        out_shape=jax.ShapeDtypeStruct((B * T, MB_W), BF16),
        scratch_shapes=[pltpu.VMEM((MB_G, MB_N, MB_W // MB_G), F32),
                        pltpu.VMEM((SUBLANES, MB_W), F32),
                        pltpu.VMEM((SUBLANES, GN), F32),
                        pltpu.VMEM((SUBLANES, GN), F32)],
        compiler_params=_cparams(("parallel", "arbitrary")),
        name="mamba2",
    )(u, u, u, u, u, dt_row, *prm)


def _layout(segs, n_total):
    offs, cur = {}, 0
    for name, _, _, pw in segs:
        offs[name] = cur
        cur += pw
    assert cur <= n_total

    def pack(w, dtype=BF16):
        cols = []
        for _, s, wd, pw in segs:
            cols.append(w[:, s:s + wd])
            if pw > wd:
                cols.append(jnp.zeros((w.shape[0], pw - wd), w.dtype))
        if n_total > cur:
            cols.append(jnp.zeros((w.shape[0], n_total - cur), w.dtype))
        return jnp.concatenate(cols, axis=1).astype(dtype)

    return offs, pack


_EV_SEGS = [("rw_r", 0, RW_W, RW_W), ("rw_k", RW_W, RW_W, RW_W), ("rw_v", 2 * RW_W, RW_W, RW_W),
            ("ml_qk", RW_IN, 2 * ML_QK, 2 * ML_QK), ("ml_v", RW_IN + 2 * ML_QK, ML_W, ML_W),
            ("ml_o", RW_IN + 2 * ML_QK + ML_W, ML_W, ML_W),
            ("rw_dg", 3 * RW_W + RW_DECAY_LORA + RW_ICLR_LORA, RW_GATE_LORA, RW_GATE_LORA),
            ("rw_dw", 3 * RW_W, RW_DECAY_LORA, LANES),
            ("rw_da", 3 * RW_W + RW_DECAY_LORA, RW_ICLR_LORA, LANES),
            ("ml_if", RW_IN + 2 * ML_QK + 2 * ML_W, 2 * ML_H, LANES)]
_EV_N = 7168
_OD_SEGS = [("hg_q", 0, HG_W, HG_W), ("hg_f", HG_W, HG_W, HG_W), ("hg_i", 2 * HG_W, HG_W, HG_W),
            ("hg_g", 3 * HG_W, HG_W, HG_W),
            ("mb_z", HG_IN, MB_W, MB_W), ("mb_x", HG_IN + MB_W, MB_W, MB_W),
            ("mb_b", HG_IN + 2 * MB_W, MB_G * MB_N, MB_G * MB_N),
            ("mb_c", HG_IN + 2 * MB_W + MB_G * MB_N, MB_G * MB_N, MB_G * MB_N),
            ("mb_dt", HG_IN + MB_W + MB_CONV_W, MB_H, LANES)]
_OD_N = 7680


def _row(v, width=None):
    v = v.reshape(1, -1).astype(F32)
    if width is not None and v.shape[1] < width:
        v = jnp.pad(v, ((0, 0), (0, width - v.shape[1])))
    return v


def _pad_rows(w, rows):
    return jnp.pad(w, ((0, rows - w.shape[0]), (0, 0)))


def _even_layer(x, B, T, p):
    offs, pack = _layout(_EV_SEGS, _EV_N)
    s_if = RW_IN + 2 * ML_QK + 2 * ML_W
    u, g_row = _norm_matmul(x, _row(p["norm1"]), pack(p["w_in"]), _bf(p["w_in"][:, s_if:s_if + 2 * ML_H].T))
    mu = p["rw_mu"]
    o_dw, o_da, o_dg = 3 * RW_W, 3 * RW_W + RW_DECAY_LORA, 3 * RW_W + RW_DECAY_LORA + RW_ICLR_LORA
    rw_prm = [_row(mu[0:RW_W]), _row(mu[RW_W:2 * RW_W]), _row(mu[2 * RW_W:3 * RW_W]),
              _row(mu[o_dw:o_da], LANES), _row(mu[o_da:o_dg], LANES), _row(mu[o_dg:]),
              _row(p["rw_w0"]), _bf(_pad_rows(p["rw_w2"], LANES)),
              _row(p["rw_a0"]), _bf(_pad_rows(p["rw_a2"], LANES)), _bf(p["rw_g2"]),
              _row(p["rw_k_k"]), _row(p["rw_k_a"]), _row(p["rw_r_k"]), _row(p["rw_ln_w"]), _row(p["rw_ln_b"])]
    y_a = _rwkv(u, offs, B, T, rw_prm)
    gb =jnp.concatenate([p["ml_i_b"], p["ml_f_b"]]).astype(F32)
    ml_prm = [p["ml_conv_w"].astype(F32), _row(p["ml_conv_b"]), _row(gb, LANES), gb.reshape(-1, 1),
              _row(p["ml_norm_w"])]
    y_b = _mlstm(u, g_row, offs, B, T, ml_prm)
    x = _matmul_res(y_a, y_b, _bf(p["w_out"]), x)
    return _ffn(x, _row(p["norm2"]), _bf(p["ffn_w_gate"]), _bf(p["ffn_w_up"]), _bf(p["ffn_w_down"]))


def _odd_layer(x, B, T, p, final_w):
    offs, pack = _layout(_OD_SEGS, _OD_N)
    s_dt = HG_IN + MB_W + MB_CONV_W
    u, dt_row = _norm_matmul(x, _row(p["norm1"]), pack(p["w_in"]), _bf(p["w_in"][:, s_dt:s_dt + MB_H].T))
    hg_prm = [p["hg_lb_logits"].astype(F32), _hgrn_level_matrices(), _hgrn_masks(), _row(p["hg_norm_w"])]
    y_c = _hgrn(u, offs, B, T, hg_prm)
    cw, cb = p["mb_conv_w"].astype(F32), p["mb_conv_b"].astype(F32)
    GN = MB_G * MB_N
    expand = (jnp.arange(LANES)[:, None] == (jnp.arange(MB_W)[None, :] // MB_HEAD)).astype(BF16)
    mb_prm = [cw[:, :MB_W], _row(cb[:MB_W]), cw[:, MB_W:MB_W + GN], _row(cb[MB_W:MB_W + GN]),
              cw[:, MB_W + GN:], _row(cb[MB_W + GN:]),
              _row(p["mb_dt_bias"], LANES), p["mb_dt_bias"].astype(F32).reshape(-1, 1),
              _row(p["mb_A_log"], LANES), p["mb_A_log"].astype(F32).reshape(-1, 1),
              _row(jnp.repeat(p["mb_D"], MB_HEAD)), _row(p["mb_norm_w"]), expand]
    y_d = _mamba(u, dt_row, offs, B, T, mb_prm)
    x = _matmul_res(y_c, y_d, _bf(p["w_out"]), x)
    wr = jnp.pad(p["moe_router"].astype(F32), ((0, 0), (0, LANES - N_EXPERTS)))
    idx, prob, h = _router(x, _row(p["norm2"]), wr, 2 * x.shape[0] + N_EXPERTS * TM_MOE)
    pos, src_tok, tile_expert, tile_active = _route_plan(idx[:, :2], TM_MOE)
    wg, wu = p["moe_w_gate"].astype(F32), p["moe_w_up"].astype(F32)
    hr = src_tok.shape[0] // 2
    ht = tile_expert.shape[0] // 2
    acts = []
    for k in range(2):
        xg = jnp.take(h, src_tok[k * hr:(k + 1) * hr], axis=0, mode="clip")
        acts.append(_moe_up(xg, tile_expert[k * ht:(k + 1) * ht], tile_active[k * ht:(k + 1) * ht], wg, wu))
    yg = _moe_down(acts[0], acts[1], tile_expert, tile_active, p["moe_w_down"].astype(F32))
    y0 = jnp.take(yg, pos[:, 0], axis=0, mode="clip")
    y1 = jnp.take(yg, pos[:, 1], axis=0, mode="clip")
    return _combine(x, y0, y1, prob, _row(final_w))


def kernel(x, final_norm_w, hg_lb_logits, ev_norm1_w, ev_w_in, ev_w_out, rw_mu, rw_w0, rw_w2, rw_a0, rw_a2, rw_g2, rw_k_k, rw_k_a, rw_r_k, rw_ln_w, rw_ln_b, ml_conv_w, ml_conv_b, ml_i_b, ml_f_b, ml_norm_w, ev_norm2_w, ffn_w_gate, ffn_w_up, ffn_w_down, od_norm1_w, od_w_in, od_w_out, hg_norm_w, mb_conv_w, mb_conv_b, mb_dt_bias, mb_A_log, mb_D, mb_norm_w, od_norm2_w, moe_router, moe_w_gate, moe_w_up, moe_w_down):
    B, T, D = x.shape
    xf = x.reshape(B * T, D)
    ev = dict(norm1=ev_norm1_w[0], w_in=ev_w_in[0], w_out=ev_w_out[0], rw_mu=rw_mu[0], rw_w0=rw_w0[0],
              rw_w2=rw_w2[0], rw_a0=rw_a0[0], rw_a2=rw_a2[0], rw_g2=rw_g2[0], rw_k_k=rw_k_k[0],
              rw_k_a=rw_k_a[0], rw_r_k=rw_r_k[0], rw_ln_w=rw_ln_w[0], rw_ln_b=rw_ln_b[0],
              ml_conv_w=ml_conv_w[0], ml_conv_b=ml_conv_b[0], ml_i_b=ml_i_b[0], ml_f_b=ml_f_b[0],
              ml_norm_w=ml_norm_w[0], norm2=ev_norm2_w[0], ffn_w_gate=ffn_w_gate[0], ffn_w_up=ffn_w_up[0],
              ffn_w_down=ffn_w_down[0])
    od = dict(norm1=od_norm1_w[0], w_in=od_w_in[0], w_out=od_w_out[0], hg_lb_logits=hg_lb_logits,
              hg_norm_w=hg_norm_w[0], mb_conv_w=mb_conv_w[0], mb_conv_b=mb_conv_b[0], mb_dt_bias=mb_dt_bias[0],
              mb_A_log=mb_A_log[0], mb_D=mb_D[0], mb_norm_w=mb_norm_w[0], norm2=od_norm2_w[0],
              moe_router=moe_router[0], moe_w_gate=moe_w_gate[0], moe_w_up=moe_w_up[0],
              moe_w_down=moe_w_down[0])
    xf = _even_layer(xf, B, T, ev)
    xf = _odd_layer(xf, B, T, od, final_norm_w)
    return xf.reshape(B, T, D)
```
